```python
import math
import jax, jax.numpy as jnp
from jax import lax
import numpy as np

D_MODEL = 4096
BATCH = 2
SEQ = 8192
DEPTH = 1
DEC_BATCH = 8
DEC_SEQ = 16
PAST_LEN = 2048

CHUNK = 64
Q_BLOCK = 128
MIX_WIDTH = D_MODEL
ATTN_WIDTH = MIX_WIDTH // 2
V_HEAD_DIM = 128
ATTN_HEADS = ATTN_WIDTH // V_HEAD_DIM
QK_DIM = V_HEAD_DIM // 2
ATTN_SCALE = QK_DIM ** -0.5
SSD_INNER = MIX_WIDTH - ATTN_WIDTH
SSD_HEAD_DIM = 64
SSD_HEADS = SSD_INNER // SSD_HEAD_DIM
SSD_GROUPS = 8
D_STATE = 128
CONV_WIDTH = 4
CONV_DIM = SSD_INNER + 2 * SSD_GROUPS * D_STATE
SSD_CHUNK = CHUNK
D_FF = ((-(-8 * D_MODEL // 3) + 255) // 256) * 256
EPS = 1e-6
IN_DIM = 3 * ATTN_WIDTH + SSD_INNER + CONV_DIM + SSD_HEADS
IN_SPLITS = [ATTN_WIDTH, 2 * ATTN_WIDTH, 3 * ATTN_WIDTH,
             3 * ATTN_WIDTH + SSD_INNER, 3 * ATTN_WIDTH + SSD_INNER + CONV_DIM]

kernel_name = "hybrid_diffattn_ssd_stream_step"


def rms_norm(x, w):
    xf = x.astype(jnp.float32)
    y = xf * lax.rsqrt(jnp.mean(xf * xf, axis=-1, keepdims=True) + EPS)
    return (y * w.astype(jnp.float32)).astype(x.dtype)


def lambda_init(layer):
    return 0.8 - 0.6 * math.exp(-0.3 * layer)


def diff_mix(q1, q2, k1, k2, v, lam, mask):
    s1 = jnp.einsum("bqhd,bkhd->bhqk", q1, k1, preferred_element_type=jnp.float32) * ATTN_SCALE
    s2 = jnp.einsum("bqhd,bkhd->bhqk", q2, k2, preferred_element_type=jnp.float32) * ATTN_SCALE
    if mask is not None:
        s1 = jnp.where(mask, s1, -jnp.inf)
        s2 = jnp.where(mask, s2, -jnp.inf)
    a = jax.nn.softmax(s1, axis=-1) - lam * jax.nn.softmax(s2, axis=-1)
    return jnp.einsum("bhqk,bkhd->bqhd", a.astype(v.dtype), v)


def diff_attention_prompt(q1, q2, k1, k2, v, lam):
    b, s = q1.shape[0], q1.shape[1]
    key_chunk = jnp.arange(s) // CHUNK

    def one_block(start):
        qb1 = lax.dynamic_slice_in_dim(q1, start, Q_BLOCK, axis=1)
        qb2 = lax.dynamic_slice_in_dim(q2, start, Q_BLOCK, axis=1)
        q_chunk = (start + jnp.arange(Q_BLOCK)) // CHUNK
        mask = key_chunk[None, :] <= q_chunk[:, None]
        return diff_mix(qb1, qb2, k1, k2, v, lam, mask)

    out = lax.map(one_block, jnp.arange(0, s, Q_BLOCK))
    return jnp.moveaxis(out, 0, 1).reshape(b, s, ATTN_HEADS, V_HEAD_DIM)


def ssd_scan(x, dt, A, Bm, Cm, init_state, chunk):
    b, l, h, p = x.shape
    g, n = Bm.shape[2], Bm.shape[3]
    r = h // g
    c = l // chunk
    f32 = jnp.float32
    x = x.astype(f32)
    dt = dt.astype(f32)
    Bc = Bm.astype(f32).reshape(b, c, chunk, g, n)
    Cc = Cm.astype(f32).reshape(b, c, chunk, g, n)
    xdt = (x * dt[..., None]).reshape(b, c, chunk, g, r, p)
    dA = jnp.moveaxis((dt * A).reshape(b, c, chunk, g, r), 2, -1)
    a_cs = jnp.cumsum(dA, axis=-1)
    tri = jnp.tril(jnp.ones((chunk, chunk), dtype=bool))
    seg = a_cs[..., :, None] - a_cs[..., None, :]
    decay_in = jnp.exp(jnp.where(tri, seg, -jnp.inf))
    cb = jnp.einsum("bcqgn,bcsgn->bcgqs", Cc, Bc)
    y_diag = jnp.einsum("bcgqs,bcgrqs,bcsgrp->bcqgrp", cb, decay_in, xdt)
    decay_to_end = jnp.exp(a_cs[..., -1:] - a_cs)
    states = jnp.einsum("bcsgn,bcgrs,bcsgrp->bcgrpn", Bc, decay_to_end, xdt)
    chunk_decay = jnp.exp(a_cs[..., -1])

    def step(carry, inp):
        st, dec = inp
        return carry * dec[..., None, None] + st, carry

    init = init_state.astype(f32).reshape(b, g, r, p, n)
    final, prev = lax.scan(step, init, (jnp.moveaxis(states, 1, 0), jnp.moveaxis(chunk_decay, 1, 0)))
    prev = jnp.moveaxis(prev, 0, 1)
    y_off = jnp.einsum("bcqgn,bcgrpn,bcgrq->bcqgrp", Cc, prev, jnp.exp(a_cs))
    y = (y_diag + y_off).reshape(b, l, h, p)
    return y, final.reshape(b, h, p, n)


def ssd_branch(z, xbc, dt_raw, conv_prev, ssm_prev, p, chunk):
    b, l, _ = xbc.shape
    xpad = jnp.concatenate([conv_prev.astype(xbc.dtype), xbc], axis=1)
    conv = p["conv_b"]
    for i in range(CONV_WIDTH):
        conv = conv + p["conv_w"][i] * xpad[:, i:i + l]
    new_conv = xpad[:, xpad.shape[1] - (CONV_WIDTH - 1):]
    act = jax.nn.silu(conv)
    xs, Bm, Cm = jnp.split(act, [SSD_INNER, SSD_INNER + SSD_GROUPS * D_STATE], axis=-1)
    xs = xs.reshape(b, l, SSD_HEADS, SSD_HEAD_DIM)
    Bm = Bm.reshape(b, l, SSD_GROUPS, D_STATE)
    Cm = Cm.reshape(b, l, SSD_GROUPS, D_STATE)
    dt = jax.nn.softplus(dt_raw.astype(jnp.float32) + p["dt_bias"].astype(jnp.float32))
    A = -jnp.exp(p["A_log"].astype(jnp.float32))
    y, final = ssd_scan(xs, dt, A, Bm, Cm, ssm_prev, chunk)
    y = y + p["D_skip"].astype(jnp.float32)[:, None] * xs.astype(jnp.float32)
    y = y.reshape(b, l, SSD_INNER) * jax.nn.silu(z.astype(jnp.float32))
    yg = y.reshape(b, l, SSD_GROUPS, SSD_INNER // SSD_GROUPS)
    yg = yg * lax.rsqrt(jnp.mean(yg * yg, axis=-1, keepdims=True) + EPS)
    y = yg.reshape(b, l, SSD_INNER) * p["ssd_norm_w"].astype(jnp.float32)
    return y.astype(z.dtype), new_conv, final.astype(z.dtype)


def hybrid_layer(x, p, layer, past_k, past_v, conv_prev, ssm_prev, ssd_chunk):
    b, l, _ = x.shape
    h = rms_norm(x, p["norm1_w"])
    proj = h @ p["w_in"]
    q, k, v, z, xbc, dt_raw = jnp.split(proj, IN_SPLITS, axis=-1)
    q = q.reshape(b, l, ATTN_HEADS, 2 * QK_DIM)
    k = k.reshape(b, l, ATTN_HEADS, 2 * QK_DIM)
    v = v.reshape(b, l, ATTN_HEADS, V_HEAD_DIM)
    lam0 = lambda_init(layer)
    f32 = jnp.float32
    lam = (jnp.exp(jnp.sum(p["lambda_q1"].astype(f32) * p["lambda_k1"].astype(f32)))
           - jnp.exp(jnp.sum(p["lambda_q2"].astype(f32) * p["lambda_k2"].astype(f32))) + lam0)
    if past_k is None:
        o = diff_attention_prompt(q[..., :QK_DIM], q[..., QK_DIM:], k[..., :QK_DIM], k[..., QK_DIM:], v, lam)
    else:
        k_all = jnp.concatenate([past_k.astype(k.dtype), k], axis=1)
        v_all = jnp.concatenate([past_v.astype(v.dtype), v], axis=1)
        o = diff_mix(q[..., :QK_DIM], q[..., QK_DIM:], k_all[..., :QK_DIM], k_all[..., QK_DIM:], v_all, lam, None)
    o = rms_norm(o, p["subln_w"]) * (1.0 - lam0)
    attn_out = o.reshape(b, l, ATTN_WIDTH)
    ssd_out, new_conv, new_ssm = ssd_branch(z, xbc, dt_raw, conv_prev, ssm_prev, p, ssd_chunk)
    mix = jnp.concatenate([attn_out, ssd_out.astype(attn_out.dtype)], axis=-1)
    x = x + mix @ p["w_out"]
    h2 = rms_norm(x, p["norm2_w"])
    x = x + (jax.nn.silu(h2 @ p["w_gate"]) * (h2 @ p["w_up"])) @ p["w_down"]
    return x, k, v, new_conv, new_ssm


def setup_inputs(seed: int = 0) -> dict:
    key = jax.random.key(seed)
    ks = jax.random.split(key, 26)
    f32 = jnp.float32
    nrm = lambda k, shape, s: jax.random.normal(k, shape, f32) * s
    dt0 = jnp.exp(jax.random.uniform(ks[15], (DEPTH, SSD_HEADS), f32)
                  * (math.log(0.1) - math.log(0.001)) + math.log(0.001))
    return {
        "x_prompt": nrm(ks[0], (BATCH, SEQ, D_MODEL), 1.0),
        "x_sample": nrm(ks[1], (DEC_BATCH, DEC_SEQ, D_MODEL), 1.0),
        "cache_k": nrm(ks[2], (DEPTH, DEC_BATCH, PAST_LEN, ATTN_HEADS, 2 * QK_DIM), 1.0),
        "cache_v": nrm(ks[3], (DEPTH, DEC_BATCH, PAST_LEN, ATTN_HEADS, V_HEAD_DIM), 1.0),
        "state_conv": nrm(ks[4], (DEPTH, DEC_BATCH, CONV_WIDTH - 1, CONV_DIM), 1.0),
        "state_ssm": nrm(ks[5], (DEPTH, DEC_BATCH, SSD_HEADS, SSD_HEAD_DIM, D_STATE), 0.1),
        "norm1_w": 1.0 + nrm(ks[6], (DEPTH, D_MODEL), 0.01),
        "w_in": nrm(ks[7], (DEPTH, D_MODEL, IN_DIM), D_MODEL ** -0.5),
        "lambda_q1": nrm(ks[8], (DEPTH, QK_DIM), 0.1),
        "lambda_k1": nrm(ks[9], (DEPTH, QK_DIM), 0.1),
        "lambda_q2": nrm(ks[10], (DEPTH, QK_DIM), 0.1),
        "lambda_k2": nrm(ks[11], (DEPTH, QK_DIM), 0.1),
        "subln_w": 1.0 + nrm(ks[12], (DEPTH, V_HEAD_DIM), 0.01),
        "conv_w": nrm(ks[13], (DEPTH, CONV_WIDTH, CONV_DIM), CONV_WIDTH ** -0.5),
        "conv_b": nrm(ks[14], (DEPTH, CONV_DIM), 0.01),
        "dt_bias": dt0 + jnp.log(-jnp.expm1(-dt0)),
        "A_log": jnp.log(jax.random.uniform(ks[16], (DEPTH, SSD_HEADS), f32, 1.0, 16.0)),
        "D_skip": 1.0 + nrm(ks[17], (DEPTH, SSD_HEADS), 0.01),
        "ssd_norm_w": 1.0 + nrm(ks[18], (DEPTH, SSD_INNER), 0.01),
        "w_out": nrm(ks[19], (DEPTH, MIX_WIDTH, D_MODEL), MIX_WIDTH ** -0.5),
        "norm2_w": 1.0 + nrm(ks[20], (DEPTH, D_MODEL), 0.01),
        "w_gate": nrm(ks[21], (DEPTH, D_MODEL, D_FF), D_MODEL ** -0.5),
        "w_up": nrm(ks[22], (DEPTH, D_MODEL, D_FF), D_MODEL ** -0.5),
        "w_down": nrm(ks[23], (DEPTH, D_FF, D_MODEL), D_FF ** -0.5),
        "final_norm_w": 1.0 + nrm(ks[24], (D_MODEL,), 0.01),
    }


def reference(x_prompt, x_sample, cache_k, cache_v, state_conv, state_ssm,
              norm1_w, w_in, lambda_q1, lambda_k1, lambda_q2, lambda_k2, subln_w,
              conv_w, conv_b, dt_bias, A_log, D_skip, ssd_norm_w, w_out,
              norm2_w, w_gate, w_up, w_down, final_norm_w):
    xp, xs = x_prompt, x_sample
    kp_l, vp_l, cp_l, sp_l = [], [], [], []
    ks_l, vs_l, cs_l, ss_l = [], [], [], []
    for layer in range(DEPTH):
        p = {
            "norm1_w": norm1_w[layer], "w_in": w_in[layer],
            "lambda_q1": lambda_q1[layer], "lambda_k1": lambda_k1[layer],
            "lambda_q2": lambda_q2[layer], "lambda_k2": lambda_k2[layer],
            "subln_w": subln_w[layer], "conv_w": conv_w[layer], "conv_b": conv_b[layer],
            "dt_bias": dt_bias[layer], "A_log": A_log[layer], "D_skip": D_skip[layer],
            "ssd_norm_w": ssd_norm_w[layer], "w_out": w_out[layer], "norm2_w": norm2_w[layer],
            "w_gate": w_gate[layer], "w_up": w_up[layer], "w_down": w_down[layer],
        }
        b_p = xp.shape[0]
        conv0 = jnp.zeros((b_p, CONV_WIDTH - 1, CONV_DIM), xp.dtype)
        ssm0 = jnp.zeros((b_p, SSD_HEADS, SSD_HEAD_DIM, D_STATE), jnp.float32)
        xp, kp, vp, cp, sp = hybrid_layer(xp, p, layer, None, None, conv0, ssm0, SSD_CHUNK)
        xs, kn, vn, cn, sn = hybrid_layer(xs, p, layer, cache_k[layer], cache_v[layer],
                                          state_conv[layer], state_ssm[layer], xs.shape[1])
        kp_l.append(kp); vp_l.append(vp); cp_l.append(cp); sp_l.append(sp)
        ks_l.append(kn); vs_l.append(vn); cs_l.append(cn); ss_l.append(sn)
    y_prompt = rms_norm(xp, final_norm_w)
    y_sample = rms_norm(xs, final_norm_w)
    return (y_prompt, y_sample,
            jnp.stack(kp_l), jnp.stack(vp_l), jnp.stack(cp_l), jnp.stack(sp_l),
            jnp.stack(ks_l), jnp.stack(vs_l), jnp.stack(cs_l), jnp.stack(ss_l))
```

```python
import functools
import math

import jax
import jax.numpy as jnp
import numpy as np
from jax import lax
from jax.experimental import pallas as pl
from jax.experimental.pallas import tpu as pltpu

F32 = jnp.float32
BF16 = jnp.bfloat16

CHUNK = 64
V_HEAD_DIM = 128
QK_DIM = V_HEAD_DIM // 2
ATTN_SCALE = QK_DIM ** -0.5
SSD_HEAD_DIM = 64
SSD_GROUPS = 8
D_STATE = 128
CONV_WIDTH = 4
EPS = 1e-6

V7X_LANES = 128
V7X_SUBLANES = 8
V7X_VMEM_BYTES = 64 * 1024 * 1024
VMEM_LIMIT_BYTES = V7X_VMEM_BYTES - 8 * 1024 * 1024

NT_DIMS = (((1,), (1,)), ((), ()))


def _cparams(n_axes):
    return pltpu.CompilerParams(
        dimension_semantics=("arbitrary",) * n_axes, vmem_limit_bytes=VMEM_LIMIT_BYTES)


def _sigmoid(x):
    return 1.0 / (1.0 + jnp.exp(-x))


def _lambda_init(layer):
    return 0.8 - 0.6 * math.exp(-0.3 * layer)


def _rmsnorm_body(x_ref, w_ref, o_ref):
    x = x_ref[...]
    ms = jnp.mean(x * x, axis=-1, keepdims=True)
    o_ref[...] = (x * lax.rsqrt(ms + EPS) * w_ref[...]).astype(o_ref.dtype)


def _rmsnorm(x, w, out_dtype, tr):
    m, d = x.shape
    return pl.pallas_call(
        _rmsnorm_body,
        grid=(m // tr,),
        in_specs=[pl.BlockSpec((tr, d), lambda i: (i, 0)),
                  pl.BlockSpec((1, d), lambda i: (0, 0))],
        out_specs=pl.BlockSpec((tr, d), lambda i: (i, 0)),
        out_shape=jax.ShapeDtypeStruct((m, d), out_dtype),
        compiler_params=_cparams(1),
        name="rmsnorm",
    )(x, w.reshape(1, d).astype(F32))


def _mm_body(*refs, n_pairs, scales, has_res):
    a_refs = refs[:n_pairs]
    b_refs = refs[n_pairs:2 * n_pairs]
    pos = 2 * n_pairs
    res_ref = refs[pos] if has_res else None
    out_refs = refs[pos + int(has_res):]
    acc = None
    for a_ref, b_ref in zip(a_refs, b_refs):
        d = jnp.dot(a_ref[...], b_ref[...], preferred_element_type=F32)
        acc = d if acc is None else acc + d
    if has_res:
        acc = acc + res_ref[...]
    for o_ref, s in zip(out_refs, scales):
        o_ref[...] = (acc if s == 1.0 else acc * s).astype(o_ref.dtype)


def _matmul(a_list, b_list, *, n, tm, tn, outs, b_row_blocks=None, b_col_off=0,
            residual=None, name="matmul"):
    m = a_list[0].shape[0]
    n_pairs = len(a_list)
    if b_row_blocks is None:
        b_row_blocks = [0] * n_pairs
    assert b_col_off % tn == 0 and n % tn == 0 and m % tm == 0
    col_blk = b_col_off // tn
    in_specs = []
    for a in a_list:
        in_specs.append(pl.BlockSpec((tm, a.shape[1]), lambda i, j: (i, 0)))
    for a, rb in zip(a_list, b_row_blocks):
        in_specs.append(pl.BlockSpec((a.shape[1], tn), lambda i, j, rb=rb: (rb, col_blk + j)))
    args = list(a_list) + list(b_list)
    if residual is not None:
        in_specs.append(pl.BlockSpec((tm, tn), lambda i, j: (i, j)))
        args.append(residual)
    out_shape = [jax.ShapeDtypeStruct((m, n), dt) for dt, _ in outs]
    out_specs = [pl.BlockSpec((tm, tn), lambda i, j: (i, j)) for _ in outs]
    res = pl.pallas_call(
        functools.partial(_mm_body, n_pairs=n_pairs, scales=tuple(s for _, s in outs),
                          has_res=residual is not None),
        grid=(m // tm, n // tn),
        in_specs=in_specs,
        out_specs=out_specs,
        out_shape=out_shape,
        compiler_params=_cparams(2),
        name=name,
    )(*args)
    return res


def _ffn_up_body(h_ref, wg_ref, wu_ref, o_ref):
    h = h_ref[...]
    g = jnp.dot(h, wg_ref[...], preferred_element_type=F32)
    u = jnp.dot(h, wu_ref[...], preferred_element_type=F32)
    o_ref[...] = (g * _sigmoid(g) * u).astype(o_ref.dtype)


def _ffn_up(h, wg, wu, *, tm, tn):
    m, d = h.shape
    n = wg.shape[1]
    return pl.pallas_call(
        _ffn_up_body,
        grid=(m // tm, n // tn),
        in_specs=[pl.BlockSpec((tm, d), lambda i, j: (i, 0)),
                  pl.BlockSpec((d, tn), lambda i, j: (0, j)),
                  pl.BlockSpec((d, tn), lambda i, j: (0, j))],
        out_specs=pl.BlockSpec((tm, tn), lambda i, j: (i, j)),
        out_shape=jax.ShapeDtypeStruct((m, n), BF16),
        compiler_params=_cparams(2),
        name="ffn_up",
    )(h, wg, wu)


def _mm_acc_body(a_ref, b_ref, r_ref, o_ref):
    k = pl.program_id(2)
    d = jnp.dot(a_ref[...], b_ref[...], preferred_element_type=F32)

    @pl.when(k == 0)
    def _first():
        o_ref[...] = r_ref[...] + d

    @pl.when(k > 0)
    def _rest():
        o_ref[...] += d


def _matmul_acc(a, b, residual, *, tm, tn, tk):
    m, kdim = a.shape
    n = b.shape[1]
    assert m % tm == 0 and n % tn == 0 and kdim % tk == 0
    return pl.pallas_call(
        _mm_acc_body,
        grid=(m // tm, n // tn, kdim // tk),
        in_specs=[pl.BlockSpec((tm, tk), lambda i, j, k: (i, k)),
                  pl.BlockSpec((tk, tn), lambda i, j, k: (k, j)),
                  pl.BlockSpec((tm, tn), lambda i, j, k: (i, j))],
        out_specs=pl.BlockSpec((tm, tn), lambda i, j, k: (i, j)),
        out_shape=jax.ShapeDtypeStruct((m, n), F32),
        compiler_params=_cparams(3),
        name="ffn_down",
    )(a, b, residual)


def _lambda_value(lq1, lk1, lq2, lk2, lam0):
    d1 = jnp.sum(lq1[...] * lk1[...], axis=-1, keepdims=True)
    d2 = jnp.sum(lq2[...] * lk2[...], axis=-1, keepdims=True)
    return jnp.exp(d1) - jnp.exp(d2) + lam0


def _subln(o, w, lam0):
    ms = jnp.mean(o * o, axis=-1, keepdims=True)
    return (o * lax.rsqrt(ms + EPS) * w) * (1.0 - lam0)


def _attn_prompt_body(lq1, lk1, lq2, lk2, w_ref, q_ref, k_ref, v_ref, o_ref,
                      k1_scr, k2_scr, vt_scr, m_scr, l_scr, acc_scr, *, t, lam0):
    qi = pl.program_id(2)
    n_blk = k1_scr.shape[0]

    @pl.when(qi == 0)
    def _prepare_head():
        first_half = lax.broadcasted_iota(jnp.int32, (t, V_HEAD_DIM), 1) < QK_DIM
        for c in range(n_blk):
            kb = k_ref[pl.ds(c * t, t), :]
            zero = jnp.zeros_like(kb)
            k1_scr[c] = jnp.where(first_half, kb, zero)
            k2_scr[c] = jnp.where(first_half, zero, kb)
            vt_scr[c] = v_ref[pl.ds(c * t, t), :].T

    m_scr[...] = jnp.full(m_scr.shape, -jnp.inf, F32)
    l_scr[...] = jnp.zeros(l_scr.shape, F32)
    acc_scr[...] = jnp.zeros(acc_scr.shape, F32)
    q = q_ref[...]

    def one_block(j, masked):
        vt = vt_scr[j]
        for idx, k_scr in enumerate((k1_scr, k2_scr)):
            s = lax.dot_general(k_scr[j], q, NT_DIMS, preferred_element_type=F32)
            if masked:
                key_chunk = lax.broadcasted_iota(jnp.int32, (t, t), 0) // CHUNK
                qry_chunk = lax.broadcasted_iota(jnp.int32, (t, t), 1) // CHUNK
                s = jnp.where(key_chunk <= qry_chunk, s, -jnp.inf)
            m_old = m_scr[idx]
            m_new = jnp.maximum(m_old, jnp.max(s, axis=0, keepdims=True))
            alpha = jnp.exp(m_old - m_new)
            p = jnp.exp(s - m_new)
            l_scr[idx] = alpha * l_scr[idx] + jnp.sum(p, axis=0, keepdims=True)
            acc_scr[idx] = alpha * acc_scr[idx] + jnp.dot(
                vt, p.astype(BF16), preferred_element_type=F32)
            m_scr[idx] = m_new

    def loop_body(j, carry):
        one_block(j, False)
        return carry

    lax.fori_loop(0, qi, loop_body, 0)
    one_block(qi, True)

    lam = _lambda_value(lq1, lk1, lq2, lk2, lam0)
    o_t = acc_scr[0] * (1.0 / l_scr[0]) - lam * (acc_scr[1] * (1.0 / l_scr[1]))
    o_ref[...] = _subln(o_t.T, w_ref[...], lam0).astype(o_ref.dtype)


def _attn_prompt(q, k, v, lams, subln_w, *, batch, seq, heads, t, lam0):
    nq = seq // t
    lam_specs = [pl.BlockSpec((1, QK_DIM), lambda b, h, i: (0, 0))] * 4
    return pl.pallas_call(
        functools.partial(_attn_prompt_body, t=t, lam0=lam0),
        grid=(batch, heads, nq),
        in_specs=lam_specs + [
            pl.BlockSpec((1, V_HEAD_DIM), lambda b, h, i: (0, 0)),
            pl.BlockSpec((t, V_HEAD_DIM), lambda b, h, i: (b * nq + i, h)),
            pl.BlockSpec((seq, V_HEAD_DIM), lambda b, h, i: (b, h)),
            pl.BlockSpec((seq, V_HEAD_DIM), lambda b, h, i: (b, h)),
        ],
        out_specs=pl.BlockSpec((t, V_HEAD_DIM), lambda b, h, i: (b * nq + i, h)),
        out_shape=jax.ShapeDtypeStruct(q.shape, BF16),
        scratch_shapes=[
            pltpu.VMEM((nq, t, V_HEAD_DIM), BF16),
            pltpu.VMEM((nq, t, V_HEAD_DIM), BF16),
            pltpu.VMEM((nq, V_HEAD_DIM, t), BF16),
            pltpu.VMEM((2, 1, t), F32),
            pltpu.VMEM((2, 1, t), F32),
            pltpu.VMEM((2, V_HEAD_DIM, t), F32),
        ],
        compiler_params=_cparams(3),
        name="attn_prompt",
    )(*lams, subln_w, q, k, v)


def _attn_decode_body(lq1, lk1, lq2, lk2, w_ref, q_ref, kn_ref, vn_ref, kc_ref, vc_ref, o_ref,
                      *, n_new, lam0):
    q = q_ref[...]
    first_half = lax.broadcasted_iota(jnp.int32, q.shape, 1) < QK_DIM
    zero = jnp.zeros_like(q)
    kc = kc_ref[...].astype(BF16)
    vc = vc_ref[...].astype(BF16)
    kn = kn_ref[...]
    vn = vn_ref[...]
    new_valid = lax.broadcasted_iota(jnp.int32, (q.shape[0], kn.shape[0]), 1) < n_new
    outs = []
    for qh in (jnp.where(first_half, q, zero), jnp.where(first_half, zero, q)):
        sc = lax.dot_general(qh, kc, NT_DIMS, preferred_element_type=F32)
        sn = lax.dot_general(qh, kn, NT_DIMS, preferred_element_type=F32)
        sn = jnp.where(new_valid, sn, -jnp.inf)
        m = jnp.maximum(jnp.max(sc, axis=-1, keepdims=True), jnp.max(sn, axis=-1, keepdims=True))
        pc = jnp.exp(sc - m)
        pn = jnp.exp(sn - m)
        l = jnp.sum(pc, axis=-1, keepdims=True) + jnp.sum(pn, axis=-1, keepdims=True)
        o = (jnp.dot(pc.astype(BF16), vc, preferred_element_type=F32)
             + jnp.dot(pn.astype(BF16), vn, preferred_element_type=F32))
        outs.append(o * (1.0 / l))
    lam = _lambda_value(lq1, lk1, lq2, lk2, lam0)
    o_ref[...] = _subln(outs[0] - lam * outs[1], w_ref[...], lam0).astype(o_ref.dtype)


def _attn_decode(q, k_new, v_new, cache_k, cache_v, lams, subln_w, *, batch, rows, heads, lam0):
    past = cache_k.shape[1]
    pad_rows = k_new.shape[1]
    lam_specs = [pl.BlockSpec((1, QK_DIM), lambda b, h: (0, 0))] * 4
    return pl.pallas_call(
        functools.partial(_attn_decode_body, n_new=rows, lam0=lam0),
        grid=(batch, heads),
        in_specs=lam_specs + [
            pl.BlockSpec((1, V_HEAD_DIM), lambda b, h: (0, 0)),
            pl.BlockSpec((rows, V_HEAD_DIM), lambda b, h: (b, h)),
            pl.BlockSpec((None, pad_rows, V_HEAD_DIM), lambda b, h: (b, 0, h)),
            pl.BlockSpec((None, pad_rows, V_HEAD_DIM), lambda b, h: (b, 0, h)),
            pl.BlockSpec((None, past, V_HEAD_DIM), lambda b, h: (b, 0, h)),
            pl.BlockSpec((None, past, V_HEAD_DIM), lambda b, h: (b, 0, h)),
        ],
        out_specs=pl.BlockSpec((rows, V_HEAD_DIM), lambda b, h: (b, h)),
        out_shape=jax.ShapeDtypeStruct(q.shape, BF16),
        compiler_params=_cparams(2),
        name="attn_decode",
    )(*lams, subln_w, q, k_new, v_new, cache_k, cache_v)


def _split3(x):
    hi = x.astype(BF16)
    r1 = x - hi.astype(F32)
    mid = r1.astype(BF16)
    lo = (r1 - mid.astype(F32)).astype(BF16)
    return hi, mid, lo


def _ssd_body(z_ref, x_ref, bc_ref, dt_ref, tail_ref, st0_ref, cw_ref, cb_ref, dtb_ref, alog_ref,
              dsk_ref, nw_ref, e_ref, tril_ref, y_ref, stout_ref, xpad_scr, st_scr,
              *, valid, heads_per_group):
    c = pl.program_id(1)
    L = CHUNK
    inner = z_ref.shape[1]
    gw = heads_per_group * SSD_HEAD_DIM
    n_groups = inner // gw
    gn = D_STATE
    tail_rows = V7X_SUBLANES

    @pl.when(c == 0)
    def _load_stream_state():
        xpad_scr[0:tail_rows, :] = tail_ref[...]
        st_scr[...] = st0_ref[...].T

    xpad_scr[tail_rows:tail_rows + L, 0:inner] = x_ref[...]
    xpad_scr[tail_rows:tail_rows + L, inner:] = bc_ref[...]
    conv = cb_ref[...]
    first = tail_rows - (CONV_WIDTH - 1)
    for i in range(CONV_WIDTH):
        conv = conv + cw_ref[i:i + 1, :] * xpad_scr[first + i:first + i + L, :]
    xpad_scr[0:tail_rows, :] = xpad_scr[L:L + tail_rows, :]
    act = conv * _sigmoid(conv)
    xs = act[:, :inner]
    bm = act[:, inner:inner + n_groups * gn]
    cm = act[:, inner + n_groups * gn:]

    dtv = dt_ref[...] + dtb_ref[...]
    dt = jnp.maximum(dtv, 0.0) + jnp.log1p(jnp.exp(-jnp.abs(dtv)))
    if valid < L:
        dt = jnp.where(lax.broadcasted_iota(jnp.int32, dt.shape, 0) < valid, dt, 0.0)
    d_a = dt * (-jnp.exp(alog_ref[...]))

    pieces = jnp.concatenate(_split3(d_a), axis=1)
    cs = jnp.dot(tril_ref[...], pieces, preferred_element_type=F32)
    w = d_a.shape[1]
    a_cs = (cs[:, :w] + cs[:, w:2 * w]) + cs[:, 2 * w:]

    both = jnp.concatenate([a_cs, dt], axis=0)
    pieces = jnp.concatenate(_split3(both), axis=0)
    ex = jnp.dot(pieces, e_ref[...], preferred_element_type=F32)
    ex = (ex[:2 * L] + ex[2 * L:4 * L]) + ex[4 * L:]
    acol = ex[:L]
    dtx = ex[L:]

    row = lax.broadcasted_iota(jnp.int32, (L, inner), 0)
    sub = lax.broadcasted_iota(jnp.int32, (L, inner), 1) & (SSD_HEAD_DIM - 1)
    arow = jnp.sum(jnp.where(row == sub, acol, 0.0), axis=0, keepdims=True)
    decay_in = jnp.exp(jnp.where(sub <= row, acol - arow, -jnp.inf))
    a_last = acol[L - 1:L, :]
    decay_to_end = jnp.exp(a_last - acol)
    decay_from_start = jnp.exp(acol)
    chunk_decay = jnp.exp(a_last)

    xdt = xs * dtx
    xdt_bf = xdt.astype(BF16)
    xd_end = (xdt * decay_to_end).astype(BF16)
    bm_t = jnp.concatenate([bm, jnp.zeros_like(bm)], axis=0).T
    zeros_rows = jnp.zeros((L, gw), BF16)
    lane_head = lax.broadcasted_iota(jnp.int32, (L, gw), 1) // SSD_HEAD_DIM

    for g in range(n_groups):
        hs = slice(g * gw, (g + 1) * gw)
        ns = slice(g * gn, (g + 1) * gn)
        b_bf = bm[:, ns].astype(BF16)
        c_bf = cm[:, ns].astype(BF16)
        cb = lax.dot_general(c_bf, jnp.concatenate([b_bf] * heads_per_group, axis=0), NT_DIMS,
                             preferred_element_type=F32)
        lhs = (cb * decay_in[:, hs]).astype(BF16)
        xg = xdt_bf[:, hs]
        zero = jnp.zeros_like(xg)
        block_diag = jnp.concatenate(
            [jnp.where(lane_head == r, xg, zero) for r in range(heads_per_group)], axis=0)
        y = jnp.dot(lhs, block_diag, preferred_element_type=F32)
        st = st_scr[:, hs]
        y = y + jnp.dot(c_bf, st.astype(BF16), preferred_element_type=F32) * decay_from_start[:, hs]
        bt = bm_t[ns, :].astype(BF16)
        st_scr[:, hs] = chunk_decay[:, hs] * st + jnp.dot(
            bt, jnp.concatenate([xd_end[:, hs], zeros_rows], axis=0), preferred_element_type=F32)
        y = y + dsk_ref[:, hs] * xs[:, hs]
        zg = z_ref[:, hs]
        y = y * (zg * _sigmoid(zg))
        y = y * lax.rsqrt(jnp.mean(y * y, axis=-1, keepdims=True) + EPS)
        y_ref[:, hs] = (y * nw_ref[:, hs]).astype(y_ref.dtype)

    @pl.when(c == pl.num_programs(1) - 1)
    def _store_stream_state():
        stout_ref[...] = st_scr[...].T


def _ssd(zx, dt_raw, tail, st0, p, *, batch, n_chunks, valid):
    rows = zx.shape[0]
    inner = zx.shape[1] // 3
    conv_dim = 2 * inner
    heads = inner // SSD_HEAD_DIM
    L = CHUNK
    assert SSD_HEAD_DIM == CHUNK and conv_dim == inner + 2 * SSD_GROUPS * D_STATE
    expand = np.zeros((V7X_LANES, inner), np.float32)
    for h in range(heads):
        expand[h, h * SSD_HEAD_DIM:(h + 1) * SSD_HEAD_DIM] = 1.0
    tril = np.tril(np.ones((L, L), np.float32))

    def row_blk(col):
        return pl.BlockSpec((L, inner), lambda b, c, col=col: (b * n_chunks + c, col))

    def const(shape):
        return pl.BlockSpec(shape, lambda b, c: (0,) * len(shape))

    return pl.pallas_call(
        functools.partial(_ssd_body, valid=valid, heads_per_group=heads // SSD_GROUPS),
        grid=(batch, n_chunks),
        in_specs=[
            row_blk(0), row_blk(1), row_blk(2),
            pl.BlockSpec((L, V7X_LANES), lambda b, c: (b * n_chunks + c, 0)),
            pl.BlockSpec((None, V7X_SUBLANES, conv_dim), lambda b, c: (b, 0, 0)),
            pl.BlockSpec((None, inner, D_STATE), lambda b, c: (b, 0, 0)),
            const((CONV_WIDTH, conv_dim)), const((1, conv_dim)),
            const((1, V7X_LANES)), const((1, V7X_LANES)),
            const((1, inner)), const((1, inner)),
            const((V7X_LANES, inner)), const((L, L)),
        ],
        out_specs=[
            pl.BlockSpec((L, inner), lambda b, c: (b * n_chunks + c, 0)),
            pl.BlockSpec((None, inner, D_STATE), lambda b, c: (b, 0, 0)),
        ],
        out_shape=[
            jax.ShapeDtypeStruct((rows, inner), BF16),
            jax.ShapeDtypeStruct((batch, inner, D_STATE), F32),
        ],
        scratch_shapes=[
            pltpu.VMEM((L + V7X_SUBLANES, conv_dim), F32),
            pltpu.VMEM((D_STATE, inner), F32),
        ],
        compiler_params=_cparams(2),
        name="ssd",
    )(zx, zx, zx, dt_raw, tail, st0, p["conv_w"], p["conv_b"], p["dt_bias"], p["a_log"],
      p["d_skip"], p["ssd_norm_w"], jnp.asarray(expand, BF16), jnp.asarray(tril, BF16))


def _pad_lanes(v, width):
    return jnp.pad(v.astype(F32), (0, width - v.shape[0])).reshape(1, width)


def _prepare_layer(layer, norm1_w, w_in, lambda_q1, lambda_k1, lambda_q2, lambda_k2, subln_w,
                   conv_w, conv_b, dt_bias, A_log, D_skip, ssd_norm_w, w_out, norm2_w,
                   w_gate, w_up, w_down, ff_tile):
    d_model = w_in.shape[1]
    attn_w = 3 * (w_out.shape[1] // 2)
    inner = w_out.shape[1] // 2
    main_cols = attn_w + inner + 2 * inner
    w_in_bf = w_in[layer].astype(BF16)
    n_heads = w_in.shape[2] - main_cols
    w_dt = jnp.pad(w_in_bf[:, main_cols:], ((0, 0), (0, V7X_LANES - n_heads)))
    d_ff = w_gate.shape[2]
    ff_pad = -d_ff % ff_tile
    return {
        "lam0": _lambda_init(layer),
        "norm1_w": norm1_w[layer], "norm2_w": norm2_w[layer],
        "w_in": w_in_bf, "w_dt": w_dt, "main_cols": main_cols,
        "lams": [v[layer].reshape(1, -1).astype(F32)
                 for v in (lambda_q1, lambda_k1, lambda_q2, lambda_k2)],
        "subln_w": subln_w[layer].reshape(1, -1).astype(F32),
        "conv_w": conv_w[layer].astype(F32), "conv_b": conv_b[layer].reshape(1, -1).astype(F32),
        "dt_bias": _pad_lanes(dt_bias[layer], V7X_LANES),
        "a_log": _pad_lanes(A_log[layer], V7X_LANES),
        "d_skip": jnp.repeat(D_skip[layer].astype(F32), SSD_HEAD_DIM).reshape(1, -1),
        "ssd_norm_w": ssd_norm_w[layer].reshape(1, -1).astype(F32),
        "w_out": w_out[layer].astype(BF16),
        "w_gate": jnp.pad(w_gate[layer].astype(BF16), ((0, 0), (0, ff_pad))),
        "w_up": jnp.pad(w_up[layer].astype(BF16), ((0, 0), (0, ff_pad))),
        "w_down": jnp.pad(w_down[layer].astype(BF16), ((0, ff_pad), (0, 0))),
        "d_model": d_model, "inner": inner,
    }


def _tiles(m):
    return 1024 if m % 1024 == 0 else m


def _in_proj(x2d, p):
    m = x2d.shape[0]
    inner = p["inner"]
    tm = _tiles(m)
    h = _rmsnorm(x2d, p["norm1_w"], BF16, min(256, m))
    q, = _matmul([h], [p["w_in"]], n=inner, tm=tm, tn=1024, b_col_off=0,
                 outs=[(BF16, ATTN_SCALE)], name="proj_q")
    k32, kbf = _matmul([h], [p["w_in"]], n=inner, tm=tm, tn=1024, b_col_off=inner,
                       outs=[(F32, 1.0), (BF16, 1.0)], name="proj_k")
    v32, vbf = _matmul([h], [p["w_in"]], n=inner, tm=tm, tn=1024, b_col_off=2 * inner,
                       outs=[(F32, 1.0), (BF16, 1.0)], name="proj_v")
    zx, = _matmul([h], [p["w_in"]], n=3 * inner, tm=tm, tn=1024, b_col_off=3 * inner,
                  outs=[(F32, 1.0)], name="proj_zx")
    dt, = _matmul([h], [p["w_dt"]], n=V7X_LANES, tm=tm, tn=V7X_LANES,
                  outs=[(F32, 1.0)], name="proj_dt")
    return q, k32, kbf, v32, vbf, zx, dt


def _out_and_ffn(x2d, attn, ssd_y, p):
    m = x2d.shape[0]
    tm = _tiles(m)
    x1, = _matmul([attn, ssd_y], [p["w_out"], p["w_out"]], b_row_blocks=[0, 1], n=p["d_model"],
                  tm=tm, tn=min(1024, p["d_model"]), outs=[(F32, 1.0)], residual=x2d, name="out_proj")
    h2 = _rmsnorm(x1, p["norm2_w"], BF16, min(256, m))
    act = _ffn_up(h2, p["w_gate"], p["w_up"], tm=tm, tn=512 if tm == 1024 else 1024)
    return _matmul_acc(act, p["w_down"], x1, tm=tm, tn=min(1024, p["d_model"]), tk=1024)


def kernel(x_prompt, x_sample, cache_k, cache_v, state_conv, state_ssm, norm1_w, w_in, lambda_q1, lambda_k1, lambda_q2, lambda_k2, subln_w, conv_w, conv_b, dt_bias, A_log, D_skip, ssd_norm_w, w_out, norm2_w, w_gate, w_up, w_down, final_norm_w):
    depth = w_in.shape[0]
    bp, seq, d_model = x_prompt.shape
    bs, rows, _ = x_sample.shape
    heads = cache_k.shape[3]
    inner = w_out.shape[1] // 2
    attn_w = heads * V_HEAD_DIM
    conv_dim = conv_w.shape[2]
    ssd_heads = state_ssm.shape[2]
    assert seq % CHUNK == 0 and rows <= CHUNK and rows >= CONV_WIDTH - 1 and attn_w == inner

    xp = x_prompt.reshape(bp * seq, d_model)
    xs = x_sample.reshape(bs * rows, d_model)
    outs_p = [[], [], [], []]
    outs_s = [[], [], [], []]
    for layer in range(depth):
        p = _prepare_layer(layer, norm1_w, w_in, lambda_q1, lambda_k1, lambda_q2, lambda_k2, subln_w,
                           conv_w, conv_b, dt_bias, A_log, D_skip, ssd_norm_w, w_out, norm2_w,
                           w_gate, w_up, w_down, ff_tile=1024)
        lam0 = p["lam0"]

        q, k32, kbf, v32, vbf, zx, dt = _in_proj(xp, p)
        attn = _attn_prompt(q, kbf, vbf, p["lams"], p["subln_w"], batch=bp, seq=seq, heads=heads,
                            t=512, lam0=lam0)
        ssd_y, st = _ssd(zx, dt, jnp.zeros((bp, V7X_SUBLANES, conv_dim), F32),
                         jnp.zeros((bp, inner, D_STATE), F32), p,
                         batch=bp, n_chunks=seq // CHUNK, valid=CHUNK)
        outs_p[0].append(k32.reshape(bp, seq, heads, V_HEAD_DIM))
        outs_p[1].append(v32.reshape(bp, seq, heads, V_HEAD_DIM))
        outs_p[2].append(zx.reshape(bp, seq, 3 * inner)[:, seq - (CONV_WIDTH - 1):, inner:])
        outs_p[3].append(st.reshape(bp, ssd_heads, SSD_HEAD_DIM, D_STATE))
        xp = _out_and_ffn(xp, attn, ssd_y, p)

        q, k32, kbf, v32, vbf, zx, dt = _in_proj(xs, p)
        pad_new = ((0, 0), (0, V7X_LANES - rows), (0, 0))
        attn = _attn_decode(q, jnp.pad(kbf.reshape(bs, rows, attn_w), pad_new),
                            jnp.pad(vbf.reshape(bs, rows, attn_w), pad_new),
                            cache_k[layer].reshape(bs, -1, attn_w), cache_v[layer].reshape(bs, -1, attn_w),
                            p["lams"], p["subln_w"], batch=bs, rows=rows, heads=heads, lam0=lam0)
        pad_chunk = ((0, 0), (0, CHUNK - rows), (0, 0))
        zx_c = jnp.pad(zx.reshape(bs, rows, 3 * inner), pad_chunk).reshape(bs * CHUNK, 3 * inner)
        dt_c = jnp.pad(dt.reshape(bs, rows, V7X_LANES), pad_chunk).reshape(bs * CHUNK, V7X_LANES)
        tail = jnp.pad(state_conv[layer].astype(F32),
                       ((0, 0), (V7X_SUBLANES - (CONV_WIDTH - 1), 0), (0, 0)))
        ssd_y, st = _ssd(zx_c, dt_c, tail, state_ssm[layer].astype(F32).reshape(bs, inner, D_STATE), p,
                         batch=bs, n_chunks=1, valid=rows)
        ssd_y = ssd_y.reshape(bs, CHUNK, inner)[:, :rows].reshape(bs * rows, inner)
        outs_s[0].append(k32.reshape(bs, rows, heads, V_HEAD_DIM))
        outs_s[1].append(v32.reshape(bs, rows, heads, V_HEAD_DIM))
        outs_s[2].append(zx.reshape(bs, rows, 3 * inner)[:, rows - (CONV_WIDTH - 1):, inner:])
        outs_s[3].append(st.reshape(bs, ssd_heads, SSD_HEAD_DIM, D_STATE))
        xs = _out_and_ffn(xs, attn, ssd_y, p)

    y_prompt = _rmsnorm(xp, final_norm_w, F32, 256).reshape(bp, seq, d_model)
    y_sample = _rmsnorm(xs, final_norm_w, F32, bs * rows).reshape(bs, rows, d_model)
    return (y_prompt, y_sample, *(jnp.stack(o) for o in outs_p), *(jnp.stack(o) for o in outs_s))
```

```python
import functools
import math

import jax
import jax.numpy as jnp
import numpy as np
from jax import lax
from jax.experimental import pallas as pl
from jax.experimental.pallas import tpu as pltpu

F32 = jnp.float32
BF16 = jnp.bfloat16

CHUNK = 64
V_HEAD_DIM = 128
QK_DIM = V_HEAD_DIM // 2
ATTN_SCALE = QK_DIM ** -0.5
LOG2_E = math.log2(math.e)
SSD_HEAD_DIM = 64
SSD_GROUPS = 8
D_STATE = 128
CONV_WIDTH = 4
EPS = 1e-6

V7X_LANES = 128
V7X_SUBLANES = 8
V7X_VMEM_BYTES = 64 * 1024 * 1024
VMEM_LIMIT_BYTES = V7X_VMEM_BYTES - 8 * 1024 * 1024

NT_DIMS = (((1,), (1,)), ((), ()))


def _cparams(n_axes):
    return pltpu.CompilerParams(
        dimension_semantics=("arbitrary",) * n_axes, vmem_limit_bytes=VMEM_LIMIT_BYTES)


def _sigmoid(x):
    return 1.0 / (1.0 + jnp.exp(-x))


def _lambda_init(layer):
    return 0.8 - 0.6 * math.exp(-0.3 * layer)


def _rmsnorm_body(x_ref, w_ref, o_ref):
    x = x_ref[...]
    ms = jnp.mean(x * x, axis=-1, keepdims=True)
    o_ref[...] = (x * lax.rsqrt(ms + EPS) * w_ref[...]).astype(o_ref.dtype)


def _rmsnorm(x, w, out_dtype, tr):
    m, d = x.shape
    return pl.pallas_call(
        _rmsnorm_body,
        grid=(m // tr,),
        in_specs=[pl.BlockSpec((tr, d), lambda i: (i, 0)),
                  pl.BlockSpec((1, d), lambda i: (0, 0))],
        out_specs=pl.BlockSpec((tr, d), lambda i: (i, 0)),
        out_shape=jax.ShapeDtypeStruct((m, d), out_dtype),
        compiler_params=_cparams(1),
        name="rmsnorm",
    )(x, w.reshape(1, d).astype(F32))


def _mm_body(*refs, n_pairs, scales, has_res):
    a_refs = refs[:n_pairs]
    b_refs = refs[n_pairs:2 * n_pairs]
    pos = 2 * n_pairs
    res_ref = refs[pos] if has_res else None
    out_refs = refs[pos + int(has_res):]
    acc = None
    for a_ref, b_ref in zip(a_refs, b_refs):
        d = jnp.dot(a_ref[...], b_ref[...], preferred_element_type=F32)
        acc = d if acc is None else acc + d
    if has_res:
        acc = acc + res_ref[...]
    for o_ref, s in zip(out_refs, scales):
        o_ref[...] = (acc if s == 1.0 else acc * s).astype(o_ref.dtype)


def _matmul(a_list, b_list, *, n, tm, tn, outs, b_row_blocks=None, b_col_off=0,
            residual=None, name="matmul"):
    m = a_list[0].shape[0]
    n_pairs = len(a_list)
    if b_row_blocks is None:
        b_row_blocks = [0] * n_pairs
    assert b_col_off % tn == 0 and n % tn == 0 and m % tm == 0
    col_blk = b_col_off // tn
    in_specs = []
    for a in a_list:
        in_specs.append(pl.BlockSpec((tm, a.shape[1]), lambda i, j: (i, 0)))
    for a, rb in zip(a_list, b_row_blocks):
        in_specs.append(pl.BlockSpec((a.shape[1], tn), lambda i, j, rb=rb: (rb, col_blk + j)))
    args = list(a_list) + list(b_list)
    if residual is not None:
        in_specs.append(pl.BlockSpec((tm, tn), lambda i, j: (i, j)))
        args.append(residual)
    out_shape = [jax.ShapeDtypeStruct((m, n), dt) for dt, _ in outs]
    out_specs = [pl.BlockSpec((tm, tn), lambda i, j: (i, j)) for _ in outs]
    res = pl.pallas_call(
        functools.partial(_mm_body, n_pairs=n_pairs, scales=tuple(s for _, s in outs),
                          has_res=residual is not None),
        grid=(m // tm, n // tn),
        in_specs=in_specs,
        out_specs=out_specs,
        out_shape=out_shape,
        compiler_params=_cparams(2),
        name=name,
    )(*args)
    return res


def _ffn_up_body(h_ref, wg_ref, wu_ref, o_ref):
    h = h_ref[...]
    g = jnp.dot(h, wg_ref[...], preferred_element_type=F32)
    u = jnp.dot(h, wu_ref[...], preferred_element_type=F32)
    o_ref[...] = (g * _sigmoid(g) * u).astype(o_ref.dtype)


def _ffn_up(h, wg, wu, *, tm, tn):
    m, d = h.shape
    n = wg.shape[1]
    return pl.pallas_call(
        _ffn_up_body,
        grid=(m // tm, n // tn),
        in_specs=[pl.BlockSpec((tm, d), lambda i, j: (i, 0)),
                  pl.BlockSpec((d, tn), lambda i, j: (0, j)),
                  pl.BlockSpec((d, tn), lambda i, j: (0, j))],
        out_specs=pl.BlockSpec((tm, tn), lambda i, j: (i, j)),
        out_shape=jax.ShapeDtypeStruct((m, n), BF16),
        compiler_params=_cparams(2),
        name="ffn_up",
    )(h, wg, wu)


def _mm_acc_body(a_ref, b_ref, r_ref, o_ref):
    k = pl.program_id(2)
    d = jnp.dot(a_ref[...], b_ref[...], preferred_element_type=F32)

    @pl.when(k == 0)
    def _first():
        o_ref[...] = r_ref[...] + d

    @pl.when(k > 0)
    def _rest():
        o_ref[...] += d


def _matmul_acc(a, b, residual, *, tm, tn, tk):
    m, kdim = a.shape
    n = b.shape[1]
    assert m % tm == 0 and n % tn == 0 and kdim % tk == 0
    return pl.pallas_call(
        _mm_acc_body,
        grid=(m // tm, n // tn, kdim // tk),
        in_specs=[pl.BlockSpec((tm, tk), lambda i, j, k: (i, k)),
                  pl.BlockSpec((tk, tn), lambda i, j, k: (k, j)),
                  pl.BlockSpec((tm, tn), lambda i, j, k: (i, j))],
        out_specs=pl.BlockSpec((tm, tn), lambda i, j, k: (i, j)),
        out_shape=jax.ShapeDtypeStruct((m, n), F32),
        compiler_params=_cparams(3),
        name="ffn_down",
    )(a, b, residual)


def _lambda_value(lq1, lk1, lq2, lk2, lam0):
    d1 = jnp.sum(lq1[...] * lk1[...], axis=-1, keepdims=True)
    d2 = jnp.sum(lq2[...] * lk2[...], axis=-1, keepdims=True)
    return jnp.exp(d1) - jnp.exp(d2) + lam0


def _subln(o, w, lam0):
    ms = jnp.mean(o * o, axis=-1, keepdims=True)
    return (o * lax.rsqrt(ms + EPS) * w) * (1.0 - lam0)


def _attn_prompt_body(lq1, lk1, lq2, lk2, w_ref, q_ref, k_ref, v_ref, o_ref,
                      k1_scr, k2_scr, vt_scr, sa, sb, pa, pb, cma, cmb, ala, alb, m_scr, l_scr, acc_scr,
                      *, tq, tk, lam0):
    qi = pl.program_id(2)
    n_kv = k1_scr.shape[0]
    k_scrs = (k1_scr, k2_scr)

    @pl.when(qi == 0)
    def _prepare_head():
        first_half = lax.broadcasted_iota(jnp.int32, (tk, V_HEAD_DIM), 1) < QK_DIM
        for c in range(n_kv):
            kb = k_ref[pl.ds(c * tk, tk), :]
            zero = jnp.zeros_like(kb)
            k1_scr[c] = jnp.where(first_half, kb, zero)
            k2_scr[c] = jnp.where(first_half, zero, kb)
            vt_scr[c] = v_ref[pl.ds(c * tk, tk), :].T

    m_scr[...] = jnp.full(m_scr.shape, -jnp.inf, F32)
    l_scr[...] = jnp.zeros(l_scr.shape, F32)
    acc_scr[...] = jnp.zeros(acc_scr.shape, F32)
    pb[...] = jnp.zeros(pb.shape, BF16)
    alb[...] = jnp.ones(alb.shape, F32)

    def scores(j, s_scr, cm_scr, c0):
        qv = q_ref[c0:, :]
        for idx in range(2):
            s = lax.dot_general(k_scrs[idx][j], qv, NT_DIMS, preferred_element_type=F32)
            s_scr[idx, :, c0:] = s
            cm_scr[idx, :, c0:] = jnp.max(s, axis=0, keepdims=True)

    def softmax(s_scr, cm_scr, p_scr, al_scr, c0, mask):
        for idx in range(2):
            s = s_scr[idx, :, c0:]
            if mask is None:
                cmax = cm_scr[idx, :, c0:]
            else:
                s = jnp.where(mask, s, -jnp.inf)
                cmax = jnp.max(s, axis=0, keepdims=True)
            m_old = m_scr[idx, :, c0:]
            m_new = jnp.maximum(m_old, cmax)
            alpha = jnp.exp2(m_old - m_new)
            p = jnp.exp2(s - m_new)
            l_scr[idx, :, c0:] = alpha * l_scr[idx, :, c0:] + jnp.sum(p, axis=0, keepdims=True)
            p_scr[idx, :, c0:] = p.astype(BF16)
            al_scr[idx, :, c0:] = alpha
            m_scr[idx, :, c0:] = m_new

    def value_product(j, p_scr, al_scr, c0):
        vt = vt_scr[j]
        for idx in range(2):
            acc_scr[idx, :, c0:] = al_scr[idx, :, c0:] * acc_scr[idx, :, c0:] + jnp.dot(
                vt, p_scr[idx, :, c0:], preferred_element_type=F32)

    scores(0, sa, cma, 0)

    def pair(jj, carry):
        j = 2 * jj
        scores(j + 1, sb, cmb, 0)
        value_product(jnp.maximum(j - 1, 0), pb, alb, 0)
        softmax(sa, cma, pa, ala, 0, None)
        scores(j + 2, sa, cma, 0)
        value_product(j, pa, ala, 0)
        softmax(sb, cmb, pb, alb, 0, None)
        return carry

    lax.fori_loop(0, qi, pair, 0)

    j0 = 2 * qi
    key_chunk = lax.broadcasted_iota(jnp.int32, (tk, tq), 0) // CHUNK
    qry_chunk = lax.broadcasted_iota(jnp.int32, (tk, tq), 1) // CHUNK
    visible = key_chunk <= qry_chunk
    scores(j0 + 1, sb, cmb, tk)
    value_product(jnp.maximum(j0 - 1, 0), pb, alb, 0)
    softmax(sa, cma, pa, ala, 0, visible)
    softmax(sb, cmb, pb, alb, tk, visible[:, :tk])
    value_product(j0, pa, ala, 0)
    value_product(j0 + 1, pb, alb, tk)

    lam = _lambda_value(lq1, lk1, lq2, lk2, lam0)
    o_t = acc_scr[0] * (1.0 / l_scr[0]) - lam * (acc_scr[1] * (1.0 / l_scr[1]))
    o_ref[...] = _subln(o_t.T, w_ref[...], lam0).astype(o_ref.dtype)


def _attn_prompt(q, k, v, lams, subln_w, *, batch, seq, heads, tk, lam0):
    tq = 2 * tk
    nq = seq // tq
    n_kv = seq // tk
    lam_specs = [pl.BlockSpec((1, QK_DIM), lambda b, h, i: (0, 0))] * 4
    return pl.pallas_call(
        functools.partial(_attn_prompt_body, tq=tq, tk=tk, lam0=lam0),
        grid=(batch, heads, nq),
        in_specs=lam_specs + [
            pl.BlockSpec((1, V_HEAD_DIM), lambda b, h, i: (0, 0)),
            pl.BlockSpec((tq, V_HEAD_DIM), lambda b, h, i: (b * nq + i, h)),
            pl.BlockSpec((seq, V_HEAD_DIM), lambda b, h, i: (b, h)),
            pl.BlockSpec((seq, V_HEAD_DIM), lambda b, h, i: (b, h)),
        ],
        out_specs=pl.BlockSpec((tq, V_HEAD_DIM), lambda b, h, i: (b * nq + i, h)),
        out_shape=jax.ShapeDtypeStruct(q.shape, BF16),
        scratch_shapes=[
            pltpu.VMEM((n_kv, tk, V_HEAD_DIM), BF16),
            pltpu.VMEM((n_kv, tk, V_HEAD_DIM), BF16),
            pltpu.VMEM((n_kv, V_HEAD_DIM, tk), BF16),
            pltpu.VMEM((2, tk, tq), F32), pltpu.VMEM((2, tk, tq), F32),
            pltpu.VMEM((2, tk, tq), BF16), pltpu.VMEM((2, tk, tq), BF16),
            pltpu.VMEM((2, 1, tq), F32), pltpu.VMEM((2, 1, tq), F32),
            pltpu.VMEM((2, 1, tq), F32), pltpu.VMEM((2, 1, tq), F32),
            pltpu.VMEM((2, 1, tq), F32),
            pltpu.VMEM((2, 1, tq), F32),
            pltpu.VMEM((2, V_HEAD_DIM, tq), F32),
        ],
        compiler_params=_cparams(3),
        name="attn_prompt",
    )(*lams, subln_w, q, k, v)


def _attn_decode_body(lq1, lk1, lq2, lk2, w_ref, q_ref, kn_ref, vn_ref, kc_ref, vc_ref, o_ref,
                      *, n_new, lam0):
    q = q_ref[...]
    first_half = lax.broadcasted_iota(jnp.int32, q.shape, 1) < QK_DIM
    zero = jnp.zeros_like(q)
    kc = kc_ref[...].astype(BF16)
    vc = vc_ref[...].astype(BF16)
    kn = kn_ref[...]
    vn = vn_ref[...]
    new_valid = lax.broadcasted_iota(jnp.int32, (q.shape[0], kn.shape[0]), 1) < n_new
    outs = []
    for qh in (jnp.where(first_half, q, zero), jnp.where(first_half, zero, q)):
        sc = lax.dot_general(qh, kc, NT_DIMS, preferred_element_type=F32)
        sn = lax.dot_general(qh, kn, NT_DIMS, preferred_element_type=F32)
        sn = jnp.where(new_valid, sn, -jnp.inf)
        m = jnp.maximum(jnp.max(sc, axis=-1, keepdims=True), jnp.max(sn, axis=-1, keepdims=True))
        pc = jnp.exp2(sc - m)
        pn = jnp.exp2(sn - m)
        l = jnp.sum(pc, axis=-1, keepdims=True) + jnp.sum(pn, axis=-1, keepdims=True)
        o = (jnp.dot(pc.astype(BF16), vc, preferred_element_type=F32)
             + jnp.dot(pn.astype(BF16), vn, preferred_element_type=F32))
        outs.append(o * (1.0 / l))
    lam = _lambda_value(lq1, lk1, lq2, lk2, lam0)
    o_ref[...] = _subln(outs[0] - lam * outs[1], w_ref[...], lam0).astype(o_ref.dtype)


def _attn_decode(q, k_new, v_new, cache_k, cache_v, lams, subln_w, *, batch, rows, heads, lam0):
    past = cache_k.shape[1]
    pad_rows = k_new.shape[1]
    lam_specs = [pl.BlockSpec((1, QK_DIM), lambda b, h: (0, 0))] * 4
    return pl.pallas_call(
        functools.partial(_attn_decode_body, n_new=rows, lam0=lam0),
        grid=(batch, heads),
        in_specs=lam_specs + [
            pl.BlockSpec((1, V_HEAD_DIM), lambda b, h: (0, 0)),
            pl.BlockSpec((rows, V_HEAD_DIM), lambda b, h: (b, h)),
            pl.BlockSpec((None, pad_rows, V_HEAD_DIM), lambda b, h: (b, 0, h)),
            pl.BlockSpec((None, pad_rows, V_HEAD_DIM), lambda b, h: (b, 0, h)),
            pl.BlockSpec((None, past, V_HEAD_DIM), lambda b, h: (b, 0, h)),
            pl.BlockSpec((None, past, V_HEAD_DIM), lambda b, h: (b, 0, h)),
        ],
        out_specs=pl.BlockSpec((rows, V_HEAD_DIM), lambda b, h: (b, h)),
        out_shape=jax.ShapeDtypeStruct(q.shape, BF16),
        compiler_params=_cparams(2),
        name="attn_decode",
    )(*lams, subln_w, q, k_new, v_new, cache_k, cache_v)


def _split3(x):
    hi = x.astype(BF16)
    r1 = x - hi.astype(F32)
    mid = r1.astype(BF16)
    lo = (r1 - mid.astype(F32)).astype(BF16)
    return hi, mid, lo


def _ssd_body(z_ref, x_ref, bc_ref, dt_ref, tail_ref, st0_ref, cw_ref, cb_ref, dtb_ref, alog_ref,
              dsk_ref, nw_ref, e_ref, tril_ref, y_ref, stout_ref, xpad_scr, st_scr,
              *, valid, heads_per_group):
    c = pl.program_id(1)
    L = CHUNK
    inner = z_ref.shape[1]
    gw = heads_per_group * SSD_HEAD_DIM
    n_groups = inner // gw
    gn = D_STATE
    tail_rows = V7X_SUBLANES

    @pl.when(c == 0)
    def _load_stream_state():
        xpad_scr[0:tail_rows, :] = tail_ref[...]
        st_scr[...] = st0_ref[...].T

    xpad_scr[tail_rows:tail_rows + L, 0:inner] = x_ref[...]
    xpad_scr[tail_rows:tail_rows + L, inner:] = bc_ref[...]
    conv = cb_ref[...]
    first = tail_rows - (CONV_WIDTH - 1)
    for i in range(CONV_WIDTH):
        conv = conv + cw_ref[i:i + 1, :] * xpad_scr[first + i:first + i + L, :]
    xpad_scr[0:tail_rows, :] = xpad_scr[L:L + tail_rows, :]
    act = conv * _sigmoid(conv)
    xs = act[:, :inner]
    bm = act[:, inner:inner + n_groups * gn]
    cm = act[:, inner + n_groups * gn:]

    dtv = dt_ref[...] + dtb_ref[...]
    dt = jnp.maximum(dtv, 0.0) + jnp.log1p(jnp.exp(-jnp.abs(dtv)))
    if valid < L:
        dt = jnp.where(lax.broadcasted_iota(jnp.int32, dt.shape, 0) < valid, dt, 0.0)
    d_a = dt * (-jnp.exp(alog_ref[...]))

    pieces = jnp.concatenate(_split3(d_a), axis=1)
    cs = jnp.dot(tril_ref[...], pieces, preferred_element_type=F32)
    w = d_a.shape[1]
    a_cs = (cs[:, :w] + cs[:, w:2 * w]) + cs[:, 2 * w:]

    both = jnp.concatenate([a_cs, dt], axis=0)
    pieces = jnp.concatenate(_split3(both), axis=0)
    ex = jnp.dot(pieces, e_ref[...], preferred_element_type=F32)
    ex = (ex[:2 * L] + ex[2 * L:4 * L]) + ex[4 * L:]
    acol = ex[:L]
    dtx = ex[L:]

    row = lax.broadcasted_iota(jnp.int32, (L, inner), 0)
    sub = lax.broadcasted_iota(jnp.int32, (L, inner), 1) & (SSD_HEAD_DIM - 1)
    arow = jnp.sum(jnp.where(row == sub, acol, 0.0), axis=0, keepdims=True)
    decay_in = jnp.exp(jnp.where(sub <= row, acol - arow, -jnp.inf))
    a_last = acol[L - 1:L, :]
    decay_to_end = jnp.exp(a_last - acol)
    decay_from_start = jnp.exp(acol)
    chunk_decay = jnp.exp(a_last)

    xdt = xs * dtx
    xdt_bf = xdt.astype(BF16)
    xd_end = (xdt * decay_to_end).astype(BF16)
    bm_t = jnp.concatenate([bm, jnp.zeros_like(bm)], axis=0).T
    zeros_rows = jnp.zeros((L, gw), BF16)
    lane_head = lax.broadcasted_iota(jnp.int32, (L, gw), 1) // SSD_HEAD_DIM

    for g in range(n_groups):
        hs = slice(g * gw, (g + 1) * gw)
        ns = slice(g * gn, (g + 1) * gn)
        b_bf = bm[:, ns].astype(BF16)
        c_bf = cm[:, ns].astype(BF16)
        cb = lax.dot_general(c_bf, jnp.concatenate([b_bf] * heads_per_group, axis=0), NT_DIMS,
                             preferred_element_type=F32)
        lhs = (cb * decay_in[:, hs]).astype(BF16)
        xg = xdt_bf[:, hs]
        zero = jnp.zeros_like(xg)
        block_diag = jnp.concatenate(
            [jnp.where(lane_head == r, xg, zero) for r in range(heads_per_group)], axis=0)
        y = jnp.dot(lhs, block_diag, preferred_element_type=F32)
        st = st_scr[:, hs]
        y = y + jnp.dot(c_bf, st.astype(BF16), preferred_element_type=F32) * decay_from_start[:, hs]
        bt = bm_t[ns, :].astype(BF16)
        st_scr[:, hs] = chunk_decay[:, hs] * st + jnp.dot(
            bt, jnp.concatenate([xd_end[:, hs], zeros_rows], axis=0), preferred_element_type=F32)
        y = y + dsk_ref[:, hs] * xs[:, hs]
        zg = z_ref[:, hs]
        y = y * (zg * _sigmoid(zg))
        y = y * lax.rsqrt(jnp.mean(y * y, axis=-1, keepdims=True) + EPS)
        y_ref[:, hs] = (y * nw_ref[:, hs]).astype(y_ref.dtype)

    @pl.when(c == pl.num_programs(1) - 1)
    def _store_stream_state():
        stout_ref[...] = st_scr[...].T


def _ssd(zx, dt_raw, tail, st0, p, *, batch, n_chunks, valid):
    rows = zx.shape[0]
    inner = zx.shape[1] // 3
    conv_dim = 2 * inner
    heads = inner // SSD_HEAD_DIM
    L = CHUNK
    assert SSD_HEAD_DIM == CHUNK and conv_dim == inner + 2 * SSD_GROUPS * D_STATE
    expand = np.zeros((V7X_LANES, inner), np.float32)
    for h in range(heads):
        expand[h, h * SSD_HEAD_DIM:(h + 1) * SSD_HEAD_DIM] = 1.0
    tril = np.tril(np.ones((L, L), np.float32))

    def row_blk(col):
        return pl.BlockSpec((L, inner), lambda b, c, col=col: (b * n_chunks + c, col))

    def const(shape):
        return pl.BlockSpec(shape, lambda b, c: (0,) * len(shape))

    return pl.pallas_call(
        functools.partial(_ssd_body, valid=valid, heads_per_group=heads // SSD_GROUPS),
        grid=(batch, n_chunks),
        in_specs=[
            row_blk(0), row_blk(1), row_blk(2),
            pl.BlockSpec((L, V7X_LANES), lambda b, c: (b * n_chunks + c, 0)),
            pl.BlockSpec((None, V7X_SUBLANES, conv_dim), lambda b, c: (b, 0, 0)),
            pl.BlockSpec((None, inner, D_STATE), lambda b, c: (b, 0, 0)),
            const((CONV_WIDTH, conv_dim)), const((1, conv_dim)),
            const((1, V7X_LANES)), const((1, V7X_LANES)),
            const((1, inner)), const((1, inner)),
            const((V7X_LANES, inner)), const((L, L)),
        ],
        out_specs=[
            pl.BlockSpec((L, inner), lambda b, c: (b * n_chunks + c, 0)),
            pl.BlockSpec((None, inner, D_STATE), lambda b, c: (b, 0, 0)),
        ],
        out_shape=[
            jax.ShapeDtypeStruct((rows, inner), BF16),
            jax.ShapeDtypeStruct((batch, inner, D_STATE), F32),
        ],
        scratch_shapes=[
            pltpu.VMEM((L + V7X_SUBLANES, conv_dim), F32),
            pltpu.VMEM((D_STATE, inner), F32),
        ],
        compiler_params=_cparams(2),
        name="ssd",
    )(zx, zx, zx, dt_raw, tail, st0, p["conv_w"], p["conv_b"], p["dt_bias"], p["a_log"],
      p["d_skip"], p["ssd_norm_w"], jnp.asarray(expand, BF16), jnp.asarray(tril, BF16))


def _pad_lanes(v, width):
    return jnp.pad(v.astype(F32), (0, width - v.shape[0])).reshape(1, width)


def _prepare_layer(layer, norm1_w, w_in, lambda_q1, lambda_k1, lambda_q2, lambda_k2, subln_w,
                   conv_w, conv_b, dt_bias, A_log, D_skip, ssd_norm_w, w_out, norm2_w,
                   w_gate, w_up, w_down, ff_tile):
    d_model = w_in.shape[1]
    attn_w = 3 * (w_out.shape[1] // 2)
    inner = w_out.shape[1] // 2
    main_cols = attn_w + inner + 2 * inner
    w_in_bf = w_in[layer].astype(BF16)
    n_heads = w_in.shape[2] - main_cols
    w_dt = jnp.pad(w_in_bf[:, main_cols:], ((0, 0), (0, V7X_LANES - n_heads)))
    d_ff = w_gate.shape[2]
    ff_pad = -d_ff % ff_tile
    return {
        "lam0": _lambda_init(layer),
        "norm1_w": norm1_w[layer], "norm2_w": norm2_w[layer],
        "w_in": w_in_bf, "w_dt": w_dt, "main_cols": main_cols,
        "lams": [v[layer].reshape(1, -1).astype(F32)
                 for v in (lambda_q1, lambda_k1, lambda_q2, lambda_k2)],
        "subln_w": subln_w[layer].reshape(1, -1).astype(F32),
        "conv_w": conv_w[layer].astype(F32), "conv_b": conv_b[layer].reshape(1, -1).astype(F32),
        "dt_bias": _pad_lanes(dt_bias[layer], V7X_LANES),
        "a_log": _pad_lanes(A_log[layer], V7X_LANES),
        "d_skip": jnp.repeat(D_skip[layer].astype(F32), SSD_HEAD_DIM).reshape(1, -1),
        "ssd_norm_w": ssd_norm_w[layer].reshape(1, -1).astype(F32),
        "w_out": w_out[layer].astype(BF16),
        "w_gate": jnp.pad(w_gate[layer].astype(BF16), ((0, 0), (0, ff_pad))),
        "w_up": jnp.pad(w_up[layer].astype(BF16), ((0, 0), (0, ff_pad))),
        "w_down": jnp.pad(w_down[layer].astype(BF16), ((0, ff_pad), (0, 0))),
        "d_model": d_model, "inner": inner,
    }


def _tiles(m):
    return 1024 if m % 1024 == 0 else m


def _in_proj(x2d, p):
    m = x2d.shape[0]
    inner = p["inner"]
    tm = _tiles(m)
    h = _rmsnorm(x2d, p["norm1_w"], BF16, min(256, m))
    q, = _matmul([h], [p["w_in"]], n=inner, tm=tm, tn=1024, b_col_off=0,
                 outs=[(BF16, ATTN_SCALE * LOG2_E)], name="proj_q")
    k32, kbf = _matmul([h], [p["w_in"]], n=inner, tm=tm, tn=1024, b_col_off=inner,
                       outs=[(F32, 1.0), (BF16, 1.0)], name="proj_k")
    v32, vbf = _matmul([h], [p["w_in"]], n=inner, tm=tm, tn=1024, b_col_off=2 * inner,
                       outs=[(F32, 1.0), (BF16, 1.0)], name="proj_v")
    zx, = _matmul([h], [p["w_in"]], n=3 * inner, tm=tm, tn=1024, b_col_off=3 * inner,
                  outs=[(F32, 1.0)], name="proj_zx")
    dt, = _matmul([h], [p["w_dt"]], n=V7X_LANES, tm=tm, tn=V7X_LANES,
                  outs=[(F32, 1.0)], name="proj_dt")
    return q, k32, kbf, v32, vbf, zx, dt


def _out_and_ffn(x2d, attn, ssd_y, p):
    m = x2d.shape[0]
    tm = _tiles(m)
    x1, = _matmul([attn, ssd_y], [p["w_out"], p["w_out"]], b_row_blocks=[0, 1], n=p["d_model"],
                  tm=tm, tn=min(1024, p["d_model"]), outs=[(F32, 1.0)], residual=x2d, name="out_proj")
    h2 = _rmsnorm(x1, p["norm2_w"], BF16, min(256, m))
    act = _ffn_up(h2, p["w_gate"], p["w_up"], tm=tm, tn=512 if tm == 1024 else 1024)
    return _matmul_acc(act, p["w_down"], x1, tm=tm, tn=min(1024, p["d_model"]), tk=1024)


def kernel(x_prompt, x_sample, cache_k, cache_v, state_conv, state_ssm, norm1_w, w_in, lambda_q1, lambda_k1, lambda_q2, lambda_k2, subln_w, conv_w, conv_b, dt_bias, A_log, D_skip, ssd_norm_w, w_out, norm2_w, w_gate, w_up, w_down, final_norm_w):
    depth = w_in.shape[0]
    bp, seq, d_model = x_prompt.shape
    bs, rows, _ = x_sample.shape
    heads = cache_k.shape[3]
    inner = w_out.shape[1] // 2
    attn_w = heads * V_HEAD_DIM
    conv_dim = conv_w.shape[2]
    ssd_heads = state_ssm.shape[2]
    assert seq % CHUNK == 0 and rows <= CHUNK and rows >= CONV_WIDTH - 1 and attn_w == inner

    xp = x_prompt.reshape(bp * seq, d_model)
    xs = x_sample.reshape(bs * rows, d_model)
    outs_p = [[], [], [], []]
    outs_s = [[], [], [], []]
    for layer in range(depth):
        p = _prepare_layer(layer, norm1_w, w_in, lambda_q1, lambda_k1, lambda_q2, lambda_k2, subln_w,
                           conv_w, conv_b, dt_bias, A_log, D_skip, ssd_norm_w, w_out, norm2_w,
                           w_gate, w_up, w_down, ff_tile=1024)
        lam0 = p["lam0"]

        q, k32, kbf, v32, vbf, zx, dt = _in_proj(xp, p)
        attn = _attn_prompt(q, kbf, vbf, p["lams"], p["subln_w"], batch=bp, seq=seq, heads=heads,
                            tk=512, lam0=lam0)
        ssd_y, st = _ssd(zx, dt, jnp.zeros((bp, V7X_SUBLANES, conv_dim), F32),
                         jnp.zeros((bp, inner, D_STATE), F32), p,
                         batch=bp, n_chunks=seq // CHUNK, valid=CHUNK)
        outs_p[0].append(k32.reshape(bp, seq, heads, V_HEAD_DIM))
        outs_p[1].append(v32.reshape(bp, seq, heads, V_HEAD_DIM))
        outs_p[2].append(zx.reshape(bp, seq, 3 * inner)[:, seq - (CONV_WIDTH - 1):, inner:])
        outs_p[3].append(st.reshape(bp, ssd_heads, SSD_HEAD_DIM, D_STATE))
        xp = _out_and_ffn(xp, attn, ssd_y, p)

        q, k32, kbf, v32, vbf, zx, dt = _in_proj(xs, p)
        pad_new = ((0, 0), (0, V7X_LANES - rows), (0, 0))
        attn = _attn_decode(q, jnp.pad(kbf.reshape(bs, rows, attn_w), pad_new),
                            jnp.pad(vbf.reshape(bs, rows, attn_w), pad_new),
                            cache_k[layer].reshape(bs, -1, attn_w), cache_v[layer].reshape(bs, -1, attn_w),
                            p["lams"], p["subln_w"], batch=bs, rows=rows, heads=heads, lam0=lam0)
        pad_chunk = ((0, 0), (0, CHUNK - rows), (0, 0))
        zx_c = jnp.pad(zx.reshape(bs, rows, 3 * inner), pad_chunk).reshape(bs * CHUNK, 3 * inner)
        dt_c = jnp.pad(dt.reshape(bs, rows, V7X_LANES), pad_chunk).reshape(bs * CHUNK, V7X_LANES)
        tail = jnp.pad(state_conv[layer].astype(F32),
                       ((0, 0), (V7X_SUBLANES - (CONV_WIDTH - 1), 0), (0, 0)))
        ssd_y, st = _ssd(zx_c, dt_c, tail, state_ssm[layer].astype(F32).reshape(bs, inner, D_STATE), p,
                         batch=bs, n_chunks=1, valid=rows)
        ssd_y = ssd_y.reshape(bs, CHUNK, inner)[:, :rows].reshape(bs * rows, inner)
        outs_s[0].append(k32.reshape(bs, rows, heads, V_HEAD_DIM))
        outs_s[1].append(v32.reshape(bs, rows, heads, V_HEAD_DIM))
        outs_s[2].append(zx.reshape(bs, rows, 3 * inner)[:, rows - (CONV_WIDTH - 1):, inner:])
        outs_s[3].append(st.reshape(bs, ssd_heads, SSD_HEAD_DIM, D_STATE))
        xs = _out_and_ffn(xs, attn, ssd_y, p)

    y_prompt = _rmsnorm(xp, final_norm_w, F32, 256).reshape(bp, seq, d_model)
    y_sample = _rmsnorm(xs, final_norm_w, F32, bs * rows).reshape(bs, rows, d_model)
    return (y_prompt, y_sample, *(jnp.stack(o) for o in outs_p), *(jnp.stack(o) for o in outs_s))
```

```python
import functools
import math

import jax
import jax.numpy as jnp
import numpy as np
from jax import lax
from jax.experimental import pallas as pl
from jax.experimental.pallas import tpu as pltpu

F32 = jnp.float32
BF16 = jnp.bfloat16

CHUNK = 64
V_HEAD_DIM = 128
QK_DIM = V_HEAD_DIM // 2
ATTN_SCALE = QK_DIM ** -0.5
LOG2_E = math.log2(math.e)
SSD_HEAD_DIM = 64
SSD_GROUPS = 8
D_STATE = 128
CONV_WIDTH = 4
EPS = 1e-6

V7X_LANES = 128
V7X_SUBLANES = 8
V7X_VMEM_BYTES = 64 * 1024 * 1024
VMEM_LIMIT_BYTES = V7X_VMEM_BYTES - 8 * 1024 * 1024

NT_DIMS = (((1,), (1,)), ((), ()))


def _cparams(n_axes, flags=None):
    return pltpu.CompilerParams(
        dimension_semantics=("arbitrary",) * n_axes, vmem_limit_bytes=VMEM_LIMIT_BYTES, flags=flags)


def _sigmoid(x):
    return 1.0 / (1.0 + jnp.exp(-x))


def _lambda_init(layer):
    return 0.8 - 0.6 * math.exp(-0.3 * layer)


def _rmsnorm_body(x_ref, w_ref, o_ref):
    x = x_ref[...]
    ms = jnp.mean(x * x, axis=-1, keepdims=True)
    o_ref[...] = (x * lax.rsqrt(ms + EPS) * w_ref[...]).astype(o_ref.dtype)


def _rmsnorm(x, w, out_dtype, tr):
    m, d = x.shape
    return pl.pallas_call(
        _rmsnorm_body,
        grid=(m // tr,),
        in_specs=[pl.BlockSpec((tr, d), lambda i: (i, 0)),
                  pl.BlockSpec((1, d), lambda i: (0, 0))],
        out_specs=pl.BlockSpec((tr, d), lambda i: (i, 0)),
        out_shape=jax.ShapeDtypeStruct((m, d), out_dtype),
        compiler_params=_cparams(1),
        name="rmsnorm",
    )(x, w.reshape(1, d).astype(F32))


def _cast_cols_body(x_ref, o_ref):
    n_in = x_ref.shape[1]
    o_ref[:, :n_in] = x_ref[...].astype(o_ref.dtype)
    if o_ref.shape[1] > n_in:
        o_ref[:, n_in:] = jnp.zeros((o_ref.shape[0], o_ref.shape[1] - n_in), o_ref.dtype)


def _cast_cols(w, n_in, n_out, tr):
    rows = w.shape[0]
    assert rows % tr == 0 and n_in % V7X_LANES == 0 and n_out >= n_in
    return pl.pallas_call(
        _cast_cols_body,
        grid=(rows // tr,),
        in_specs=[pl.BlockSpec((tr, n_in), lambda i: (i, 0))],
        out_specs=pl.BlockSpec((tr, n_out), lambda i: (i, 0)),
        out_shape=jax.ShapeDtypeStruct((rows, n_out), BF16),
        compiler_params=_cparams(1),
        name="cast_cols",
    )(w)


def _cast_rows_body(x_ref, o_ref, *, n_blocks_in):
    keep = pl.program_id(0) < n_blocks_in
    o_ref[...] = jnp.where(keep, x_ref[...], 0.0).astype(o_ref.dtype)


def _cast_rows(w, rows_out, tr):
    rows, n = w.shape
    assert rows % tr == 0 and rows_out % tr == 0 and rows_out >= rows
    n_blocks_in = rows // tr
    return pl.pallas_call(
        functools.partial(_cast_rows_body, n_blocks_in=n_blocks_in),
        grid=(rows_out // tr,),
        in_specs=[pl.BlockSpec((tr, n), lambda i: (jnp.minimum(i, n_blocks_in - 1), 0))],
        out_specs=pl.BlockSpec((tr, n), lambda i: (i, 0)),
        out_shape=jax.ShapeDtypeStruct((rows_out, n), BF16),
        compiler_params=_cparams(1),
        name="cast_rows",
    )(w)


def _mm_body(*refs, n_pairs, scales, has_res):
    a_refs = refs[:n_pairs]
    b_refs = refs[n_pairs:2 * n_pairs]
    pos = 2 * n_pairs
    res_ref = refs[pos] if has_res else None
    out_refs = refs[pos + int(has_res):]
    acc = None
    for a_ref, b_ref in zip(a_refs, b_refs):
        d = jnp.dot(a_ref[...], b_ref[...], preferred_element_type=F32)
        acc = d if acc is None else acc + d
    if has_res:
        acc = acc + res_ref[...]
    for o_ref, s in zip(out_refs, scales):
        o_ref[...] = (acc if s == 1.0 else acc * s).astype(o_ref.dtype)


def _matmul(a_list, b_list, *, n, tm, tn, outs, b_row_blocks=None, b_col_off=0,
            residual=None, name="matmul"):
    m = a_list[0].shape[0]
    n_pairs = len(a_list)
    if b_row_blocks is None:
        b_row_blocks = [0] * n_pairs
    assert b_col_off % tn == 0 and n % tn == 0 and m % tm == 0
    col_blk = b_col_off // tn
    in_specs = []
    for a in a_list:
        in_specs.append(pl.BlockSpec((tm, a.shape[1]), lambda i, j: (i, 0)))
    for a, rb in zip(a_list, b_row_blocks):
        in_specs.append(pl.BlockSpec((a.shape[1], tn), lambda i, j, rb=rb: (rb, col_blk + j)))
    args = list(a_list) + list(b_list)
    if residual is not None:
        in_specs.append(pl.BlockSpec((tm, tn), lambda i, j: (i, j)))
        args.append(residual)
    out_shape = [jax.ShapeDtypeStruct((m, n), dt) for dt, _ in outs]
    out_specs = [pl.BlockSpec((tm, tn), lambda i, j: (i, j)) for _ in outs]
    res = pl.pallas_call(
        functools.partial(_mm_body, n_pairs=n_pairs, scales=tuple(s for _, s in outs),
                          has_res=residual is not None),
        grid=(m // tm, n // tn),
        in_specs=in_specs,
        out_specs=out_specs,
        out_shape=out_shape,
        compiler_params=_cparams(2),
        name=name,
    )(*args)
    return res


def _ffn_up_body(h_ref, wg_ref, wu_ref, o_ref):
    h = h_ref[...]
    g = jnp.dot(h, wg_ref[...], preferred_element_type=F32)
    u = jnp.dot(h, wu_ref[...], preferred_element_type=F32)
    o_ref[...] = (g * _sigmoid(g) * u).astype(o_ref.dtype)


def _ffn_up(h, wg, wu, *, tm, tn):
    m, d = h.shape
    n = wg.shape[1]
    return pl.pallas_call(
        _ffn_up_body,
        grid=(m // tm, n // tn),
        in_specs=[pl.BlockSpec((tm, d), lambda i, j: (i, 0)),
                  pl.BlockSpec((d, tn), lambda i, j: (0, j)),
                  pl.BlockSpec((d, tn), lambda i, j: (0, j))],
        out_specs=pl.BlockSpec((tm, tn), lambda i, j: (i, j)),
        out_shape=jax.ShapeDtypeStruct((m, n), BF16),
        compiler_params=_cparams(2),
        name="ffn_up",
    )(h, wg, wu)


def _mm_acc_body(a_ref, b_ref, r_ref, o_ref):
    @pl.when(pl.program_id(2) == 0)
    def _start_from_residual():
        o_ref[...] = r_ref[...]

    o_ref[...] += jnp.dot(a_ref[...], b_ref[...], preferred_element_type=F32)


def _matmul_acc(a, b, residual, *, tm, tn, tk):
    m, kdim = a.shape
    n = b.shape[1]
    assert m % tm == 0 and n % tn == 0 and kdim % tk == 0
    return pl.pallas_call(
        _mm_acc_body,
        grid=(m // tm, n // tn, kdim // tk),
        in_specs=[pl.BlockSpec((tm, tk), lambda i, j, k: (i, k)),
                  pl.BlockSpec((tk, tn), lambda i, j, k: (k, j)),
                  pl.BlockSpec((tm, tn), lambda i, j, k: (i, j))],
        out_specs=pl.BlockSpec((tm, tn), lambda i, j, k: (i, j)),
        out_shape=jax.ShapeDtypeStruct((m, n), F32),
        compiler_params=_cparams(3),
        name="ffn_down",
    )(a, b, residual)


def _lambda_value(lq1, lk1, lq2, lk2, lam0):
    d1 = jnp.sum(lq1[...] * lk1[...], axis=-1, keepdims=True)
    d2 = jnp.sum(lq2[...] * lk2[...], axis=-1, keepdims=True)
    return jnp.exp(d1) - jnp.exp(d2) + lam0


def _subln(o, w, lam0):
    ms = jnp.mean(o * o, axis=-1, keepdims=True)
    return (o * lax.rsqrt(ms + EPS) * w) * (1.0 - lam0)


def _attn_prompt_body(lq1, lk1, lq2, lk2, w_ref, q_ref, k_ref, v_ref, o_ref,
                      k1_scr, k2_scr, vt_scr, sa, sb, pa, pb, cma, cmb, ala, alb, m_scr, l_scr, acc_scr,
                      *, tq, tk, lam0):
    qi = pl.program_id(2)
    n_kv = k1_scr.shape[0]
    k_scrs = (k1_scr, k2_scr)

    @pl.when(qi == 0)
    def _prepare_head():
        first_half = lax.broadcasted_iota(jnp.int32, (tk, V_HEAD_DIM), 1) < QK_DIM
        for c in range(n_kv):
            kb = k_ref[pl.ds(c * tk, tk), :]
            zero = jnp.zeros_like(kb)
            k1_scr[c] = jnp.where(first_half, kb, zero)
            k2_scr[c] = jnp.where(first_half, zero, kb)
            vt_scr[c] = v_ref[pl.ds(c * tk, tk), :].T

    m_scr[...] = jnp.full(m_scr.shape, -jnp.inf, F32)
    l_scr[...] = jnp.zeros(l_scr.shape, F32)
    acc_scr[...] = jnp.zeros(acc_scr.shape, F32)
    pb[...] = jnp.zeros(pb.shape, BF16)
    alb[...] = jnp.ones(alb.shape, F32)

    def scores(j, s_scr, cm_scr, c0):
        qv = q_ref[c0:, :]
        for idx in range(2):
            s = lax.dot_general(k_scrs[idx][j], qv, NT_DIMS, preferred_element_type=F32)
            s_scr[idx, :, c0:] = s
            cm_scr[idx, :, c0:] = jnp.max(s, axis=0, keepdims=True)

    def softmax(s_scr, cm_scr, p_scr, al_scr, c0, mask):
        for idx in range(2):
            s = s_scr[idx, :, c0:]
            if mask is None:
                cmax = cm_scr[idx, :, c0:]
            else:
                s = jnp.where(mask, s, -jnp.inf)
                cmax = jnp.max(s, axis=0, keepdims=True)
            m_old = m_scr[idx, :, c0:]
            m_new = jnp.maximum(m_old, cmax)
            alpha = jnp.exp2(m_old - m_new)
            p = jnp.exp2(s - m_new)
            l_scr[idx, :, c0:] = alpha * l_scr[idx, :, c0:] + jnp.sum(p, axis=0, keepdims=True)
            p_scr[idx, :, c0:] = p.astype(BF16)
            al_scr[idx, :, c0:] = alpha
            m_scr[idx, :, c0:] = m_new

    def value_product(j, p_scr, al_scr, c0):
        vt = vt_scr[j]
        for idx in range(2):
            acc_scr[idx, :, c0:] = al_scr[idx, :, c0:] * acc_scr[idx, :, c0:] + jnp.dot(
                vt, p_scr[idx, :, c0:], preferred_element_type=F32)

    scores(0, sa, cma, 0)

    def pair(jj, carry):
        j = 2 * jj
        scores(j + 1, sb, cmb, 0)
        value_product(jnp.maximum(j - 1, 0), pb, alb, 0)
        softmax(sa, cma, pa, ala, 0, None)
        scores(j + 2, sa, cma, 0)
        value_product(j, pa, ala, 0)
        softmax(sb, cmb, pb, alb, 0, None)
        return carry

    lax.fori_loop(0, qi, pair, 0)

    j0 = 2 * qi
    key_chunk = lax.broadcasted_iota(jnp.int32, (tk, tq), 0) // CHUNK
    qry_chunk = lax.broadcasted_iota(jnp.int32, (tk, tq), 1) // CHUNK
    visible = key_chunk <= qry_chunk
    scores(j0 + 1, sb, cmb, tk)
    value_product(jnp.maximum(j0 - 1, 0), pb, alb, 0)
    softmax(sa, cma, pa, ala, 0, visible)
    softmax(sb, cmb, pb, alb, tk, visible[:, :tk])
    value_product(j0, pa, ala, 0)
    value_product(j0 + 1, pb, alb, tk)

    lam = _lambda_value(lq1, lk1, lq2, lk2, lam0)
    o_t = acc_scr[0] * (1.0 / l_scr[0]) - lam * (acc_scr[1] * (1.0 / l_scr[1]))
    o_ref[...] = _subln(o_t.T, w_ref[...], lam0).astype(o_ref.dtype)


def _attn_prompt(q, k, v, lams, subln_w, *, batch, seq, heads, tk, lam0):
    tq = 2 * tk
    nq = seq // tq
    n_kv = seq // tk
    lam_specs = [pl.BlockSpec((1, QK_DIM), lambda b, h, i: (0, 0))] * 4
    return pl.pallas_call(
        functools.partial(_attn_prompt_body, tq=tq, tk=tk, lam0=lam0),
        grid=(batch, heads, nq),
        in_specs=lam_specs + [
            pl.BlockSpec((1, V_HEAD_DIM), lambda b, h, i: (0, 0)),
            pl.BlockSpec((tq, V_HEAD_DIM), lambda b, h, i: (b * nq + i, h)),
            pl.BlockSpec((seq, V_HEAD_DIM), lambda b, h, i: (b, h)),
            pl.BlockSpec((seq, V_HEAD_DIM), lambda b, h, i: (b, h)),
        ],
        out_specs=pl.BlockSpec((tq, V_HEAD_DIM), lambda b, h, i: (b * nq + i, h)),
        out_shape=jax.ShapeDtypeStruct(q.shape, BF16),
        scratch_shapes=[
            pltpu.VMEM((n_kv, tk, V_HEAD_DIM), BF16),
            pltpu.VMEM((n_kv, tk, V_HEAD_DIM), BF16),
            pltpu.VMEM((n_kv, V_HEAD_DIM, tk), BF16),
            pltpu.VMEM((2, tk, tq), F32), pltpu.VMEM((2, tk, tq), F32),
            pltpu.VMEM((2, tk, tq), BF16), pltpu.VMEM((2, tk, tq), BF16),
            pltpu.VMEM((2, 1, tq), F32), pltpu.VMEM((2, 1, tq), F32),
            pltpu.VMEM((2, 1, tq), F32), pltpu.VMEM((2, 1, tq), F32),
            pltpu.VMEM((2, 1, tq), F32),
            pltpu.VMEM((2, 1, tq), F32),
            pltpu.VMEM((2, V_HEAD_DIM, tq), F32),
        ],
        compiler_params=_cparams(3),
        name="attn_prompt",
    )(*lams, subln_w, q, k, v)


def _attn_decode_body(lq1, lk1, lq2, lk2, w_ref, q_ref, kn_ref, vn_ref, kc_ref, vc_ref, o_ref,
                      m_scr, l_scr, acc_scr, *, rows, lam0):
    kb = pl.program_id(2)
    group = kc_ref.shape[1]
    n_q = group * rows

    def stack_heads(ref):
        x = ref[...]
        return jnp.concatenate(
            [x[:, h * V_HEAD_DIM:(h + 1) * V_HEAD_DIM] for h in range(group)], axis=0)

    q = stack_heads(q_ref)
    first_half = lax.broadcasted_iota(jnp.int32, q.shape, 1) < QK_DIM
    zero = jnp.zeros_like(q)
    q_streams = (jnp.where(first_half, q, zero), jnp.where(first_half, zero, q))

    def accumulate(k2d, v2d, same_head):
        for idx in range(2):
            s = lax.dot_general(q_streams[idx], k2d, NT_DIMS, preferred_element_type=F32)
            s = jnp.where(same_head, s, -jnp.inf)
            m_old = m_scr[idx]
            m_new = jnp.maximum(m_old, jnp.max(s, axis=-1, keepdims=True))
            alpha = jnp.exp2(m_old - m_new)
            p = jnp.exp2(s - m_new)
            l_scr[idx] = alpha * l_scr[idx] + jnp.sum(p, axis=-1, keepdims=True)
            acc_scr[idx] = alpha * acc_scr[idx] + jnp.dot(
                p.astype(BF16), v2d, preferred_element_type=F32)
            m_scr[idx] = m_new

    @pl.when(kb == 0)
    def _new_rows():
        m_scr[...] = jnp.full(m_scr.shape, -jnp.inf, F32)
        l_scr[...] = jnp.zeros(l_scr.shape, F32)
        acc_scr[...] = jnp.zeros(acc_scr.shape, F32)
        q_head = lax.broadcasted_iota(jnp.int32, (n_q, n_q), 0) // rows
        k_head = lax.broadcasted_iota(jnp.int32, (n_q, n_q), 1) // rows
        accumulate(stack_heads(kn_ref), stack_heads(vn_ref), q_head == k_head)

    n_keys = kc_ref.shape[0] * group
    q_head = lax.broadcasted_iota(jnp.int32, (n_q, n_keys), 0) // rows
    k_head = lax.broadcasted_iota(jnp.int32, (n_q, n_keys), 1) % group
    accumulate(kc_ref[...].reshape(n_keys, V_HEAD_DIM).astype(BF16),
               vc_ref[...].reshape(n_keys, V_HEAD_DIM).astype(BF16), q_head == k_head)

    @pl.when(kb == pl.num_programs(2) - 1)
    def _finish():
        lam = _lambda_value(lq1, lk1, lq2, lk2, lam0)
        o = acc_scr[0] * (1.0 / l_scr[0]) - lam * (acc_scr[1] * (1.0 / l_scr[1]))
        o = _subln(o, w_ref[...], lam0).astype(o_ref.dtype)
        for h in range(group):
            o_ref[:, h * V_HEAD_DIM:(h + 1) * V_HEAD_DIM] = o[h * rows:(h + 1) * rows, :]


def _attn_decode(q, k_new, v_new, cache_k, cache_v, lams, subln_w, *, batch, rows, heads, lam0):
    past = cache_k.shape[1]
    group = V7X_SUBLANES
    t_kv = min(past, 512)
    assert heads % group == 0 and past % t_kv == 0
    gw = group * V_HEAD_DIM
    lam_specs = [pl.BlockSpec((1, QK_DIM), lambda b, g, k: (0, 0))] * 4
    new_spec = pl.BlockSpec((rows, gw), lambda b, g, k: (b, g))
    cache_spec = pl.BlockSpec((None, t_kv, group, V_HEAD_DIM), lambda b, g, k: (b, k, g, 0))
    return pl.pallas_call(
        functools.partial(_attn_decode_body, rows=rows, lam0=lam0),
        grid=(batch, heads // group, past // t_kv),
        in_specs=lam_specs + [
            pl.BlockSpec((1, V_HEAD_DIM), lambda b, g, k: (0, 0)),
            new_spec, new_spec, new_spec, cache_spec, cache_spec,
        ],
        out_specs=new_spec,
        out_shape=jax.ShapeDtypeStruct(q.shape, BF16),
        scratch_shapes=[
            pltpu.VMEM((2, group * rows, 1), F32),
            pltpu.VMEM((2, group * rows, 1), F32),
            pltpu.VMEM((2, group * rows, V_HEAD_DIM), F32),
        ],
        compiler_params=_cparams(3),
        name="attn_decode",
    )(*lams, subln_w, q, k_new, v_new, cache_k, cache_v)


def _split3(x):
    hi = x.astype(BF16)
    r1 = x - hi.astype(F32)
    mid = r1.astype(BF16)
    lo = (r1 - mid.astype(F32)).astype(BF16)
    return hi, mid, lo


def _ssd_body(z_ref, x_ref, bc_ref, dt_ref, tail_ref, st0_ref, cw_ref, cb_ref, dtb_ref, alog_ref,
              dsk_ref, nw_ref, e_ref, tril_ref, y_ref, stout_ref, xpad_scr, st_scr,
              *, valid, heads_per_group):
    c = pl.program_id(1)
    L = CHUNK
    inner = z_ref.shape[1]
    gw = heads_per_group * SSD_HEAD_DIM
    n_groups = inner // gw
    gn = D_STATE
    tail_rows = V7X_SUBLANES

    @pl.when(c == 0)
    def _load_stream_state():
        xpad_scr[0:tail_rows, :] = tail_ref[...]
        st_scr[...] = st0_ref[...].T

    xpad_scr[tail_rows:tail_rows + L, 0:inner] = x_ref[...]
    xpad_scr[tail_rows:tail_rows + L, inner:] = bc_ref[...]
    conv = cb_ref[...]
    first = tail_rows - (CONV_WIDTH - 1)
    for i in range(CONV_WIDTH):
        conv = conv + cw_ref[i:i + 1, :] * xpad_scr[first + i:first + i + L, :]
    xpad_scr[0:tail_rows, :] = xpad_scr[L:L + tail_rows, :]
    act = conv * _sigmoid(conv)
    xs = act[:, :inner]
    bm = act[:, inner:inner + n_groups * gn]
    cm = act[:, inner + n_groups * gn:]

    dtv = dt_ref[...] + dtb_ref[...]
    dt = jnp.maximum(dtv, 0.0) + jnp.log1p(jnp.exp(-jnp.abs(dtv)))
    if valid < L:
        dt = jnp.where(lax.broadcasted_iota(jnp.int32, dt.shape, 0) < valid, dt, 0.0)
    d_a = dt * (-jnp.exp(alog_ref[...]))

    pieces = jnp.concatenate(_split3(d_a), axis=1)
    cs = jnp.dot(tril_ref[...], pieces, preferred_element_type=F32)
    w = d_a.shape[1]
    a_cs = (cs[:, :w] + cs[:, w:2 * w]) + cs[:, 2 * w:]

    both = jnp.concatenate([a_cs, dt], axis=0)
    pieces = jnp.concatenate(_split3(both), axis=0)
    ex = jnp.dot(pieces, e_ref[...], preferred_element_type=F32)
    ex = (ex[:2 * L] + ex[2 * L:4 * L]) + ex[4 * L:]
    acol = ex[:L]
    dtx = ex[L:]

    row = lax.broadcasted_iota(jnp.int32, (L, inner), 0)
    sub = lax.broadcasted_iota(jnp.int32, (L, inner), 1) & (SSD_HEAD_DIM - 1)
    arow = jnp.sum(jnp.where(row == sub, acol, 0.0), axis=0, keepdims=True)
    decay_in = jnp.exp(jnp.where(sub <= row, acol - arow, -jnp.inf))
    a_last = acol[L - 1:L, :]
    decay_to_end = jnp.exp(a_last - acol)
    decay_from_start = jnp.exp(acol)
    chunk_decay = jnp.exp(a_last)

    xdt = xs * dtx
    xdt_bf = xdt.astype(BF16)
    xd_end = (xdt * decay_to_end).astype(BF16)
    bm_t = jnp.concatenate([bm, jnp.zeros_like(bm)], axis=0).T
    zeros_rows = jnp.zeros((L, gw), BF16)
    lane_head = lax.broadcasted_iota(jnp.int32, (L, gw), 1) // SSD_HEAD_DIM

    for g in range(n_groups):
        hs = slice(g * gw, (g + 1) * gw)
        ns = slice(g * gn, (g + 1) * gn)
        b_bf = bm[:, ns].astype(BF16)
        c_bf = cm[:, ns].astype(BF16)
        cb = lax.dot_general(c_bf, jnp.concatenate([b_bf] * heads_per_group, axis=0), NT_DIMS,
                             preferred_element_type=F32)
        lhs = (cb * decay_in[:, hs]).astype(BF16)
        xg = xdt_bf[:, hs]
        zero = jnp.zeros_like(xg)
        block_diag = jnp.concatenate(
            [jnp.where(lane_head == r, xg, zero) for r in range(heads_per_group)], axis=0)
        y = jnp.dot(lhs, block_diag, preferred_element_type=F32)
        st = st_scr[:, hs]
        y = y + jnp.dot(c_bf, st.astype(BF16), preferred_element_type=F32) * decay_from_start[:, hs]
        bt = bm_t[ns, :].astype(BF16)
        st_scr[:, hs] = chunk_decay[:, hs] * st + jnp.dot(
            bt, jnp.concatenate([xd_end[:, hs], zeros_rows], axis=0), preferred_element_type=F32)
        y = y + dsk_ref[:, hs] * xs[:, hs]
        zg = z_ref[:, hs]
        y = y * (zg * _sigmoid(zg))
        y = y * lax.rsqrt(jnp.mean(y * y, axis=-1, keepdims=True) + EPS)
        y_ref[:, hs] = (y * nw_ref[:, hs]).astype(y_ref.dtype)

    @pl.when(c == pl.num_programs(1) - 1)
    def _store_stream_state():
        stout_ref[...] = st_scr[...].T


def _ssd(zx, dt_raw, tail, st0, p, *, batch, n_chunks, valid):
    rows = zx.shape[0]
    inner = zx.shape[1] // 3
    conv_dim = 2 * inner
    heads = inner // SSD_HEAD_DIM
    L = CHUNK
    assert SSD_HEAD_DIM == CHUNK and conv_dim == inner + 2 * SSD_GROUPS * D_STATE
    expand = np.zeros((V7X_LANES, inner), np.float32)
    for h in range(heads):
        expand[h, h * SSD_HEAD_DIM:(h + 1) * SSD_HEAD_DIM] = 1.0
    tril = np.tril(np.ones((L, L), np.float32))

    def row_blk(col):
        return pl.BlockSpec((L, inner), lambda b, c, col=col: (b * n_chunks + c, col))

    def const(shape):
        return pl.BlockSpec(shape, lambda b, c: (0,) * len(shape))

    return pl.pallas_call(
        functools.partial(_ssd_body, valid=valid, heads_per_group=heads // SSD_GROUPS),
        grid=(batch, n_chunks),
        in_specs=[
            row_blk(0), row_blk(1), row_blk(2),
            pl.BlockSpec((L, V7X_LANES), lambda b, c: (b * n_chunks + c, 0)),
            pl.BlockSpec((None, V7X_SUBLANES, conv_dim), lambda b, c: (b, 0, 0)),
            pl.BlockSpec((None, inner, D_STATE), lambda b, c: (b, 0, 0)),
            const((CONV_WIDTH, conv_dim)), const((1, conv_dim)),
            const((1, V7X_LANES)), const((1, V7X_LANES)),
            const((1, inner)), const((1, inner)),
            const((V7X_LANES, inner)), const((L, L)),
        ],
        out_specs=[
            pl.BlockSpec((L, inner), lambda b, c: (b * n_chunks + c, 0)),
            pl.BlockSpec((None, inner, D_STATE), lambda b, c: (b, 0, 0)),
        ],
        out_shape=[
            jax.ShapeDtypeStruct((rows, inner), BF16),
            jax.ShapeDtypeStruct((batch, inner, D_STATE), F32),
        ],
        scratch_shapes=[
            pltpu.VMEM((L + V7X_SUBLANES, conv_dim), F32),
            pltpu.VMEM((D_STATE, inner), F32),
        ],
        compiler_params=_cparams(2),
        name="ssd",
    )(zx, zx, zx, dt_raw, tail, st0, p["conv_w"], p["conv_b"], p["dt_bias"], p["a_log"],
      p["d_skip"], p["ssd_norm_w"], jnp.asarray(expand, BF16), jnp.asarray(tril, BF16))


def _pad_lanes(v, width):
    return jnp.pad(v.astype(F32), (0, width - v.shape[0])).reshape(1, width)


def _prepare_layer(layer, norm1_w, w_in, lambda_q1, lambda_k1, lambda_q2, lambda_k2, subln_w,
                   conv_w, conv_b, dt_bias, A_log, D_skip, ssd_norm_w, w_out, norm2_w,
                   w_gate, w_up, w_down, ff_tile):
    d_model = w_in.shape[1]
    attn_w = 3 * (w_out.shape[1] // 2)
    inner = w_out.shape[1] // 2
    main_cols = attn_w + inner + 2 * inner
    cast_rows = 128
    w_in_bf = _cast_cols(w_in[layer], main_cols, main_cols, cast_rows)
    n_heads = w_in.shape[2] - main_cols
    w_dt = jnp.pad(w_in[layer][:, main_cols:].astype(BF16), ((0, 0), (0, V7X_LANES - n_heads)))
    d_ff = w_gate.shape[2]
    ff_width = d_ff + (-d_ff % ff_tile)
    return {
        "lam0": _lambda_init(layer),
        "norm1_w": norm1_w[layer], "norm2_w": norm2_w[layer],
        "w_in": w_in_bf, "w_dt": w_dt, "main_cols": main_cols,
        "lams": [v[layer].reshape(1, -1).astype(F32)
                 for v in (lambda_q1, lambda_k1, lambda_q2, lambda_k2)],
        "subln_w": subln_w[layer].reshape(1, -1).astype(F32),
        "conv_w": conv_w[layer].astype(F32), "conv_b": conv_b[layer].reshape(1, -1).astype(F32),
        "dt_bias": _pad_lanes(dt_bias[layer], V7X_LANES),
        "a_log": _pad_lanes(A_log[layer], V7X_LANES),
        "d_skip": jnp.repeat(D_skip[layer].astype(F32), SSD_HEAD_DIM).reshape(1, -1),
        "ssd_norm_w": ssd_norm_w[layer].reshape(1, -1).astype(F32),
        "w_out": _cast_cols(w_out[layer], d_model, d_model, cast_rows),
        "w_gate": _cast_cols(w_gate[layer], d_ff, ff_width, cast_rows),
        "w_up": _cast_cols(w_up[layer], d_ff, ff_width, cast_rows),
        "w_down": _cast_rows(w_down[layer], ff_width, 2 * cast_rows),
        "d_model": d_model, "inner": inner,
    }


def _tiles(m):
    return 1024 if m % 1024 == 0 else m


def _in_proj(x2d, p):
    m = x2d.shape[0]
    inner = p["inner"]
    tm = _tiles(m)
    h = _rmsnorm(x2d, p["norm1_w"], BF16, min(256, m))
    q, = _matmul([h], [p["w_in"]], n=inner, tm=tm, tn=1024, b_col_off=0,
                 outs=[(BF16, ATTN_SCALE * LOG2_E)], name="proj_q")
    k32, kbf = _matmul([h], [p["w_in"]], n=inner, tm=tm, tn=1024, b_col_off=inner,
                       outs=[(F32, 1.0), (BF16, 1.0)], name="proj_k")
    v32, vbf = _matmul([h], [p["w_in"]], n=inner, tm=tm, tn=1024, b_col_off=2 * inner,
                       outs=[(F32, 1.0), (BF16, 1.0)], name="proj_v")
    zx, = _matmul([h], [p["w_in"]], n=3 * inner, tm=tm, tn=1024, b_col_off=3 * inner,
                  outs=[(F32, 1.0)], name="proj_zx")
    dt, = _matmul([h], [p["w_dt"]], n=V7X_LANES, tm=tm, tn=V7X_LANES,
                  outs=[(F32, 1.0)], name="proj_dt")
    return q, k32, kbf, v32, vbf, zx, dt


def _out_and_ffn(x2d, attn, ssd_y, p):
    m = x2d.shape[0]
    tm = _tiles(m)
    x1, = _matmul([attn, ssd_y], [p["w_out"], p["w_out"]], b_row_blocks=[0, 1], n=p["d_model"],
                  tm=tm, tn=min(1024, p["d_model"]), outs=[(F32, 1.0)], residual=x2d, name="out_proj")
    h2 = _rmsnorm(x1, p["norm2_w"], BF16, min(256, m))
    act = _ffn_up(h2, p["w_gate"], p["w_up"], tm=tm, tn=512 if tm == 1024 else 1024)
    return _matmul_acc(act, p["w_down"], x1, tm=tm, tn=min(1024, p["d_model"]), tk=1024)


def kernel(x_prompt, x_sample, cache_k, cache_v, state_conv, state_ssm, norm1_w, w_in, lambda_q1, lambda_k1, lambda_q2, lambda_k2, subln_w, conv_w, conv_b, dt_bias, A_log, D_skip, ssd_norm_w, w_out, norm2_w, w_gate, w_up, w_down, final_norm_w):
    depth = w_in.shape[0]
    bp, seq, d_model = x_prompt.shape
    bs, rows, _ = x_sample.shape
    heads = cache_k.shape[3]
    inner = w_out.shape[1] // 2
    attn_w = heads * V_HEAD_DIM
    conv_dim = conv_w.shape[2]
    ssd_heads = state_ssm.shape[2]
    assert seq % CHUNK == 0 and rows <= CHUNK and rows >= CONV_WIDTH - 1 and attn_w == inner

    xp = x_prompt.reshape(bp * seq, d_model)
    xs = x_sample.reshape(bs * rows, d_model)
    outs_p = [[], [], [], []]
    outs_s = [[], [], [], []]
    for layer in range(depth):
        p = _prepare_layer(layer, norm1_w, w_in, lambda_q1, lambda_k1, lambda_q2, lambda_k2, subln_w,
                           conv_w, conv_b, dt_bias, A_log, D_skip, ssd_norm_w, w_out, norm2_w,
                           w_gate, w_up, w_down, ff_tile=1024)
        lam0 = p["lam0"]

        q, k32, kbf, v32, vbf, zx, dt = _in_proj(xp, p)
        attn = _attn_prompt(q, kbf, vbf, p["lams"], p["subln_w"], batch=bp, seq=seq, heads=heads,
                            tk=512, lam0=lam0)
        ssd_y, st = _ssd(zx, dt, jnp.zeros((bp, V7X_SUBLANES, conv_dim), F32),
                         jnp.zeros((bp, inner, D_STATE), F32), p,
                         batch=bp, n_chunks=seq // CHUNK, valid=CHUNK)
        outs_p[0].append(k32.reshape(bp, seq, heads, V_HEAD_DIM))
        outs_p[1].append(v32.reshape(bp, seq, heads, V_HEAD_DIM))
        outs_p[2].append(zx.reshape(bp, seq, 3 * inner)[:, seq - (CONV_WIDTH - 1):, inner:])
        outs_p[3].append(st.reshape(bp, ssd_heads, SSD_HEAD_DIM, D_STATE))
        xp = _out_and_ffn(xp, attn, ssd_y, p)

        q, k32, kbf, v32, vbf, zx, dt = _in_proj(xs, p)
        attn = _attn_decode(q, kbf, vbf, cache_k[layer], cache_v[layer],
                            p["lams"], p["subln_w"], batch=bs, rows=rows, heads=heads, lam0=lam0)
        pad_chunk = ((0, 0), (0, CHUNK - rows), (0, 0))
        zx_c = jnp.pad(zx.reshape(bs, rows, 3 * inner), pad_chunk).reshape(bs * CHUNK, 3 * inner)
        dt_c = jnp.pad(dt.reshape(bs, rows, V7X_LANES), pad_chunk).reshape(bs * CHUNK, V7X_LANES)
        tail = jnp.pad(state_conv[layer].astype(F32),
                       ((0, 0), (V7X_SUBLANES - (CONV_WIDTH - 1), 0), (0, 0)))
        ssd_y, st = _ssd(zx_c, dt_c, tail, state_ssm[layer].astype(F32).reshape(bs, inner, D_STATE), p,
                         batch=bs, n_chunks=1, valid=rows)
        ssd_y = ssd_y.reshape(bs, CHUNK, inner)[:, :rows].reshape(bs * rows, inner)
        outs_s[0].append(k32.reshape(bs, rows, heads, V_HEAD_DIM))
        outs_s[1].append(v32.reshape(bs, rows, heads, V_HEAD_DIM))
        outs_s[2].append(zx.reshape(bs, rows, 3 * inner)[:, rows - (CONV_WIDTH - 1):, inner:])
        outs_s[3].append(st.reshape(bs, ssd_heads, SSD_HEAD_DIM, D_STATE))
        xs = _out_and_ffn(xs, attn, ssd_y, p)

    y_prompt = _rmsnorm(xp, final_norm_w, F32, 256).reshape(bp, seq, d_model)
    y_sample = _rmsnorm(xs, final_norm_w, F32, bs * rows).reshape(bs, rows, d_model)
    return (y_prompt, y_sample, *(jnp.stack(o) for o in outs_p), *(jnp.stack(o) for o in outs_s))
```

```python
import functools
import math

import jax
import jax.numpy as jnp
import numpy as np
from jax import lax
from jax.experimental import pallas as pl
from jax.experimental.pallas import tpu as pltpu

F32 = jnp.float32
BF16 = jnp.bfloat16

CHUNK = 64
V_HEAD_DIM = 128
QK_DIM = V_HEAD_DIM // 2
ATTN_SCALE = QK_DIM ** -0.5
LOG2_E = math.log2(math.e)
SSD_HEAD_DIM = 64
SSD_GROUPS = 8
D_STATE = 128
CONV_WIDTH = 4
EPS = 1e-6

V7X_LANES = 128
V7X_SUBLANES = 8
BF16_SUBLANES = 2 * V7X_SUBLANES
V7X_VMEM_BYTES = 64 * 1024 * 1024
VMEM_LIMIT_BYTES = V7X_VMEM_BYTES - 8 * 1024 * 1024

NT_DIMS = (((1,), (1,)), ((), ()))


def _cparams(n_axes, flags=None):
    return pltpu.CompilerParams(
        dimension_semantics=("arbitrary",) * n_axes, vmem_limit_bytes=VMEM_LIMIT_BYTES, flags=flags)


def _sigmoid(x):
    return 1.0 / (1.0 + jnp.exp(-x))


def _lambda_init(layer):
    return 0.8 - 0.6 * math.exp(-0.3 * layer)


def _rmsnorm_body(x_ref, w_ref, o_ref):
    x = x_ref[...]
    ms = jnp.mean(x * x, axis=-1, keepdims=True)
    o_ref[...] = (x * lax.rsqrt(ms + EPS) * w_ref[...]).astype(o_ref.dtype)


def _rmsnorm(x, w, out_dtype, tr):
    m, d = x.shape
    return pl.pallas_call(
        _rmsnorm_body,
        grid=(m // tr,),
        in_specs=[pl.BlockSpec((tr, d), lambda i: (i, 0)),
                  pl.BlockSpec((1, d), lambda i: (0, 0))],
        out_specs=pl.BlockSpec((tr, d), lambda i: (i, 0)),
        out_shape=jax.ShapeDtypeStruct((m, d), out_dtype),
        compiler_params=_cparams(1),
        name="rmsnorm",
    )(x, w.reshape(1, d).astype(F32))


def _cast_cols_body(x_ref, o_ref):
    n_in = x_ref.shape[1]
    o_ref[:, :n_in] = x_ref[...].astype(o_ref.dtype)
    if o_ref.shape[1] > n_in:
        o_ref[:, n_in:] = jnp.zeros((o_ref.shape[0], o_ref.shape[1] - n_in), o_ref.dtype)


def _cast_cols(w, n_in, n_out, tr):
    rows = w.shape[0]
    assert rows % tr == 0 and n_in % V7X_LANES == 0 and n_out >= n_in
    return pl.pallas_call(
        _cast_cols_body,
        grid=(rows // tr,),
        in_specs=[pl.BlockSpec((tr, n_in), lambda i: (i, 0))],
        out_specs=pl.BlockSpec((tr, n_out), lambda i: (i, 0)),
        out_shape=jax.ShapeDtypeStruct((rows, n_out), BF16),
        compiler_params=_cparams(1),
        name="cast_cols",
    )(w)


def _cast_rows_body(x_ref, o_ref, *, n_blocks_in):
    keep = pl.program_id(0) < n_blocks_in
    o_ref[...] = jnp.where(keep, x_ref[...], 0.0).astype(o_ref.dtype)


def _cast_rows(w, rows_out, tr):
    rows, n = w.shape
    assert rows_out % tr == 0 and (rows_out <= rows or rows % tr == 0)
    n_blocks_in = rows // tr
    return pl.pallas_call(
        functools.partial(_cast_rows_body, n_blocks_in=n_blocks_in),
        grid=(rows_out // tr,),
        in_specs=[pl.BlockSpec((tr, n), lambda i: (jnp.minimum(i, n_blocks_in - 1), 0))],
        out_specs=pl.BlockSpec((tr, n), lambda i: (i, 0)),
        out_shape=jax.ShapeDtypeStruct((rows_out, n), BF16),
        compiler_params=_cparams(1),
        name="cast_rows",
    )(w)


def _mm_body(*refs, n_pairs, scales, has_res, b_transposed):
    a_refs = refs[:n_pairs]
    b_refs = refs[n_pairs:2 * n_pairs]
    pos = 2 * n_pairs
    res_ref = refs[pos] if has_res else None
    out_refs = refs[pos + int(has_res):]
    acc = None
    for a_ref, b_ref in zip(a_refs, b_refs):
        if b_transposed:
            d = lax.dot_general(a_ref[...], b_ref[...], NT_DIMS, preferred_element_type=F32)
        else:
            d = jnp.dot(a_ref[...], b_ref[...], preferred_element_type=F32)
        acc = d if acc is None else acc + d
    if has_res:
        acc = acc + res_ref[...]
    for o_ref, s in zip(out_refs, scales):
        o_ref[...] = (acc if s == 1.0 else acc * s).astype(o_ref.dtype)


def _matmul(a_list, b_list, *, n, tm, tn, outs, b_row_blocks=None, b_col_off=0,
            residual=None, b_transposed=False, name="matmul"):
    m = a_list[0].shape[0]
    n_pairs = len(a_list)
    if b_row_blocks is None:
        b_row_blocks = [0] * n_pairs
    assert b_col_off % tn == 0 and n % tn == 0 and m % tm == 0
    col_blk = b_col_off // tn
    in_specs = []
    for a in a_list:
        in_specs.append(pl.BlockSpec((tm, a.shape[1]), lambda i, j: (i, 0)))
    for a, rb in zip(a_list, b_row_blocks):
        if b_transposed:
            in_specs.append(pl.BlockSpec((tn, a.shape[1]), lambda i, j, rb=rb: (col_blk + j, rb)))
        else:
            in_specs.append(pl.BlockSpec((a.shape[1], tn), lambda i, j, rb=rb: (rb, col_blk + j)))
    args = list(a_list) + list(b_list)
    if residual is not None:
        in_specs.append(pl.BlockSpec((tm, tn), lambda i, j: (i, j)))
        args.append(residual)
    out_shape = [jax.ShapeDtypeStruct((m, n), dt) for dt, _ in outs]
    out_specs = [pl.BlockSpec((tm, tn), lambda i, j: (i, j)) for _ in outs]
    res = pl.pallas_call(
        functools.partial(_mm_body, n_pairs=n_pairs, scales=tuple(s for _, s in outs),
                          has_res=residual is not None, b_transposed=b_transposed),
        grid=(m // tm, n // tn),
        in_specs=in_specs,
        out_specs=out_specs,
        out_shape=out_shape,
        compiler_params=_cparams(2),
        name=name,
    )(*args)
    return res


def _ffn_up_body(h_ref, wg_ref, wu_ref, o_ref):
    h = h_ref[...]
    g = jnp.dot(h, wg_ref[...], preferred_element_type=F32)
    u = jnp.dot(h, wu_ref[...], preferred_element_type=F32)
    o_ref[...] = (g * _sigmoid(g) * u).astype(o_ref.dtype)


def _ffn_up(h, wg, wu, *, tm, tn):
    m, d = h.shape
    n = wg.shape[1]
    return pl.pallas_call(
        _ffn_up_body,
        grid=(m // tm, n // tn),
        in_specs=[pl.BlockSpec((tm, d), lambda i, j: (i, 0)),
                  pl.BlockSpec((d, tn), lambda i, j: (0, j)),
                  pl.BlockSpec((d, tn), lambda i, j: (0, j))],
        out_specs=pl.BlockSpec((tm, tn), lambda i, j: (i, j)),
        out_shape=jax.ShapeDtypeStruct((m, n), BF16),
        compiler_params=_cparams(2),
        name="ffn_up",
    )(h, wg, wu)


def _mm_acc_body(a_ref, b_ref, r_ref, o_ref):
    @pl.when(pl.program_id(2) == 0)
    def _start_from_residual():
        o_ref[...] = r_ref[...]

    o_ref[...] += jnp.dot(a_ref[...], b_ref[...], preferred_element_type=F32)


def _matmul_acc(a, b, residual, *, tm, tn, tk):
    m, kdim = a.shape
    n = b.shape[1]
    assert m % tm == 0 and n % tn == 0 and kdim % tk == 0
    return pl.pallas_call(
        _mm_acc_body,
        grid=(m // tm, n // tn, kdim // tk),
        in_specs=[pl.BlockSpec((tm, tk), lambda i, j, k: (i, k)),
                  pl.BlockSpec((tk, tn), lambda i, j, k: (k, j)),
                  pl.BlockSpec((tm, tn), lambda i, j, k: (i, j))],
        out_specs=pl.BlockSpec((tm, tn), lambda i, j, k: (i, j)),
        out_shape=jax.ShapeDtypeStruct((m, n), F32),
        compiler_params=_cparams(3),
        name="ffn_down",
    )(a, b, residual)


def _lambda_value(lq1, lk1, lq2, lk2, lam0):
    d1 = jnp.sum(lq1[...] * lk1[...], axis=-1, keepdims=True)
    d2 = jnp.sum(lq2[...] * lk2[...], axis=-1, keepdims=True)
    return jnp.exp(d1) - jnp.exp(d2) + lam0


def _subln(o, w, lam0):
    ms = jnp.mean(o * o, axis=-1, keepdims=True)
    return (o * lax.rsqrt(ms + EPS) * w) * (1.0 - lam0)


def _attn_prompt_body(lq1, lk1, lq2, lk2, w_ref, q_ref, k_ref, v_ref, o_ref,
                      k1_scr, k2_scr, vt_scr, sa, sb, pa, pb, cma, cmb, ala, alb, m_scr, acc_scr,
                      *, tq, tk, lam0):
    qi = pl.program_id(2)
    n_kv = k1_scr.shape[0]
    k_scrs = (k1_scr, k2_scr)

    @pl.when(qi == 0)
    def _prepare_head():
        first_half = lax.broadcasted_iota(jnp.int32, (tk, V_HEAD_DIM), 1) < QK_DIM
        for c in range(n_kv):
            kb = k_ref[pl.ds(c * tk, tk), :]
            zero = jnp.zeros_like(kb)
            k1_scr[c] = jnp.where(first_half, kb, zero)
            k2_scr[c] = jnp.where(first_half, zero, kb)
            vt_scr[c, :V_HEAD_DIM, :] = v_ref[pl.ds(c * tk, tk), :].T
            ones_row = lax.broadcasted_iota(jnp.int32, (BF16_SUBLANES, tk), 0) == 0
            vt_scr[c, V_HEAD_DIM:, :] = jnp.where(ones_row, 1.0, 0.0).astype(BF16)

    m_scr[...] = jnp.full(m_scr.shape, -jnp.inf, F32)
    acc_scr[...] = jnp.zeros(acc_scr.shape, F32)
    pb[...] = jnp.zeros(pb.shape, BF16)
    alb[...] = jnp.ones(alb.shape, F32)

    def scores(j, s_scr, cm_scr, c0):
        qv = q_ref[c0:, :]
        for idx in range(2):
            s = lax.dot_general(k_scrs[idx][j], qv, NT_DIMS, preferred_element_type=F32)
            s_scr[idx, :, c0:] = s
            cm_scr[idx, :, c0:] = jnp.max(s, axis=0, keepdims=True)

    def softmax(s_scr, cm_scr, p_scr, al_scr, c0, mask):
        for idx in range(2):
            s = s_scr[idx, :, c0:]
            if mask is None:
                cmax = cm_scr[idx, :, c0:]
            else:
                s = jnp.where(mask, s, -jnp.inf)
                cmax = jnp.max(s, axis=0, keepdims=True)
            m_old = m_scr[idx, :, c0:]
            m_new = jnp.maximum(m_old, cmax)
            alpha = jnp.exp2(m_old - m_new)
            p = jnp.exp2(s - m_new)
            p_scr[idx, :, c0:] = p.astype(BF16)
            al_scr[idx, :, c0:] = alpha
            m_scr[idx, :, c0:] = m_new

    def value_product(j, p_scr, al_scr, c0):
        vt = vt_scr[j]
        for idx in range(2):
            acc_scr[idx, :, c0:] = al_scr[idx, :, c0:] * acc_scr[idx, :, c0:] + jnp.dot(
                vt, p_scr[idx, :, c0:], preferred_element_type=F32)

    scores(0, sa, cma, 0)

    def pair(jj, carry):
        j = 2 * jj
        scores(j + 1, sb, cmb, 0)
        value_product(jnp.maximum(j - 1, 0), pb, alb, 0)
        softmax(sa, cma, pa, ala, 0, None)
        scores(j + 2, sa, cma, 0)
        value_product(j, pa, ala, 0)
        softmax(sb, cmb, pb, alb, 0, None)
        return carry

    lax.fori_loop(0, qi, pair, 0)

    j0 = 2 * qi
    key_chunk = lax.broadcasted_iota(jnp.int32, (tk, tq), 0) // CHUNK
    qry_chunk = lax.broadcasted_iota(jnp.int32, (tk, tq), 1) // CHUNK
    visible = key_chunk <= qry_chunk
    scores(j0 + 1, sb, cmb, tk)
    value_product(jnp.maximum(j0 - 1, 0), pb, alb, 0)
    softmax(sa, cma, pa, ala, 0, visible)
    softmax(sb, cmb, pb, alb, tk, visible[:, :tk])
    value_product(j0, pa, ala, 0)
    value_product(j0 + 1, pb, alb, tk)

    lam = _lambda_value(lq1, lk1, lq2, lk2, lam0)
    outs = [acc_scr[idx, :V_HEAD_DIM, :] * (1.0 / acc_scr[idx, V_HEAD_DIM:V_HEAD_DIM + 1, :])
            for idx in range(2)]
    o_t = outs[0] - lam * outs[1]
    o_ref[...] = _subln(o_t.T, w_ref[...], lam0).astype(o_ref.dtype)


def _attn_prompt(q, k, v, lams, subln_w, *, batch, seq, heads, tk, lam0):
    tq = 2 * tk
    nq = seq // tq
    n_kv = seq // tk
    lam_specs = [pl.BlockSpec((1, QK_DIM), lambda b, h, i: (0, 0))] * 4
    return pl.pallas_call(
        functools.partial(_attn_prompt_body, tq=tq, tk=tk, lam0=lam0),
        grid=(batch, heads, nq),
        in_specs=lam_specs + [
            pl.BlockSpec((1, V_HEAD_DIM), lambda b, h, i: (0, 0)),
            pl.BlockSpec((tq, V_HEAD_DIM), lambda b, h, i: (b * nq + i, h)),
            pl.BlockSpec((seq, V_HEAD_DIM), lambda b, h, i: (b, h)),
            pl.BlockSpec((seq, V_HEAD_DIM), lambda b, h, i: (b, h)),
        ],
        out_specs=pl.BlockSpec((tq, V_HEAD_DIM), lambda b, h, i: (b * nq + i, h)),
        out_shape=jax.ShapeDtypeStruct(q.shape, BF16),
        scratch_shapes=[
            pltpu.VMEM((n_kv, tk, V_HEAD_DIM), BF16),
            pltpu.VMEM((n_kv, tk, V_HEAD_DIM), BF16),
            pltpu.VMEM((n_kv, V_HEAD_DIM + BF16_SUBLANES, tk), BF16),
            pltpu.VMEM((2, tk, tq), F32), pltpu.VMEM((2, tk, tq), F32),
            pltpu.VMEM((2, tk, tq), BF16), pltpu.VMEM((2, tk, tq), BF16),
            pltpu.VMEM((2, 1, tq), F32), pltpu.VMEM((2, 1, tq), F32),
            pltpu.VMEM((2, 1, tq), F32), pltpu.VMEM((2, 1, tq), F32),
            pltpu.VMEM((2, 1, tq), F32),
            pltpu.VMEM((2, V_HEAD_DIM + BF16_SUBLANES, tq), F32),
        ],
        compiler_params=_cparams(3),
        name="attn_prompt",
    )(*lams, subln_w, q, k, v)


def _attn_decode_body(lq1, lk1, lq2, lk2, w_ref, q_ref, kn_ref, vn_ref, kc_ref, vc_ref, o_ref,
                      m_scr, l_scr, acc_scr, *, rows, lam0):
    kb = pl.program_id(2)
    group = kc_ref.shape[1]
    n_q = group * rows

    def stack_heads(ref):
        x = ref[...]
        return jnp.concatenate(
            [x[:, h * V_HEAD_DIM:(h + 1) * V_HEAD_DIM] for h in range(group)], axis=0)

    q = stack_heads(q_ref)
    first_half = lax.broadcasted_iota(jnp.int32, q.shape, 1) < QK_DIM
    zero = jnp.zeros_like(q)
    q_streams = (jnp.where(first_half, q, zero), jnp.where(first_half, zero, q))

    def accumulate(k2d, v2d, same_head):
        for idx in range(2):
            s = lax.dot_general(q_streams[idx], k2d, NT_DIMS, preferred_element_type=F32)
            s = jnp.where(same_head, s, -jnp.inf)
            m_old = m_scr[idx]
            m_new = jnp.maximum(m_old, jnp.max(s, axis=-1, keepdims=True))
            alpha = jnp.exp2(m_old - m_new)
            p = jnp.exp2(s - m_new)
            l_scr[idx] = alpha * l_scr[idx] + jnp.sum(p, axis=-1, keepdims=True)
            acc_scr[idx] = alpha * acc_scr[idx] + jnp.dot(
                p.astype(BF16), v2d, preferred_element_type=F32)
            m_scr[idx] = m_new

    @pl.when(kb == 0)
    def _new_rows():
        m_scr[...] = jnp.full(m_scr.shape, -jnp.inf, F32)
        l_scr[...] = jnp.zeros(l_scr.shape, F32)
        acc_scr[...] = jnp.zeros(acc_scr.shape, F32)
        q_head = lax.broadcasted_iota(jnp.int32, (n_q, n_q), 0) // rows
        k_head = lax.broadcasted_iota(jnp.int32, (n_q, n_q), 1) // rows
        accumulate(stack_heads(kn_ref), stack_heads(vn_ref), q_head == k_head)

    n_keys = kc_ref.shape[0] * group
    q_head = lax.broadcasted_iota(jnp.int32, (n_q, n_keys), 0) // rows
    k_head = lax.broadcasted_iota(jnp.int32, (n_q, n_keys), 1) % group
    accumulate(kc_ref[...].reshape(n_keys, V_HEAD_DIM).astype(BF16),
               vc_ref[...].reshape(n_keys, V_HEAD_DIM).astype(BF16), q_head == k_head)

    @pl.when(kb == pl.num_programs(2) - 1)
    def _finish():
        lam = _lambda_value(lq1, lk1, lq2, lk2, lam0)
        o = acc_scr[0] * (1.0 / l_scr[0]) - lam * (acc_scr[1] * (1.0 / l_scr[1]))
        o = _subln(o, w_ref[...], lam0).astype(o_ref.dtype)
        for h in range(group):
            o_ref[:, h * V_HEAD_DIM:(h + 1) * V_HEAD_DIM] = o[h * rows:(h + 1) * rows, :]


def _attn_decode(q, k_new, v_new, cache_k, cache_v, lams, subln_w, *, batch, rows, heads, lam0):
    past = cache_k.shape[1]
    group = V7X_SUBLANES
    t_kv = min(past, 512)
    assert heads % group == 0 and past % t_kv == 0
    gw = group * V_HEAD_DIM
    lam_specs = [pl.BlockSpec((1, QK_DIM), lambda b, g, k: (0, 0))] * 4
    new_spec = pl.BlockSpec((rows, gw), lambda b, g, k: (b, g))
    cache_spec = pl.BlockSpec((None, t_kv, group, V_HEAD_DIM), lambda b, g, k: (b, k, g, 0))
    return pl.pallas_call(
        functools.partial(_attn_decode_body, rows=rows, lam0=lam0),
        grid=(batch, heads // group, past // t_kv),
        in_specs=lam_specs + [
            pl.BlockSpec((1, V_HEAD_DIM), lambda b, g, k: (0, 0)),
            new_spec, new_spec, new_spec, cache_spec, cache_spec,
        ],
        out_specs=new_spec,
        out_shape=jax.ShapeDtypeStruct(q.shape, BF16),
        scratch_shapes=[
            pltpu.VMEM((2, group * rows, 1), F32),
            pltpu.VMEM((2, group * rows, 1), F32),
            pltpu.VMEM((2, group * rows, V_HEAD_DIM), F32),
        ],
        compiler_params=_cparams(3),
        name="attn_decode",
    )(*lams, subln_w, q, k_new, v_new, cache_k, cache_v)


def _split3(x):
    hi = x.astype(BF16)
    r1 = x - hi.astype(F32)
    mid = r1.astype(BF16)
    lo = (r1 - mid.astype(F32)).astype(BF16)
    return hi, mid, lo


def _ssd_body(z_ref, x_ref, bc_ref, dt_ref, tail_ref, st0_ref, cw_ref, cb_ref, dtb_ref, alog_ref,
              dsk_ref, nw_ref, e_ref, tril_ref, y_ref, stout_ref, xpad_scr, st_scr,
              *, valid, heads_per_group):
    c = pl.program_id(1)
    L = CHUNK
    inner = z_ref.shape[1]
    gw = heads_per_group * SSD_HEAD_DIM
    n_groups = inner // gw
    gn = D_STATE
    tail_rows = V7X_SUBLANES

    @pl.when(c == 0)
    def _load_stream_state():
        xpad_scr[0:tail_rows, :] = tail_ref[...]
        st_scr[...] = st0_ref[...].T

    xpad_scr[tail_rows:tail_rows + L, 0:inner] = x_ref[...]
    xpad_scr[tail_rows:tail_rows + L, inner:] = bc_ref[...]
    conv = cb_ref[...]
    first = tail_rows - (CONV_WIDTH - 1)
    for i in range(CONV_WIDTH):
        conv = conv + cw_ref[i:i + 1, :] * xpad_scr[first + i:first + i + L, :]
    xpad_scr[0:tail_rows, :] = xpad_scr[L:L + tail_rows, :]
    act = conv * _sigmoid(conv)
    xs = act[:, :inner]
    bm = act[:, inner:inner + n_groups * gn]
    cm = act[:, inner + n_groups * gn:]

    dtv = dt_ref[...] + dtb_ref[...]
    dt = jnp.maximum(dtv, 0.0) + jnp.log1p(jnp.exp(-jnp.abs(dtv)))
    if valid < L:
        dt = jnp.where(lax.broadcasted_iota(jnp.int32, dt.shape, 0) < valid, dt, 0.0)
    d_a = dt * (-jnp.exp(alog_ref[...]))

    pieces = jnp.concatenate(_split3(d_a), axis=1)
    cs = jnp.dot(tril_ref[...], pieces, preferred_element_type=F32)
    w = d_a.shape[1]
    a_cs = (cs[:, :w] + cs[:, w:2 * w]) + cs[:, 2 * w:]

    both = jnp.concatenate([a_cs, dt], axis=0)
    pieces = jnp.concatenate(_split3(both), axis=0)
    ex = jnp.dot(pieces, e_ref[...], preferred_element_type=F32)
    ex = (ex[:2 * L] + ex[2 * L:4 * L]) + ex[4 * L:]
    acol = ex[:L]
    dtx = ex[L:]

    row = lax.broadcasted_iota(jnp.int32, (L, inner), 0)
    sub = lax.broadcasted_iota(jnp.int32, (L, inner), 1) & (SSD_HEAD_DIM - 1)
    arow = jnp.sum(jnp.where(row == sub, acol, 0.0), axis=0, keepdims=True)
    decay_in = jnp.exp(jnp.where(sub <= row, acol - arow, -jnp.inf))
    a_last = acol[L - 1:L, :]
    decay_to_end = jnp.exp(a_last - acol)
    decay_from_start = jnp.exp(acol)
    chunk_decay = jnp.exp(a_last)

    xdt = xs * dtx
    xdt_bf = xdt.astype(BF16)
    xd_end = (xdt * decay_to_end).astype(BF16)
    bm_t = jnp.concatenate([bm, jnp.zeros_like(bm)], axis=0).T
    zeros_rows = jnp.zeros((L, gw), BF16)
    lane_head = lax.broadcasted_iota(jnp.int32, (L, gw), 1) // SSD_HEAD_DIM

    for g in range(n_groups):
        hs = slice(g * gw, (g + 1) * gw)
        ns = slice(g * gn, (g + 1) * gn)
        b_bf = bm[:, ns].astype(BF16)
        c_bf = cm[:, ns].astype(BF16)
        cb = lax.dot_general(c_bf, jnp.concatenate([b_bf] * heads_per_group, axis=0), NT_DIMS,
                             preferred_element_type=F32)
        lhs = (cb * decay_in[:, hs]).astype(BF16)
        xg = xdt_bf[:, hs]
        zero = jnp.zeros_like(xg)
        block_diag = jnp.concatenate(
            [jnp.where(lane_head == r, xg, zero) for r in range(heads_per_group)], axis=0)
        y = jnp.dot(lhs, block_diag, preferred_element_type=F32)
        st = st_scr[:, hs]
        y = y + jnp.dot(c_bf, st.astype(BF16), preferred_element_type=F32) * decay_from_start[:, hs]
        bt = bm_t[ns, :].astype(BF16)
        st_scr[:, hs] = chunk_decay[:, hs] * st + jnp.dot(
            bt, jnp.concatenate([xd_end[:, hs], zeros_rows], axis=0), preferred_element_type=F32)
        y = y + dsk_ref[:, hs] * xs[:, hs]
        zg = z_ref[:, hs]
        y = y * (zg * _sigmoid(zg))
        y = y * lax.rsqrt(jnp.mean(y * y, axis=-1, keepdims=True) + EPS)
        y_ref[:, hs] = (y * nw_ref[:, hs]).astype(y_ref.dtype)

    @pl.when(c == pl.num_programs(1) - 1)
    def _store_stream_state():
        stout_ref[...] = st_scr[...].T


def _ssd(zx, dt_raw, tail, st0, p, *, batch, n_chunks, valid):
    rows = zx.shape[0]
    inner = zx.shape[1] // 3
    conv_dim = 2 * inner
    heads = inner // SSD_HEAD_DIM
    L = CHUNK
    assert SSD_HEAD_DIM == CHUNK and conv_dim == inner + 2 * SSD_GROUPS * D_STATE
    expand = np.zeros((V7X_LANES, inner), np.float32)
    for h in range(heads):
        expand[h, h * SSD_HEAD_DIM:(h + 1) * SSD_HEAD_DIM] = 1.0
    tril = np.tril(np.ones((L, L), np.float32))

    def row_blk(col):
        return pl.BlockSpec((L, inner), lambda b, c, col=col: (b * n_chunks + c, col))

    def const(shape):
        return pl.BlockSpec(shape, lambda b, c: (0,) * len(shape))

    return pl.pallas_call(
        functools.partial(_ssd_body, valid=valid, heads_per_group=heads // SSD_GROUPS),
        grid=(batch, n_chunks),
        in_specs=[
            row_blk(0), row_blk(1), row_blk(2),
            pl.BlockSpec((L, V7X_LANES), lambda b, c: (b * n_chunks + c, 0)),
            pl.BlockSpec((None, V7X_SUBLANES, conv_dim), lambda b, c: (b, 0, 0)),
            pl.BlockSpec((None, inner, D_STATE), lambda b, c: (b, 0, 0)),
            const((CONV_WIDTH, conv_dim)), const((1, conv_dim)),
            const((1, V7X_LANES)), const((1, V7X_LANES)),
            const((1, inner)), const((1, inner)),
            const((V7X_LANES, inner)), const((L, L)),
        ],
        out_specs=[
            pl.BlockSpec((L, inner), lambda b, c: (b * n_chunks + c, 0)),
            pl.BlockSpec((None, inner, D_STATE), lambda b, c: (b, 0, 0)),
        ],
        out_shape=[
            jax.ShapeDtypeStruct((rows, inner), BF16),
            jax.ShapeDtypeStruct((batch, inner, D_STATE), F32),
        ],
        scratch_shapes=[
            pltpu.VMEM((L + V7X_SUBLANES, conv_dim), F32),
            pltpu.VMEM((D_STATE, inner), F32),
        ],
        compiler_params=_cparams(2),
        name="ssd",
    )(zx, zx, zx, dt_raw, tail, st0, p["conv_w"], p["conv_b"], p["dt_bias"], p["a_log"],
      p["d_skip"], p["ssd_norm_w"], jnp.asarray(expand, BF16), jnp.asarray(tril, BF16))


def _pad_lanes(v, width):
    return jnp.pad(v.astype(F32), (0, width - v.shape[0])).reshape(1, width)


def _prepare_layer(layer, norm1_w, w_in, lambda_q1, lambda_k1, lambda_q2, lambda_k2, subln_w,
                   conv_w, conv_b, dt_bias, A_log, D_skip, ssd_norm_w, w_out, norm2_w,
                   w_gate, w_up, w_down, ff_tile):
    d_model = w_in.shape[1]
    attn_w = 3 * (w_out.shape[1] // 2)
    inner = w_out.shape[1] // 2
    main_cols = attn_w + inner + 2 * inner
    cast_rows = 128
    w_in_t = jnp.swapaxes(w_in[layer], 0, 1)
    w_in_bf = _cast_rows(w_in_t, main_cols, 2 * cast_rows)
    n_heads = w_in.shape[2] - main_cols
    w_dt = jnp.pad(w_in_t[main_cols:, :].astype(BF16), ((0, V7X_LANES - n_heads), (0, 0)))
    d_ff = w_gate.shape[2]
    ff_width = d_ff + (-d_ff % ff_tile)
    return {
        "lam0": _lambda_init(layer),
        "norm1_w": norm1_w[layer], "norm2_w": norm2_w[layer],
        "w_in": w_in_bf, "w_dt": w_dt, "main_cols": main_cols,
        "lams": [v[layer].reshape(1, -1).astype(F32)
                 for v in (lambda_q1, lambda_k1, lambda_q2, lambda_k2)],
        "subln_w": subln_w[layer].reshape(1, -1).astype(F32),
        "conv_w": conv_w[layer].astype(F32), "conv_b": conv_b[layer].reshape(1, -1).astype(F32),
        "dt_bias": _pad_lanes(dt_bias[layer], V7X_LANES),
        "a_log": _pad_lanes(A_log[layer], V7X_LANES),
        "d_skip": jnp.repeat(D_skip[layer].astype(F32), SSD_HEAD_DIM).reshape(1, -1),
        "ssd_norm_w": ssd_norm_w[layer].reshape(1, -1).astype(F32),
        "w_out": _cast_cols(w_out[layer], d_model, d_model, cast_rows),
        "w_gate": _cast_cols(w_gate[layer], d_ff, ff_width, cast_rows),
        "w_up": _cast_cols(w_up[layer], d_ff, ff_width, cast_rows),
        "w_down": _cast_rows(w_down[layer], ff_width, 2 * cast_rows),
        "d_model": d_model, "inner": inner,
    }


def _tiles(m):
    return 1024 if m % 1024 == 0 else m


def _in_proj(x2d, p):
    m = x2d.shape[0]
    inner = p["inner"]
    tm = _tiles(m)
    h = _rmsnorm(x2d, p["norm1_w"], BF16, min(256, m))
    proj = functools.partial(_matmul, [h], tm=tm, b_transposed=True)
    q, = proj([p["w_in"]], n=inner, tn=1024, b_col_off=0,
              outs=[(BF16, ATTN_SCALE * LOG2_E)], name="proj_q")
    k32, kbf = proj([p["w_in"]], n=inner, tn=1024, b_col_off=inner,
                    outs=[(F32, 1.0), (BF16, 1.0)], name="proj_k")
    v32, vbf = proj([p["w_in"]], n=inner, tn=1024, b_col_off=2 * inner,
                    outs=[(F32, 1.0), (BF16, 1.0)], name="proj_v")
    zx, = proj([p["w_in"]], n=3 * inner, tn=1024, b_col_off=3 * inner,
               outs=[(F32, 1.0)], name="proj_zx")
    dt, = proj([p["w_dt"]], n=V7X_LANES, tn=V7X_LANES, outs=[(F32, 1.0)], name="proj_dt")
    return q, k32, kbf, v32, vbf, zx, dt


def _out_and_ffn(x2d, attn, ssd_y, p):
    m = x2d.shape[0]
    tm = _tiles(m)
    x1, = _matmul([attn, ssd_y], [p["w_out"], p["w_out"]], b_row_blocks=[0, 1], n=p["d_model"],
                  tm=tm, tn=min(1024, p["d_model"]), outs=[(F32, 1.0)], residual=x2d, name="out_proj")
    h2 = _rmsnorm(x1, p["norm2_w"], BF16, min(256, m))
    act = _ffn_up(h2, p["w_gate"], p["w_up"], tm=tm, tn=512 if tm == 1024 else 1024)
    return _matmul_acc(act, p["w_down"], x1, tm=tm, tn=min(1024, p["d_model"]), tk=act.shape[1] // 4)


def kernel(x_prompt, x_sample, cache_k, cache_v, state_conv, state_ssm, norm1_w, w_in, lambda_q1, lambda_k1, lambda_q2, lambda_k2, subln_w, conv_w, conv_b, dt_bias, A_log, D_skip, ssd_norm_w, w_out, norm2_w, w_gate, w_up, w_down, final_norm_w):
    depth = w_in.shape[0]
    bp, seq, d_model = x_prompt.shape
    bs, rows, _ = x_sample.shape
    heads = cache_k.shape[3]
    inner = w_out.shape[1] // 2
    attn_w = heads * V_HEAD_DIM
    conv_dim = conv_w.shape[2]
    ssd_heads = state_ssm.shape[2]
    assert seq % CHUNK == 0 and rows <= CHUNK and rows >= CONV_WIDTH - 1 and attn_w == inner

    xp = x_prompt.reshape(bp * seq, d_model)
    xs = x_sample.reshape(bs * rows, d_model)
    outs_p = [[], [], [], []]
    outs_s = [[], [], [], []]
    for layer in range(depth):
        p = _prepare_layer(layer, norm1_w, w_in, lambda_q1, lambda_k1, lambda_q2, lambda_k2, subln_w,
                           conv_w, conv_b, dt_bias, A_log, D_skip, ssd_norm_w, w_out, norm2_w,
                           w_gate, w_up, w_down, ff_tile=1024)
        lam0 = p["lam0"]

        q, k32, kbf, v32, vbf, zx, dt = _in_proj(xp, p)
        attn = _attn_prompt(q, kbf, vbf, p["lams"], p["subln_w"], batch=bp, seq=seq, heads=heads,
                            tk=512, lam0=lam0)
        ssd_y, st = _ssd(zx, dt, jnp.zeros((bp, V7X_SUBLANES, conv_dim), F32),
                         jnp.zeros((bp, inner, D_STATE), F32), p,
                         batch=bp, n_chunks=seq // CHUNK, valid=CHUNK)
        outs_p[0].append(k32.reshape(bp, seq, heads, V_HEAD_DIM))
        outs_p[1].append(v32.reshape(bp, seq, heads, V_HEAD_DIM))
        outs_p[2].append(zx.reshape(bp, seq, 3 * inner)[:, seq - (CONV_WIDTH - 1):, inner:])
        outs_p[3].append(st.reshape(bp, ssd_heads, SSD_HEAD_DIM, D_STATE))
        xp = _out_and_ffn(xp, attn, ssd_y, p)

        q, k32, kbf, v32, vbf, zx, dt = _in_proj(xs, p)
        attn = _attn_decode(q, kbf, vbf, cache_k[layer], cache_v[layer],
                            p["lams"], p["subln_w"], batch=bs, rows=rows, heads=heads, lam0=lam0)
        pad_chunk = ((0, 0), (0, CHUNK - rows), (0, 0))
        zx_c = jnp.pad(zx.reshape(bs, rows, 3 * inner), pad_chunk).reshape(bs * CHUNK, 3 * inner)
        dt_c = jnp.pad(dt.reshape(bs, rows, V7X_LANES), pad_chunk).reshape(bs * CHUNK, V7X_LANES)
        tail = jnp.pad(state_conv[layer].astype(F32),
                       ((0, 0), (V7X_SUBLANES - (CONV_WIDTH - 1), 0), (0, 0)))
        ssd_y, st = _ssd(zx_c, dt_c, tail, state_ssm[layer].astype(F32).reshape(bs, inner, D_STATE), p,
                         batch=bs, n_chunks=1, valid=rows)
        ssd_y = ssd_y.reshape(bs, CHUNK, inner)[:, :rows].reshape(bs * rows, inner)
        outs_s[0].append(k32.reshape(bs, rows, heads, V_HEAD_DIM))
        outs_s[1].append(v32.reshape(bs, rows, heads, V_HEAD_DIM))
        outs_s[2].append(zx.reshape(bs, rows, 3 * inner)[:, rows - (CONV_WIDTH - 1):, inner:])
        outs_s[3].append(st.reshape(bs, ssd_heads, SSD_HEAD_DIM, D_STATE))
        xs = _out_and_ffn(xs, attn, ssd_y, p)

    y_prompt = _rmsnorm(xp, final_norm_w, F32, 256).reshape(bp, seq, d_model)
    y_sample = _rmsnorm(xs, final_norm_w, F32, bs * rows).reshape(bs, rows, d_model)
    return (y_prompt, y_sample, *(jnp.stack(o) for o in outs_p), *(jnp.stack(o) for o in outs_s))
```

```python
import functools
import math

import jax
import jax.numpy as jnp
import numpy as np
from jax import lax
from jax.experimental import pallas as pl
from jax.experimental.pallas import tpu as pltpu

F32 = jnp.float32
BF16 = jnp.bfloat16

CHUNK = 64
V_HEAD_DIM = 128
QK_DIM = V_HEAD_DIM // 2
ATTN_SCALE = QK_DIM ** -0.5
LOG2_E = math.log2(math.e)
SSD_HEAD_DIM = 64
SSD_GROUPS = 8
D_STATE = 128
CONV_WIDTH = 4
EPS = 1e-6

V7X_LANES = 128
V7X_SUBLANES = 8
BF16_SUBLANES = 2 * V7X_SUBLANES
V7X_VMEM_BYTES = 64 * 1024 * 1024
VMEM_LIMIT_BYTES = V7X_VMEM_BYTES - 8 * 1024 * 1024

NT_DIMS = (((1,), (1,)), ((), ()))


def _cparams(n_axes, flags=None):
    return pltpu.CompilerParams(
        dimension_semantics=("arbitrary",) * n_axes, vmem_limit_bytes=VMEM_LIMIT_BYTES, flags=flags)


def _sigmoid(x):
    return 1.0 / (1.0 + jnp.exp(-x))


def _lambda_init(layer):
    return 0.8 - 0.6 * math.exp(-0.3 * layer)


def _rmsnorm_body(x_ref, w_ref, o_ref):
    x = x_ref[...]
    ms = jnp.mean(x * x, axis=-1, keepdims=True)
    o_ref[...] = (x * lax.rsqrt(ms + EPS) * w_ref[...]).astype(o_ref.dtype)


def _rmsnorm(x, w, out_dtype, tr):
    m, d = x.shape
    return pl.pallas_call(
        _rmsnorm_body,
        grid=(m // tr,),
        in_specs=[pl.BlockSpec((tr, d), lambda i: (i, 0)),
                  pl.BlockSpec((1, d), lambda i: (0, 0))],
        out_specs=pl.BlockSpec((tr, d), lambda i: (i, 0)),
        out_shape=jax.ShapeDtypeStruct((m, d), out_dtype),
        compiler_params=_cparams(1),
        name="rmsnorm",
    )(x, w.reshape(1, d).astype(F32))


def _cast_cols_body(x_ref, o_ref):
    n_in = x_ref.shape[1]
    o_ref[:, :n_in] = x_ref[...].astype(o_ref.dtype)
    if o_ref.shape[1] > n_in:
        o_ref[:, n_in:] = jnp.zeros((o_ref.shape[0], o_ref.shape[1] - n_in), o_ref.dtype)


def _cast_cols(w, n_in, n_out, tr):
    rows = w.shape[0]
    assert rows % tr == 0 and n_in % V7X_LANES == 0 and n_out >= n_in
    return pl.pallas_call(
        _cast_cols_body,
        grid=(rows // tr,),
        in_specs=[pl.BlockSpec((tr, n_in), lambda i: (i, 0))],
        out_specs=pl.BlockSpec((tr, n_out), lambda i: (i, 0)),
        out_shape=jax.ShapeDtypeStruct((rows, n_out), BF16),
        compiler_params=_cparams(1),
        name="cast_cols",
    )(w)


def _cast_rows_body(x_ref, o_ref, *, n_blocks_in):
    keep = pl.program_id(0) < n_blocks_in
    o_ref[...] = jnp.where(keep, x_ref[...], 0.0).astype(o_ref.dtype)


def _cast_rows(w, rows_out, tr):
    rows, n = w.shape
    assert rows_out % tr == 0 and (rows_out <= rows or rows % tr == 0)
    n_blocks_in = rows // tr
    return pl.pallas_call(
        functools.partial(_cast_rows_body, n_blocks_in=n_blocks_in),
        grid=(rows_out // tr,),
        in_specs=[pl.BlockSpec((tr, n), lambda i: (jnp.minimum(i, n_blocks_in - 1), 0))],
        out_specs=pl.BlockSpec((tr, n), lambda i: (i, 0)),
        out_shape=jax.ShapeDtypeStruct((rows_out, n), BF16),
        compiler_params=_cparams(1),
        name="cast_rows",
    )(w)


def _mm_body(*refs, n_pairs, scales, has_res, b_transposed):
    a_refs = refs[:n_pairs]
    b_refs = refs[n_pairs:2 * n_pairs]
    pos = 2 * n_pairs
    res_ref = refs[pos] if has_res else None
    out_refs = refs[pos + int(has_res):]
    acc = None
    for a_ref, b_ref in zip(a_refs, b_refs):
        if b_transposed:
            d = lax.dot_general(a_ref[...], b_ref[...], NT_DIMS, preferred_element_type=F32)
        else:
            d = jnp.dot(a_ref[...], b_ref[...], preferred_element_type=F32)
        acc = d if acc is None else acc + d
    if has_res:
        acc = acc + res_ref[...]
    for o_ref, s in zip(out_refs, scales):
        o_ref[...] = (acc if s == 1.0 else acc * s).astype(o_ref.dtype)


def _matmul(a_list, b_list, *, n, tm, tn, outs, b_row_blocks=None, b_col_off=0,
            residual=None, b_transposed=False, name="matmul"):
    m = a_list[0].shape[0]
    n_pairs = len(a_list)
    if b_row_blocks is None:
        b_row_blocks = [0] * n_pairs
    assert b_col_off % tn == 0 and n % tn == 0 and m % tm == 0
    col_blk = b_col_off // tn
    in_specs = []
    for a in a_list:
        in_specs.append(pl.BlockSpec((tm, a.shape[1]), lambda i, j: (i, 0)))
    for a, rb in zip(a_list, b_row_blocks):
        if b_transposed:
            in_specs.append(pl.BlockSpec((tn, a.shape[1]), lambda i, j, rb=rb: (col_blk + j, rb)))
        else:
            in_specs.append(pl.BlockSpec((a.shape[1], tn), lambda i, j, rb=rb: (rb, col_blk + j)))
    args = list(a_list) + list(b_list)
    if residual is not None:
        in_specs.append(pl.BlockSpec((tm, tn), lambda i, j: (i, j)))
        args.append(residual)
    out_shape = [jax.ShapeDtypeStruct((m, n), dt) for dt, _ in outs]
    out_specs = [pl.BlockSpec((tm, tn), lambda i, j: (i, j)) for _ in outs]
    res = pl.pallas_call(
        functools.partial(_mm_body, n_pairs=n_pairs, scales=tuple(s for _, s in outs),
                          has_res=residual is not None, b_transposed=b_transposed),
        grid=(m // tm, n // tn),
        in_specs=in_specs,
        out_specs=out_specs,
        out_shape=out_shape,
        compiler_params=_cparams(2),
        name=name,
    )(*args)
    return res


def _ffn_up_body(h_ref, wg_ref, wu_ref, o_ref):
    h = h_ref[...]
    g = jnp.dot(h, wg_ref[...], preferred_element_type=F32)
    u = jnp.dot(h, wu_ref[...], preferred_element_type=F32)
    o_ref[...] = (g * _sigmoid(g) * u).astype(o_ref.dtype)


def _ffn_up(h, wg, wu, *, tm, tn):
    m, d = h.shape
    n = wg.shape[1]
    return pl.pallas_call(
        _ffn_up_body,
        grid=(m // tm, n // tn),
        in_specs=[pl.BlockSpec((tm, d), lambda i, j: (i, 0)),
                  pl.BlockSpec((d, tn), lambda i, j: (0, j)),
                  pl.BlockSpec((d, tn), lambda i, j: (0, j))],
        out_specs=pl.BlockSpec((tm, tn), lambda i, j: (i, j)),
        out_shape=jax.ShapeDtypeStruct((m, n), BF16),
        compiler_params=_cparams(2),
        name="ffn_up",
    )(h, wg, wu)


def _mm_acc_body(a_ref, b_ref, r_ref, o_ref):
    @pl.when(pl.program_id(2) == 0)
    def _start_from_residual():
        o_ref[...] = r_ref[...]

    o_ref[...] += jnp.dot(a_ref[...], b_ref[...], preferred_element_type=F32)


def _matmul_acc(a, b, residual, *, tm, tn, tk):
    m, kdim = a.shape
    n = b.shape[1]
    assert m % tm == 0 and n % tn == 0 and kdim % tk == 0
    return pl.pallas_call(
        _mm_acc_body,
        grid=(m // tm, n // tn, kdim // tk),
        in_specs=[pl.BlockSpec((tm, tk), lambda i, j, k: (i, k)),
                  pl.BlockSpec((tk, tn), lambda i, j, k: (k, j)),
                  pl.BlockSpec((tm, tn), lambda i, j, k: (i, j))],
        out_specs=pl.BlockSpec((tm, tn), lambda i, j, k: (i, j)),
        out_shape=jax.ShapeDtypeStruct((m, n), F32),
        compiler_params=_cparams(3),
        name="ffn_down",
    )(a, b, residual)


def _lambda_value(lq1, lk1, lq2, lk2, lam0):
    d1 = jnp.sum(lq1[...] * lk1[...], axis=-1, keepdims=True)
    d2 = jnp.sum(lq2[...] * lk2[...], axis=-1, keepdims=True)
    return jnp.exp(d1) - jnp.exp(d2) + lam0


def _subln(o, w, lam0):
    ms = jnp.mean(o * o, axis=-1, keepdims=True)
    return (o * lax.rsqrt(ms + EPS) * w) * (1.0 - lam0)


def _side_cast(in_ref, out_ref, step, n_in_blocks, n_out_blocks):
    x = in_ref[...]
    if n_out_blocks > n_in_blocks:
        x = jnp.where(jnp.minimum(step, n_out_blocks - 1) < n_in_blocks, x, 0.0)
    n_in = in_ref.shape[1]
    out_ref[:, :n_in] = x.astype(out_ref.dtype)
    if out_ref.shape[1] > n_in:
        out_ref[:, n_in:] = jnp.zeros((out_ref.shape[0], out_ref.shape[1] - n_in), out_ref.dtype)


def _attn_prompt_body(*refs, side_blocks, tq, tk, lam0):
    n_side = len(side_blocks)
    lq1, lk1, lq2, lk2, w_ref, q_ref, k_ref, v_ref = refs[:8]
    side_in = refs[8:8 + n_side]
    o_ref = refs[8 + n_side]
    side_out = refs[9 + n_side:9 + 2 * n_side]
    (k1_scr, k2_scr, vt_scr, sa, sb, pa, pb, cma, cmb, ala, alb, m_scr,
     acc_scr) = refs[9 + 2 * n_side:]
    qi = pl.program_id(2)
    step = (pl.program_id(0) * pl.num_programs(1) + pl.program_id(1)) * pl.num_programs(2) + qi
    n_kv = k1_scr.shape[0]
    k_scrs = (k1_scr, k2_scr)

    @pl.when(qi == 0)
    def _prepare_head():
        first_half = lax.broadcasted_iota(jnp.int32, (tk, V_HEAD_DIM), 1) < QK_DIM
        for c in range(n_kv):
            kb = k_ref[pl.ds(c * tk, tk), :]
            zero = jnp.zeros_like(kb)
            k1_scr[c] = jnp.where(first_half, kb, zero)
            k2_scr[c] = jnp.where(first_half, zero, kb)
            vt_scr[c, :V_HEAD_DIM, :] = v_ref[pl.ds(c * tk, tk), :].T
            ones_row = lax.broadcasted_iota(jnp.int32, (BF16_SUBLANES, tk), 0) == 0
            vt_scr[c, V_HEAD_DIM:, :] = jnp.where(ones_row, 1.0, 0.0).astype(BF16)

    m_scr[...] = jnp.full(m_scr.shape, -jnp.inf, F32)
    acc_scr[...] = jnp.zeros(acc_scr.shape, F32)
    pb[...] = jnp.zeros(pb.shape, BF16)
    alb[...] = jnp.ones(alb.shape, F32)
    for in_ref, out_ref, (n_in_blocks, n_out_blocks) in zip(side_in, side_out, side_blocks):
        _side_cast(in_ref, out_ref, step, n_in_blocks, n_out_blocks)

    def scores(j, s_scr, cm_scr, c0):
        qv = q_ref[c0:, :]
        for idx in range(2):
            s = lax.dot_general(k_scrs[idx][j], qv, NT_DIMS, preferred_element_type=F32)
            s_scr[idx, :, c0:] = s
            cm_scr[idx, :, c0:] = jnp.max(s, axis=0, keepdims=True)

    def softmax(s_scr, cm_scr, p_scr, al_scr, c0, mask):
        for idx in range(2):
            s = s_scr[idx, :, c0:]
            if mask is None:
                cmax = cm_scr[idx, :, c0:]
            else:
                s = jnp.where(mask, s, -jnp.inf)
                cmax = jnp.max(s, axis=0, keepdims=True)
            m_old = m_scr[idx, :, c0:]
            m_new = jnp.maximum(m_old, cmax)
            alpha = jnp.exp2(m_old - m_new)
            p = jnp.exp2(s - m_new)
            p_scr[idx, :, c0:] = p.astype(BF16)
            al_scr[idx, :, c0:] = alpha
            m_scr[idx, :, c0:] = m_new

    def value_product(j, p_scr, al_scr, c0):
        vt = vt_scr[j]
        for idx in range(2):
            acc_scr[idx, :, c0:] = al_scr[idx, :, c0:] * acc_scr[idx, :, c0:] + jnp.dot(
                vt, p_scr[idx, :, c0:], preferred_element_type=F32)

    scores(0, sa, cma, 0)

    def pair(jj, carry):
        j = 2 * jj
        scores(j + 1, sb, cmb, 0)
        value_product(jnp.maximum(j - 1, 0), pb, alb, 0)
        softmax(sa, cma, pa, ala, 0, None)
        scores(j + 2, sa, cma, 0)
        value_product(j, pa, ala, 0)
        softmax(sb, cmb, pb, alb, 0, None)
        return carry

    lax.fori_loop(0, qi, pair, 0)

    j0 = 2 * qi
    key_chunk = lax.broadcasted_iota(jnp.int32, (tk, tq), 0) // CHUNK
    qry_chunk = lax.broadcasted_iota(jnp.int32, (tk, tq), 1) // CHUNK
    visible = key_chunk <= qry_chunk
    scores(j0 + 1, sb, cmb, tk)
    value_product(jnp.maximum(j0 - 1, 0), pb, alb, 0)
    softmax(sa, cma, pa, ala, 0, visible)
    softmax(sb, cmb, pb, alb, tk, visible[:, :tk])
    value_product(j0, pa, ala, 0)
    value_product(j0 + 1, pb, alb, tk)

    lam = _lambda_value(lq1, lk1, lq2, lk2, lam0)
    outs = [acc_scr[idx, :V_HEAD_DIM, :] * (1.0 / acc_scr[idx, V_HEAD_DIM:V_HEAD_DIM + 1, :])
            for idx in range(2)]
    o_t = outs[0] - lam * outs[1]
    o_ref[...] = _subln(o_t.T, w_ref[...], lam0).astype(o_ref.dtype)


def _side_cast_rows(rows_in, rows_out, n_steps):
    br = BF16_SUBLANES
    while rows_out > br * n_steps:
        br *= 2
    return br if rows_in % br == 0 and rows_out % br == 0 else None


def _attn_prompt(q, k, v, lams, subln_w, side, *, batch, seq, heads, tk, lam0):
    tq = 2 * tk
    nq = seq // tq
    n_kv = seq // tk
    n_steps = batch * heads * nq
    lam_specs = [pl.BlockSpec((1, QK_DIM), lambda b, h, i: (0, 0))] * 4
    side_in_specs, side_out_specs, side_shapes, side_blocks = [], [], [], []
    for w, rows_out, cols_out in side:
        br = _side_cast_rows(w.shape[0], rows_out, n_steps)
        assert br is not None and w.shape[1] % V7X_LANES == 0
        n_in_blocks, n_out_blocks = w.shape[0] // br, rows_out // br

        def block_of(b, h, i, last):
            return (jnp.minimum((b * heads + h) * nq + i, last), 0)

        side_in_specs.append(pl.BlockSpec(
            (br, w.shape[1]), functools.partial(block_of, last=n_in_blocks - 1)))
        side_out_specs.append(pl.BlockSpec(
            (br, cols_out), functools.partial(block_of, last=n_out_blocks - 1)))
        side_shapes.append(jax.ShapeDtypeStruct((rows_out, cols_out), BF16))
        side_blocks.append((n_in_blocks, n_out_blocks))
    out = pl.pallas_call(
        functools.partial(_attn_prompt_body, side_blocks=tuple(side_blocks), tq=tq, tk=tk, lam0=lam0),
        grid=(batch, heads, nq),
        in_specs=lam_specs + [
            pl.BlockSpec((1, V_HEAD_DIM), lambda b, h, i: (0, 0)),
            pl.BlockSpec((tq, V_HEAD_DIM), lambda b, h, i: (b * nq + i, h)),
            pl.BlockSpec((seq, V_HEAD_DIM), lambda b, h, i: (b, h)),
            pl.BlockSpec((seq, V_HEAD_DIM), lambda b, h, i: (b, h)),
        ] + side_in_specs,
        out_specs=[pl.BlockSpec((tq, V_HEAD_DIM), lambda b, h, i: (b * nq + i, h))] + side_out_specs,
        out_shape=[jax.ShapeDtypeStruct(q.shape, BF16)] + side_shapes,
        scratch_shapes=[
            pltpu.VMEM((n_kv, tk, V_HEAD_DIM), BF16),
            pltpu.VMEM((n_kv, tk, V_HEAD_DIM), BF16),
            pltpu.VMEM((n_kv, V_HEAD_DIM + BF16_SUBLANES, tk), BF16),
            pltpu.VMEM((2, tk, tq), F32), pltpu.VMEM((2, tk, tq), F32),
            pltpu.VMEM((2, tk, tq), BF16), pltpu.VMEM((2, tk, tq), BF16),
            pltpu.VMEM((2, 1, tq), F32), pltpu.VMEM((2, 1, tq), F32),
            pltpu.VMEM((2, 1, tq), F32), pltpu.VMEM((2, 1, tq), F32),
            pltpu.VMEM((2, 1, tq), F32),
            pltpu.VMEM((2, V_HEAD_DIM + BF16_SUBLANES, tq), F32),
        ],
        compiler_params=_cparams(3),
        name="attn_prompt",
    )(*lams, subln_w, q, k, v, *(w for w, _, _ in side))
    return out[0], out[1:]


def _attn_decode_body(lq1, lk1, lq2, lk2, w_ref, q_ref, kn_ref, vn_ref, kc_ref, vc_ref, o_ref,
                      m_scr, l_scr, acc_scr, *, rows, lam0):
    kb = pl.program_id(2)
    group = kc_ref.shape[1]
    n_q = group * rows

    def stack_heads(ref):
        x = ref[...]
        return jnp.concatenate(
            [x[:, h * V_HEAD_DIM:(h + 1) * V_HEAD_DIM] for h in range(group)], axis=0)

    q = stack_heads(q_ref)
    first_half = lax.broadcasted_iota(jnp.int32, q.shape, 1) < QK_DIM
    zero = jnp.zeros_like(q)
    q_streams = (jnp.where(first_half, q, zero), jnp.where(first_half, zero, q))

    def accumulate(k2d, v2d, same_head):
        for idx in range(2):
            s = lax.dot_general(q_streams[idx], k2d, NT_DIMS, preferred_element_type=F32)
            s = jnp.where(same_head, s, -jnp.inf)
            m_old = m_scr[idx]
            m_new = jnp.maximum(m_old, jnp.max(s, axis=-1, keepdims=True))
            alpha = jnp.exp2(m_old - m_new)
            p = jnp.exp2(s - m_new)
            l_scr[idx] = alpha * l_scr[idx] + jnp.sum(p, axis=-1, keepdims=True)
            acc_scr[idx] = alpha * acc_scr[idx] + jnp.dot(
                p.astype(BF16), v2d, preferred_element_type=F32)
            m_scr[idx] = m_new

    @pl.when(kb == 0)
    def _new_rows():
        m_scr[...] = jnp.full(m_scr.shape, -jnp.inf, F32)
        l_scr[...] = jnp.zeros(l_scr.shape, F32)
        acc_scr[...] = jnp.zeros(acc_scr.shape, F32)
        q_head = lax.broadcasted_iota(jnp.int32, (n_q, n_q), 0) // rows
        k_head = lax.broadcasted_iota(jnp.int32, (n_q, n_q), 1) // rows
        accumulate(stack_heads(kn_ref), stack_heads(vn_ref), q_head == k_head)

    n_keys = kc_ref.shape[0] * group
    q_head = lax.broadcasted_iota(jnp.int32, (n_q, n_keys), 0) // rows
    k_head = lax.broadcasted_iota(jnp.int32, (n_q, n_keys), 1) % group
    accumulate(kc_ref[...].reshape(n_keys, V_HEAD_DIM).astype(BF16),
               vc_ref[...].reshape(n_keys, V_HEAD_DIM).astype(BF16), q_head == k_head)

    @pl.when(kb == pl.num_programs(2) - 1)
    def _finish():
        lam = _lambda_value(lq1, lk1, lq2, lk2, lam0)
        o = acc_scr[0] * (1.0 / l_scr[0]) - lam * (acc_scr[1] * (1.0 / l_scr[1]))
        o = _subln(o, w_ref[...], lam0).astype(o_ref.dtype)
        for h in range(group):
            o_ref[:, h * V_HEAD_DIM:(h + 1) * V_HEAD_DIM] = o[h * rows:(h + 1) * rows, :]


def _attn_decode(q, k_new, v_new, cache_k, cache_v, lams, subln_w, *, batch, rows, heads, lam0):
    past = cache_k.shape[1]
    group = V7X_SUBLANES
    t_kv = min(past, 512)
    assert heads % group == 0 and past % t_kv == 0
    gw = group * V_HEAD_DIM
    lam_specs = [pl.BlockSpec((1, QK_DIM), lambda b, g, k: (0, 0))] * 4
    new_spec = pl.BlockSpec((rows, gw), lambda b, g, k: (b, g))
    cache_spec = pl.BlockSpec((None, t_kv, group, V_HEAD_DIM), lambda b, g, k: (b, k, g, 0))
    return pl.pallas_call(
        functools.partial(_attn_decode_body, rows=rows, lam0=lam0),
        grid=(batch, heads // group, past // t_kv),
        in_specs=lam_specs + [
            pl.BlockSpec((1, V_HEAD_DIM), lambda b, g, k: (0, 0)),
            new_spec, new_spec, new_spec, cache_spec, cache_spec,
        ],
        out_specs=new_spec,
        out_shape=jax.ShapeDtypeStruct(q.shape, BF16),
        scratch_shapes=[
            pltpu.VMEM((2, group * rows, 1), F32),
            pltpu.VMEM((2, group * rows, 1), F32),
            pltpu.VMEM((2, group * rows, V_HEAD_DIM), F32),
        ],
        compiler_params=_cparams(3),
        name="attn_decode",
    )(*lams, subln_w, q, k_new, v_new, cache_k, cache_v)


def _split3(x):
    hi = x.astype(BF16)
    r1 = x - hi.astype(F32)
    mid = r1.astype(BF16)
    lo = (r1 - mid.astype(F32)).astype(BF16)
    return hi, mid, lo


def _ssd_body(z_ref, x_ref, bc_ref, dt_ref, tail_ref, st0_ref, cw_ref, cb_ref, dtb_ref, alog_ref,
              dsk_ref, nw_ref, e_ref, tril_ref, y_ref, stout_ref, xpad_scr, st_scr,
              *, valid, heads_per_group):
    c = pl.program_id(1)
    L = CHUNK
    inner = z_ref.shape[1]
    gw = heads_per_group * SSD_HEAD_DIM
    n_groups = inner // gw
    gn = D_STATE
    tail_rows = V7X_SUBLANES

    @pl.when(c == 0)
    def _load_stream_state():
        xpad_scr[0:tail_rows, :] = tail_ref[...]
        st_scr[...] = st0_ref[...].T

    xpad_scr[tail_rows:tail_rows + L, 0:inner] = x_ref[...]
    xpad_scr[tail_rows:tail_rows + L, inner:] = bc_ref[...]
    conv = cb_ref[...]
    first = tail_rows - (CONV_WIDTH - 1)
    for i in range(CONV_WIDTH):
        conv = conv + cw_ref[i:i + 1, :] * xpad_scr[first + i:first + i + L, :]
    xpad_scr[0:tail_rows, :] = xpad_scr[L:L + tail_rows, :]
    act = conv * _sigmoid(conv)
    xs = act[:, :inner]
    bm = act[:, inner:inner + n_groups * gn]
    cm = act[:, inner + n_groups * gn:]

    dtv = dt_ref[...] + dtb_ref[...]
    dt = jnp.maximum(dtv, 0.0) + jnp.log1p(jnp.exp(-jnp.abs(dtv)))
    if valid < L:
        dt = jnp.where(lax.broadcasted_iota(jnp.int32, dt.shape, 0) < valid, dt, 0.0)
    d_a = dt * (-jnp.exp(alog_ref[...]))

    pieces = jnp.concatenate(_split3(d_a), axis=1)
    cs = jnp.dot(tril_ref[...], pieces, preferred_element_type=F32)
    w = d_a.shape[1]
    a_cs = (cs[:, :w] + cs[:, w:2 * w]) + cs[:, 2 * w:]

    both = jnp.concatenate([a_cs, dt], axis=0)
    pieces = jnp.concatenate(_split3(both), axis=0)
    ex = jnp.dot(pieces, e_ref[...], preferred_element_type=F32)
    ex = (ex[:2 * L] + ex[2 * L:4 * L]) + ex[4 * L:]
    acol = ex[:L]
    dtx = ex[L:]

    row = lax.broadcasted_iota(jnp.int32, (L, inner), 0)
    sub = lax.broadcasted_iota(jnp.int32, (L, inner), 1) & (SSD_HEAD_DIM - 1)
    arow = jnp.sum(jnp.where(row == sub, acol, 0.0), axis=0, keepdims=True)
    decay_in = jnp.exp(jnp.where(sub <= row, acol - arow, -jnp.inf))
    a_last = acol[L - 1:L, :]
    decay_to_end = jnp.exp(a_last - acol)
    decay_from_start = jnp.exp(acol)
    chunk_decay = jnp.exp(a_last)

    xdt = xs * dtx
    xdt_bf = xdt.astype(BF16)
    xd_end = (xdt * decay_to_end).astype(BF16)
    bm_t = jnp.concatenate([bm, jnp.zeros_like(bm)], axis=0).T
    zeros_rows = jnp.zeros((L, gw), BF16)
    lane_head = lax.broadcasted_iota(jnp.int32, (L, gw), 1) // SSD_HEAD_DIM

    for g in range(n_groups):
        hs = slice(g * gw, (g + 1) * gw)
        ns = slice(g * gn, (g + 1) * gn)
        b_bf = bm[:, ns].astype(BF16)
        c_bf = cm[:, ns].astype(BF16)
        cb = lax.dot_general(c_bf, jnp.concatenate([b_bf] * heads_per_group, axis=0), NT_DIMS,
                             preferred_element_type=F32)
        lhs = (cb * decay_in[:, hs]).astype(BF16)
        xg = xdt_bf[:, hs]
        zero = jnp.zeros_like(xg)
        block_diag = jnp.concatenate(
            [jnp.where(lane_head == r, xg, zero) for r in range(heads_per_group)], axis=0)
        y = jnp.dot(lhs, block_diag, preferred_element_type=F32)
        st = st_scr[:, hs]
        y = y + jnp.dot(c_bf, st.astype(BF16), preferred_element_type=F32) * decay_from_start[:, hs]
        bt = bm_t[ns, :].astype(BF16)
        st_scr[:, hs] = chunk_decay[:, hs] * st + jnp.dot(
            bt, jnp.concatenate([xd_end[:, hs], zeros_rows], axis=0), preferred_element_type=F32)
        y = y + dsk_ref[:, hs] * xs[:, hs]
        zg = z_ref[:, hs]
        y = y * (zg * _sigmoid(zg))
        y = y * lax.rsqrt(jnp.mean(y * y, axis=-1, keepdims=True) + EPS)
        y_ref[:, hs] = (y * nw_ref[:, hs]).astype(y_ref.dtype)

    @pl.when(c == pl.num_programs(1) - 1)
    def _store_stream_state():
        stout_ref[...] = st_scr[...].T


def _ssd(zx, dt_raw, tail, st0, p, *, batch, n_chunks, valid):
    rows = zx.shape[0]
    inner = zx.shape[1] // 3
    conv_dim = 2 * inner
    heads = inner // SSD_HEAD_DIM
    L = CHUNK
    assert SSD_HEAD_DIM == CHUNK and conv_dim == inner + 2 * SSD_GROUPS * D_STATE
    expand = np.zeros((V7X_LANES, inner), np.float32)
    for h in range(heads):
        expand[h, h * SSD_HEAD_DIM:(h + 1) * SSD_HEAD_DIM] = 1.0
    tril = np.tril(np.ones((L, L), np.float32))

    def row_blk(col):
        return pl.BlockSpec((L, inner), lambda b, c, col=col: (b * n_chunks + c, col))

    def const(shape):
        return pl.BlockSpec(shape, lambda b, c: (0,) * len(shape))

    return pl.pallas_call(
        functools.partial(_ssd_body, valid=valid, heads_per_group=heads // SSD_GROUPS),
        grid=(batch, n_chunks),
        in_specs=[
            row_blk(0), row_blk(1), row_blk(2),
            pl.BlockSpec((L, V7X_LANES), lambda b, c: (b * n_chunks + c, 0)),
            pl.BlockSpec((None, V7X_SUBLANES, conv_dim), lambda b, c: (b, 0, 0)),
            pl.BlockSpec((None, inner, D_STATE), lambda b, c: (b, 0, 0)),
            const((CONV_WIDTH, conv_dim)), const((1, conv_dim)),
            const((1, V7X_LANES)), const((1, V7X_LANES)),
            const((1, inner)), const((1, inner)),
            const((V7X_LANES, inner)), const((L, L)),
        ],
        out_specs=[
            pl.BlockSpec((L, inner), lambda b, c: (b * n_chunks + c, 0)),
            pl.BlockSpec((None, inner, D_STATE), lambda b, c: (b, 0, 0)),
        ],
        out_shape=[
            jax.ShapeDtypeStruct((rows, inner), BF16),
            jax.ShapeDtypeStruct((batch, inner, D_STATE), F32),
        ],
        scratch_shapes=[
            pltpu.VMEM((L + V7X_SUBLANES, conv_dim), F32),
            pltpu.VMEM((D_STATE, inner), F32),
        ],
        compiler_params=_cparams(2),
        name="ssd",
    )(zx, zx, zx, dt_raw, tail, st0, p["conv_w"], p["conv_b"], p["dt_bias"], p["a_log"],
      p["d_skip"], p["ssd_norm_w"], jnp.asarray(expand, BF16), jnp.asarray(tril, BF16))


def _pad_lanes(v, width):
    return jnp.pad(v.astype(F32), (0, width - v.shape[0])).reshape(1, width)


def _prepare_layer(layer, norm1_w, w_in, lambda_q1, lambda_k1, lambda_q2, lambda_k2, subln_w,
                   conv_w, conv_b, dt_bias, A_log, D_skip, ssd_norm_w, w_out, norm2_w,
                   w_gate, w_up, w_down, ff_tile):
    d_model = w_in.shape[1]
    attn_w = 3 * (w_out.shape[1] // 2)
    inner = w_out.shape[1] // 2
    main_cols = attn_w + inner + 2 * inner
    cast_rows = 128
    w_in_t = jnp.swapaxes(w_in[layer], 0, 1)
    w_in_bf = _cast_rows(w_in_t, main_cols, 2 * cast_rows)
    n_heads = w_in.shape[2] - main_cols
    w_dt = jnp.pad(w_in_t[main_cols:, :].astype(BF16), ((0, V7X_LANES - n_heads), (0, 0)))
    d_ff = w_gate.shape[2]
    ff_width = d_ff + (-d_ff % ff_tile)
    return {
        "lam0": _lambda_init(layer),
        "norm1_w": norm1_w[layer], "norm2_w": norm2_w[layer],
        "w_in": w_in_bf, "w_dt": w_dt, "main_cols": main_cols,
        "lams": [v[layer].reshape(1, -1).astype(F32)
                 for v in (lambda_q1, lambda_k1, lambda_q2, lambda_k2)],
        "subln_w": subln_w[layer].reshape(1, -1).astype(F32),
        "conv_w": conv_w[layer].astype(F32), "conv_b": conv_b[layer].reshape(1, -1).astype(F32),
        "dt_bias": _pad_lanes(dt_bias[layer], V7X_LANES),
        "a_log": _pad_lanes(A_log[layer], V7X_LANES),
        "d_skip": jnp.repeat(D_skip[layer].astype(F32), SSD_HEAD_DIM).reshape(1, -1),
        "ssd_norm_w": ssd_norm_w[layer].reshape(1, -1).astype(F32),
        "late_weights": [("w_out", w_out[layer], 2 * inner, d_model),
                         ("w_gate", w_gate[layer], d_model, ff_width),
                         ("w_up", w_up[layer], d_model, ff_width),
                         ("w_down", w_down[layer], ff_width, d_model)],
        "d_model": d_model, "inner": inner,
    }


def _cast_late_weights(p):
    cast_rows = 128
    for name, w, rows_out, cols_out in p["late_weights"]:
        if rows_out == w.shape[0]:
            p[name] = _cast_cols(w, w.shape[1], cols_out, cast_rows)
        else:
            p[name] = _cast_rows(w, rows_out, 2 * cast_rows)


def _tiles(m):
    return 1024 if m % 1024 == 0 else m


def _in_proj(x2d, p):
    m = x2d.shape[0]
    inner = p["inner"]
    tm = _tiles(m)
    h = _rmsnorm(x2d, p["norm1_w"], BF16, min(256, m))
    proj = functools.partial(_matmul, [h], tm=tm, b_transposed=True)
    q, = proj([p["w_in"]], n=inner, tn=1024, b_col_off=0,
              outs=[(BF16, ATTN_SCALE * LOG2_E)], name="proj_q")
    k32, kbf = proj([p["w_in"]], n=inner, tn=1024, b_col_off=inner,
                    outs=[(F32, 1.0), (BF16, 1.0)], name="proj_k")
    v32, vbf = proj([p["w_in"]], n=inner, tn=1024, b_col_off=2 * inner,
                    outs=[(F32, 1.0), (BF16, 1.0)], name="proj_v")
    zx, = proj([p["w_in"]], n=3 * inner, tn=1024, b_col_off=3 * inner,
               outs=[(F32, 1.0)], name="proj_zx")
    dt, = proj([p["w_dt"]], n=V7X_LANES, tn=V7X_LANES, outs=[(F32, 1.0)], name="proj_dt")
    return q, k32, kbf, v32, vbf, zx, dt


def _out_and_ffn(x2d, attn, ssd_y, p):
    m = x2d.shape[0]
    tm = _tiles(m)
    x1, = _matmul([attn, ssd_y], [p["w_out"], p["w_out"]], b_row_blocks=[0, 1], n=p["d_model"],
                  tm=tm, tn=min(1024, p["d_model"]), outs=[(F32, 1.0)], residual=x2d, name="out_proj")
    h2 = _rmsnorm(x1, p["norm2_w"], BF16, min(256, m))
    act = _ffn_up(h2, p["w_gate"], p["w_up"], tm=tm, tn=512 if tm == 1024 else 1024)
    return _matmul_acc(act, p["w_down"], x1, tm=tm, tn=min(1024, p["d_model"]), tk=act.shape[1] // 4)


def kernel(x_prompt, x_sample, cache_k, cache_v, state_conv, state_ssm, norm1_w, w_in, lambda_q1, lambda_k1, lambda_q2, lambda_k2, subln_w, conv_w, conv_b, dt_bias, A_log, D_skip, ssd_norm_w, w_out, norm2_w, w_gate, w_up, w_down, final_norm_w):
    depth = w_in.shape[0]
    bp, seq, d_model = x_prompt.shape
    bs, rows, _ = x_sample.shape
    heads = cache_k.shape[3]
    inner = w_out.shape[1] // 2
    attn_w = heads * V_HEAD_DIM
    conv_dim = conv_w.shape[2]
    ssd_heads = state_ssm.shape[2]
    assert seq % CHUNK == 0 and rows <= CHUNK and rows >= CONV_WIDTH - 1 and attn_w == inner

    xp = x_prompt.reshape(bp * seq, d_model)
    xs = x_sample.reshape(bs * rows, d_model)
    outs_p = [[], [], [], []]
    outs_s = [[], [], [], []]
    for layer in range(depth):
        p = _prepare_layer(layer, norm1_w, w_in, lambda_q1, lambda_k1, lambda_q2, lambda_k2, subln_w,
                           conv_w, conv_b, dt_bias, A_log, D_skip, ssd_norm_w, w_out, norm2_w,
                           w_gate, w_up, w_down, ff_tile=1024)
        lam0 = p["lam0"]

        q, k32, kbf, v32, vbf, zx, dt = _in_proj(xp, p)
        attn_tk = 512
        n_steps = bp * heads * (seq // (2 * attn_tk))
        late = p["late_weights"]
        ride_along = all(_side_cast_rows(w.shape[0], rows, n_steps) is not None
                         and w.shape[1] % V7X_LANES == 0 for _, w, rows, _ in late)
        attn, casts = _attn_prompt(q, kbf, vbf, p["lams"], p["subln_w"],
                                   [(w, rows, cols) for _, w, rows, cols in late] if ride_along else [],
                                   batch=bp, seq=seq, heads=heads, tk=attn_tk, lam0=lam0)
        if ride_along:
            p.update({name: c for (name, _, _, _), c in zip(late, casts)})
        else:
            _cast_late_weights(p)
        ssd_y, st = _ssd(zx, dt, jnp.zeros((bp, V7X_SUBLANES, conv_dim), F32),
                         jnp.zeros((bp, inner, D_STATE), F32), p,
                         batch=bp, n_chunks=seq // CHUNK, valid=CHUNK)
        outs_p[0].append(k32.reshape(bp, seq, heads, V_HEAD_DIM))
        outs_p[1].append(v32.reshape(bp, seq, heads, V_HEAD_DIM))
        outs_p[2].append(zx.reshape(bp, seq, 3 * inner)[:, seq - (CONV_WIDTH - 1):, inner:])
        outs_p[3].append(st.reshape(bp, ssd_heads, SSD_HEAD_DIM, D_STATE))
        xp = _out_and_ffn(xp, attn, ssd_y, p)

        q, k32, kbf, v32, vbf, zx, dt = _in_proj(xs, p)
        attn = _attn_decode(q, kbf, vbf, cache_k[layer], cache_v[layer],
                            p["lams"], p["subln_w"], batch=bs, rows=rows, heads=heads, lam0=lam0)
        pad_chunk = ((0, 0), (0, CHUNK - rows), (0, 0))
        zx_c = jnp.pad(zx.reshape(bs, rows, 3 * inner), pad_chunk).reshape(bs * CHUNK, 3 * inner)
        dt_c = jnp.pad(dt.reshape(bs, rows, V7X_LANES), pad_chunk).reshape(bs * CHUNK, V7X_LANES)
        tail = jnp.pad(state_conv[layer].astype(F32),
                       ((0, 0), (V7X_SUBLANES - (CONV_WIDTH - 1), 0), (0, 0)))
        ssd_y, st = _ssd(zx_c, dt_c, tail, state_ssm[layer].astype(F32).reshape(bs, inner, D_STATE), p,
                         batch=bs, n_chunks=1, valid=rows)
        ssd_y = ssd_y.reshape(bs, CHUNK, inner)[:, :rows].reshape(bs * rows, inner)
        outs_s[0].append(k32.reshape(bs, rows, heads, V_HEAD_DIM))
        outs_s[1].append(v32.reshape(bs, rows, heads, V_HEAD_DIM))
        outs_s[2].append(zx.reshape(bs, rows, 3 * inner)[:, rows - (CONV_WIDTH - 1):, inner:])
        outs_s[3].append(st.reshape(bs, ssd_heads, SSD_HEAD_DIM, D_STATE))
        xs = _out_and_ffn(xs, attn, ssd_y, p)

    y_prompt = _rmsnorm(xp, final_norm_w, F32, 256).reshape(bp, seq, d_model)
    y_sample = _rmsnorm(xs, final_norm_w, F32, bs * rows).reshape(bs, rows, d_model)
    return (y_prompt, y_sample, *(jnp.stack(o) for o in outs_p), *(jnp.stack(o) for o in outs_s))
```

```python
import functools
import math
from typing import Any, NamedTuple

import jax
import jax.numpy as jnp
import numpy as np
from jax import lax
from jax.experimental import pallas as pl
from jax.experimental.pallas import tpu as pltpu

F32 = jnp.float32
BF16 = jnp.bfloat16

CHUNK = 64
V_HEAD_DIM = 128
QK_DIM = V_HEAD_DIM // 2
ATTN_SCALE = QK_DIM ** -0.5
LOG2_E = math.log2(math.e)
SSD_HEAD_DIM = 64
SSD_GROUPS = 8
D_STATE = 128
CONV_WIDTH = 4
EPS = 1e-6

V7X_LANES = 128
V7X_SUBLANES = 8
BF16_SUBLANES = 2 * V7X_SUBLANES
V7X_VMEM_BYTES = 64 * 1024 * 1024
VMEM_LIMIT_BYTES = V7X_VMEM_BYTES - 8 * 1024 * 1024

NT_DIMS = (((1,), (1,)), ((), ()))


def _cparams(n_axes, flags=None):
    return pltpu.CompilerParams(
        dimension_semantics=("arbitrary",) * n_axes, vmem_limit_bytes=VMEM_LIMIT_BYTES, flags=flags)


def _sigmoid(x):
    return 1.0 / (1.0 + jnp.exp(-x))


def _lambda_init(layer):
    return 0.8 - 0.6 * math.exp(-0.3 * layer)


def _rmsnorm_body(x_ref, w_ref, o_ref):
    x = x_ref[...]
    ms = jnp.mean(x * x, axis=-1, keepdims=True)
    o_ref[...] = (x * lax.rsqrt(ms + EPS) * w_ref[...]).astype(o_ref.dtype)


def _rmsnorm(x, w, out_dtype, tr):
    m, d = x.shape
    return pl.pallas_call(
        _rmsnorm_body,
        grid=(m // tr,),
        in_specs=[pl.BlockSpec((tr, d), lambda i: (i, 0)),
                  pl.BlockSpec((1, d), lambda i: (0, 0))],
        out_specs=pl.BlockSpec((tr, d), lambda i: (i, 0)),
        out_shape=jax.ShapeDtypeStruct((m, d), out_dtype),
        compiler_params=_cparams(1),
        name="rmsnorm",
    )(x, w.reshape(1, d).astype(F32))


def _cast_cols_body(x_ref, o_ref):
    n_in = x_ref.shape[1]
    o_ref[:, :n_in] = x_ref[...].astype(o_ref.dtype)
    if o_ref.shape[1] > n_in:
        o_ref[:, n_in:] = jnp.zeros((o_ref.shape[0], o_ref.shape[1] - n_in), o_ref.dtype)


def _cast_cols(w, n_in, n_out, tr):
    rows = w.shape[0]
    assert rows % tr == 0 and n_in % V7X_LANES == 0 and n_out >= n_in
    return pl.pallas_call(
        _cast_cols_body,
        grid=(rows // tr,),
        in_specs=[pl.BlockSpec((tr, n_in), lambda i: (i, 0))],
        out_specs=pl.BlockSpec((tr, n_out), lambda i: (i, 0)),
        out_shape=jax.ShapeDtypeStruct((rows, n_out), BF16),
        compiler_params=_cparams(1),
        name="cast_cols",
    )(w)


def _cast_rows_body(x_ref, o_ref, *, n_blocks_in):
    keep = pl.program_id(0) < n_blocks_in
    o_ref[...] = jnp.where(keep, x_ref[...], 0.0).astype(o_ref.dtype)


def _cast_rows(w, rows_out, tr):
    rows, n = w.shape
    assert rows_out % tr == 0 and (rows_out <= rows or rows % tr == 0)
    n_blocks_in = rows // tr
    return pl.pallas_call(
        functools.partial(_cast_rows_body, n_blocks_in=n_blocks_in),
        grid=(rows_out // tr,),
        in_specs=[pl.BlockSpec((tr, n), lambda i: (jnp.minimum(i, n_blocks_in - 1), 0))],
        out_specs=pl.BlockSpec((tr, n), lambda i: (i, 0)),
        out_shape=jax.ShapeDtypeStruct((rows_out, n), BF16),
        compiler_params=_cparams(1),
        name="cast_rows",
    )(w)


class _Out(NamedTuple):
    dtype: Any
    scale: float = 1.0
    by_head: bool = False


def _mm_body(*refs, n_pairs, scales, has_res, b_transposed):
    a_refs = refs[:n_pairs]
    b_refs = refs[n_pairs:2 * n_pairs]
    pos = 2 * n_pairs
    res_ref = refs[pos] if has_res else None
    out_refs = refs[pos + int(has_res):]
    acc = None
    for a_ref, b_ref in zip(a_refs, b_refs):
        if b_transposed:
            d = lax.dot_general(a_ref[...], b_ref[...], NT_DIMS, preferred_element_type=F32)
        else:
            d = jnp.dot(a_ref[...], b_ref[...], preferred_element_type=F32)
        acc = d if acc is None else acc + d
    if has_res:
        acc = acc + res_ref[...]
    for o_ref, s in zip(out_refs, scales):
        val = (acc if s == 1.0 else acc * s).astype(o_ref.dtype)
        o_ref[...] = val.reshape(o_ref.shape)


def _matmul(a_list, b_list, *, n, tm, tn, outs, b_row_blocks=None, b_col_off=0,
            residual=None, b_transposed=False, name="matmul"):
    m = a_list[0].shape[0]
    n_pairs = len(a_list)
    if b_row_blocks is None:
        b_row_blocks = [0] * n_pairs
    assert b_col_off % tn == 0 and n % tn == 0 and m % tm == 0
    col_blk = b_col_off // tn
    in_specs = []
    for a in a_list:
        in_specs.append(pl.BlockSpec((tm, a.shape[1]), lambda i, j: (i, 0)))
    for a, rb in zip(a_list, b_row_blocks):
        if b_transposed:
            in_specs.append(pl.BlockSpec((tn, a.shape[1]), lambda i, j, rb=rb: (col_blk + j, rb)))
        else:
            in_specs.append(pl.BlockSpec((a.shape[1], tn), lambda i, j, rb=rb: (rb, col_blk + j)))
    args = list(a_list) + list(b_list)
    if residual is not None:
        in_specs.append(pl.BlockSpec((tm, tn), lambda i, j: (i, j)))
        args.append(residual)
    out_shape, out_specs = [], []
    for dt, _, by_head in outs:
        if by_head:
            out_shape.append(jax.ShapeDtypeStruct((m, n // V_HEAD_DIM, V_HEAD_DIM), dt))
            out_specs.append(pl.BlockSpec((tm, tn // V_HEAD_DIM, V_HEAD_DIM), lambda i, j: (i, j, 0)))
        else:
            out_shape.append(jax.ShapeDtypeStruct((m, n), dt))
            out_specs.append(pl.BlockSpec((tm, tn), lambda i, j: (i, j)))
    res = pl.pallas_call(
        functools.partial(_mm_body, n_pairs=n_pairs, scales=tuple(o.scale for o in outs),
                          has_res=residual is not None, b_transposed=b_transposed),
        grid=(m // tm, n // tn),
        in_specs=in_specs,
        out_specs=out_specs,
        out_shape=out_shape,
        compiler_params=_cparams(2),
        name=name,
    )(*args)
    return res


def _ffn_up_body(h_ref, wg_ref, wu_ref, o_ref):
    h = h_ref[...]
    g = jnp.dot(h, wg_ref[...], preferred_element_type=F32)
    u = jnp.dot(h, wu_ref[...], preferred_element_type=F32)
    o_ref[...] = (g * _sigmoid(g) * u).astype(o_ref.dtype)


def _ffn_up(h, wg, wu, *, tm, tn):
    m, d = h.shape
    n = wg.shape[1]
    return pl.pallas_call(
        _ffn_up_body,
        grid=(m // tm, n // tn),
        in_specs=[pl.BlockSpec((tm, d), lambda i, j: (i, 0)),
                  pl.BlockSpec((d, tn), lambda i, j: (0, j)),
                  pl.BlockSpec((d, tn), lambda i, j: (0, j))],
        out_specs=pl.BlockSpec((tm, tn), lambda i, j: (i, j)),
        out_shape=jax.ShapeDtypeStruct((m, n), BF16),
        compiler_params=_cparams(2),
        name="ffn_up",
    )(h, wg, wu)


def _mm_acc_body(a_ref, b_ref, r_ref, o_ref):
    @pl.when(pl.program_id(2) == 0)
    def _start_from_residual():
        o_ref[...] = r_ref[...]

    o_ref[...] += jnp.dot(a_ref[...], b_ref[...], preferred_element_type=F32)


def _matmul_acc(a, b, residual, *, tm, tn, tk):
    m, kdim = a.shape
    n = b.shape[1]
    assert m % tm == 0 and n % tn == 0 and kdim % tk == 0
    return pl.pallas_call(
        _mm_acc_body,
        grid=(m // tm, n // tn, kdim // tk),
        in_specs=[pl.BlockSpec((tm, tk), lambda i, j, k: (i, k)),
                  pl.BlockSpec((tk, tn), lambda i, j, k: (k, j)),
                  pl.BlockSpec((tm, tn), lambda i, j, k: (i, j))],
        out_specs=pl.BlockSpec((tm, tn), lambda i, j, k: (i, j)),
        out_shape=jax.ShapeDtypeStruct((m, n), F32),
        compiler_params=_cparams(3),
        name="ffn_down",
    )(a, b, residual)


def _lambda_value(lq1, lk1, lq2, lk2, lam0):
    d1 = jnp.sum(lq1[...] * lk1[...], axis=-1, keepdims=True)
    d2 = jnp.sum(lq2[...] * lk2[...], axis=-1, keepdims=True)
    return jnp.exp(d1) - jnp.exp(d2) + lam0


def _subln(o, w, lam0):
    ms = jnp.mean(o * o, axis=-1, keepdims=True)
    return (o * lax.rsqrt(ms + EPS) * w) * (1.0 - lam0)


def _side_cast(in_ref, out_ref, step, n_in_blocks, n_out_blocks):
    x = in_ref[...]
    if n_out_blocks > n_in_blocks:
        x = jnp.where(jnp.minimum(step, n_out_blocks - 1) < n_in_blocks, x, 0.0)
    n_in = in_ref.shape[1]
    out_ref[:, :n_in] = x.astype(out_ref.dtype)
    if out_ref.shape[1] > n_in:
        out_ref[:, n_in:] = jnp.zeros((out_ref.shape[0], out_ref.shape[1] - n_in), out_ref.dtype)


def _attn_prompt_body(*refs, side_blocks, tq, tk, lam0):
    n_side = len(side_blocks)
    lq1, lk1, lq2, lk2, w_ref, q_ref, k_ref, v_ref = refs[:8]
    side_in = refs[8:8 + n_side]
    o_ref = refs[8 + n_side]
    side_out = refs[9 + n_side:9 + 2 * n_side]
    (k1_scr, k2_scr, vt_scr, sa, sb, pa, pb, cma, cmb, ala, alb, m_scr,
     acc_scr) = refs[9 + 2 * n_side:]
    qi = pl.program_id(2)
    step = (pl.program_id(0) * pl.num_programs(1) + pl.program_id(1)) * pl.num_programs(2) + qi
    n_kv = k1_scr.shape[0]
    k_scrs = (k1_scr, k2_scr)

    @pl.when(qi == 0)
    def _prepare_head():
        first_half = lax.broadcasted_iota(jnp.int32, (tk, V_HEAD_DIM), 1) < QK_DIM
        for c in range(n_kv):
            kb = k_ref[pl.ds(c * tk, tk), :]
            zero = jnp.zeros_like(kb)
            k1_scr[c] = jnp.where(first_half, kb, zero)
            k2_scr[c] = jnp.where(first_half, zero, kb)
            vt_scr[c, :V_HEAD_DIM, :] = v_ref[pl.ds(c * tk, tk), :].T
            ones_row = lax.broadcasted_iota(jnp.int32, (BF16_SUBLANES, tk), 0) == 0
            vt_scr[c, V_HEAD_DIM:, :] = jnp.where(ones_row, 1.0, 0.0).astype(BF16)

    m_scr[...] = jnp.full(m_scr.shape, -jnp.inf, F32)
    acc_scr[...] = jnp.zeros(acc_scr.shape, F32)
    pb[...] = jnp.zeros(pb.shape, BF16)
    alb[...] = jnp.ones(alb.shape, F32)
    for in_ref, out_ref, (n_in_blocks, n_out_blocks) in zip(side_in, side_out, side_blocks):
        _side_cast(in_ref, out_ref, step, n_in_blocks, n_out_blocks)

    def scores(j, s_scr, cm_scr, c0):
        qv = q_ref[c0:, :]
        for idx in range(2):
            s = lax.dot_general(k_scrs[idx][j], qv, NT_DIMS, preferred_element_type=F32)
            s_scr[idx, :, c0:] = s
            cm_scr[idx, :, c0:] = jnp.max(s, axis=0, keepdims=True)

    def softmax(s_scr, cm_scr, p_scr, al_scr, c0, mask):
        for idx in range(2):
            s = s_scr[idx, :, c0:]
            if mask is None:
                cmax = cm_scr[idx, :, c0:]
            else:
                s = jnp.where(mask, s, -jnp.inf)
                cmax = jnp.max(s, axis=0, keepdims=True)
            m_old = m_scr[idx, :, c0:]
            m_new = jnp.maximum(m_old, cmax)
            alpha = jnp.exp2(m_old - m_new)
            p = jnp.exp2(s - m_new)
            p_scr[idx, :, c0:] = p.astype(BF16)
            al_scr[idx, :, c0:] = alpha
            m_scr[idx, :, c0:] = m_new

    def value_product(j, p_scr, al_scr, c0):
        vt = vt_scr[j]
        for idx in range(2):
            acc_scr[idx, :, c0:] = al_scr[idx, :, c0:] * acc_scr[idx, :, c0:] + jnp.dot(
                vt, p_scr[idx, :, c0:], preferred_element_type=F32)

    scores(0, sa, cma, 0)

    def pair(jj, carry):
        j = 2 * jj
        scores(j + 1, sb, cmb, 0)
        value_product(jnp.maximum(j - 1, 0), pb, alb, 0)
        softmax(sa, cma, pa, ala, 0, None)
        scores(j + 2, sa, cma, 0)
        value_product(j, pa, ala, 0)
        softmax(sb, cmb, pb, alb, 0, None)
        return carry

    lax.fori_loop(0, qi, pair, 0)

    j0 = 2 * qi
    key_chunk = lax.broadcasted_iota(jnp.int32, (tk, tq), 0) // CHUNK
    qry_chunk = lax.broadcasted_iota(jnp.int32, (tk, tq), 1) // CHUNK
    visible = key_chunk <= qry_chunk
    scores(j0 + 1, sb, cmb, tk)
    value_product(jnp.maximum(j0 - 1, 0), pb, alb, 0)
    softmax(sa, cma, pa, ala, 0, visible)
    softmax(sb, cmb, pb, alb, tk, visible[:, :tk])
    value_product(j0, pa, ala, 0)
    value_product(j0 + 1, pb, alb, tk)

    lam = _lambda_value(lq1, lk1, lq2, lk2, lam0)
    outs = [acc_scr[idx, :V_HEAD_DIM, :] * (1.0 / acc_scr[idx, V_HEAD_DIM:V_HEAD_DIM + 1, :])
            for idx in range(2)]
    o_t = outs[0] - lam * outs[1]
    ms = jnp.mean(o_t * o_t, axis=0, keepdims=True)
    o_t = (o_t * lax.rsqrt(ms + EPS) * w_ref[...]) * (1.0 - lam0)
    o_ref[...] = o_t.astype(o_ref.dtype).T


def _side_cast_rows(rows_in, rows_out, n_steps):
    br = BF16_SUBLANES
    while rows_out > br * n_steps:
        br *= 2
    return br if rows_in % br == 0 and rows_out % br == 0 else None


def _attn_prompt(q, k, v, lams, subln_w, side, *, batch, seq, heads, tk, lam0):
    tq = 2 * tk
    nq = seq // tq
    n_kv = seq // tk
    n_steps = batch * heads * nq
    lam_specs = [pl.BlockSpec((1, QK_DIM), lambda b, h, i: (0, 0))] * 4
    side_in_specs, side_out_specs, side_shapes, side_blocks = [], [], [], []
    for w, rows_out, cols_out in side:
        br = _side_cast_rows(w.shape[0], rows_out, n_steps)
        assert br is not None and w.shape[1] % V7X_LANES == 0
        n_in_blocks, n_out_blocks = w.shape[0] // br, rows_out // br

        def block_of(b, h, i, last):
            return (jnp.minimum((b * heads + h) * nq + i, last), 0)

        side_in_specs.append(pl.BlockSpec(
            (br, w.shape[1]), functools.partial(block_of, last=n_in_blocks - 1)))
        side_out_specs.append(pl.BlockSpec(
            (br, cols_out), functools.partial(block_of, last=n_out_blocks - 1)))
        side_shapes.append(jax.ShapeDtypeStruct((rows_out, cols_out), BF16))
        side_blocks.append((n_in_blocks, n_out_blocks))
    out = pl.pallas_call(
        functools.partial(_attn_prompt_body, side_blocks=tuple(side_blocks), tq=tq, tk=tk, lam0=lam0),
        grid=(batch, heads, nq),
        in_specs=lam_specs + [
            pl.BlockSpec((V_HEAD_DIM, 1), lambda b, h, i: (0, 0)),
            pl.BlockSpec((tq, V_HEAD_DIM), lambda b, h, i: (b * nq + i, h)),
            pl.BlockSpec((seq, V_HEAD_DIM), lambda b, h, i: (b, h)),
            pl.BlockSpec((seq, V_HEAD_DIM), lambda b, h, i: (b, h)),
        ] + side_in_specs,
        out_specs=[pl.BlockSpec((tq, V_HEAD_DIM), lambda b, h, i: (b * nq + i, h))] + side_out_specs,
        out_shape=[jax.ShapeDtypeStruct(q.shape, BF16)] + side_shapes,
        scratch_shapes=[
            pltpu.VMEM((n_kv, tk, V_HEAD_DIM), BF16),
            pltpu.VMEM((n_kv, tk, V_HEAD_DIM), BF16),
            pltpu.VMEM((n_kv, V_HEAD_DIM + BF16_SUBLANES, tk), BF16),
            pltpu.VMEM((2, tk, tq), F32), pltpu.VMEM((2, tk, tq), F32),
            pltpu.VMEM((2, tk, tq), BF16), pltpu.VMEM((2, tk, tq), BF16),
            pltpu.VMEM((2, 1, tq), F32), pltpu.VMEM((2, 1, tq), F32),
            pltpu.VMEM((2, 1, tq), F32), pltpu.VMEM((2, 1, tq), F32),
            pltpu.VMEM((2, 1, tq), F32),
            pltpu.VMEM((2, V_HEAD_DIM + BF16_SUBLANES, tq), F32),
        ],
        compiler_params=_cparams(3),
        name="attn_prompt",
    )(*lams, subln_w.reshape(V_HEAD_DIM, 1), q, k, v, *(w for w, _, _ in side))
    return out[0], out[1:]


def _attn_decode_body(lq1, lk1, lq2, lk2, w_ref, q_ref, kn_ref, vn_ref, kc_ref, vc_ref, o_ref,
                      m_scr, l_scr, acc_scr, *, rows, lam0):
    kb = pl.program_id(2)
    group = kc_ref.shape[1]
    n_q = group * rows

    def stack_heads(ref):
        x = ref[...]
        return jnp.concatenate(
            [x[:, h * V_HEAD_DIM:(h + 1) * V_HEAD_DIM] for h in range(group)], axis=0)

    q = stack_heads(q_ref)
    first_half = lax.broadcasted_iota(jnp.int32, q.shape, 1) < QK_DIM
    zero = jnp.zeros_like(q)
    q_streams = (jnp.where(first_half, q, zero), jnp.where(first_half, zero, q))

    def accumulate(k2d, v2d, same_head):
        for idx in range(2):
            s = lax.dot_general(q_streams[idx], k2d, NT_DIMS, preferred_element_type=F32)
            s = jnp.where(same_head, s, -jnp.inf)
            m_old = m_scr[idx]
            m_new = jnp.maximum(m_old, jnp.max(s, axis=-1, keepdims=True))
            alpha = jnp.exp2(m_old - m_new)
            p = jnp.exp2(s - m_new)
            l_scr[idx] = alpha * l_scr[idx] + jnp.sum(p, axis=-1, keepdims=True)
            acc_scr[idx] = alpha * acc_scr[idx] + jnp.dot(
                p.astype(BF16), v2d, preferred_element_type=F32)
            m_scr[idx] = m_new

    @pl.when(kb == 0)
    def _new_rows():
        m_scr[...] = jnp.full(m_scr.shape, -jnp.inf, F32)
        l_scr[...] = jnp.zeros(l_scr.shape, F32)
        acc_scr[...] = jnp.zeros(acc_scr.shape, F32)
        q_head = lax.broadcasted_iota(jnp.int32, (n_q, n_q), 0) // rows
        k_head = lax.broadcasted_iota(jnp.int32, (n_q, n_q), 1) // rows
        accumulate(stack_heads(kn_ref), stack_heads(vn_ref), q_head == k_head)

    n_keys = kc_ref.shape[0] * group
    q_head = lax.broadcasted_iota(jnp.int32, (n_q, n_keys), 0) // rows
    k_head = lax.broadcasted_iota(jnp.int32, (n_q, n_keys), 1) % group
    accumulate(kc_ref[...].reshape(n_keys, V_HEAD_DIM).astype(BF16),
               vc_ref[...].reshape(n_keys, V_HEAD_DIM).astype(BF16), q_head == k_head)

    @pl.when(kb == pl.num_programs(2) - 1)
    def _finish():
        lam = _lambda_value(lq1, lk1, lq2, lk2, lam0)
        o = acc_scr[0] * (1.0 / l_scr[0]) - lam * (acc_scr[1] * (1.0 / l_scr[1]))
        o = _subln(o, w_ref[...], lam0).astype(o_ref.dtype)
        for h in range(group):
            o_ref[:, h * V_HEAD_DIM:(h + 1) * V_HEAD_DIM] = o[h * rows:(h + 1) * rows, :]


def _attn_decode(q, k_new, v_new, cache_k, cache_v, lams, subln_w, *, batch, rows, heads, lam0):
    past = cache_k.shape[1]
    group = V7X_SUBLANES
    t_kv = min(past, 512)
    assert heads % group == 0 and past % t_kv == 0
    gw = group * V_HEAD_DIM
    lam_specs = [pl.BlockSpec((1, QK_DIM), lambda b, g, k: (0, 0))] * 4
    new_spec = pl.BlockSpec((rows, gw), lambda b, g, k: (b, g))
    cache_spec = pl.BlockSpec((None, t_kv, group, V_HEAD_DIM), lambda b, g, k: (b, k, g, 0))
    return pl.pallas_call(
        functools.partial(_attn_decode_body, rows=rows, lam0=lam0),
        grid=(batch, heads // group, past // t_kv),
        in_specs=lam_specs + [
            pl.BlockSpec((1, V_HEAD_DIM), lambda b, g, k: (0, 0)),
            new_spec, new_spec, new_spec, cache_spec, cache_spec,
        ],
        out_specs=new_spec,
        out_shape=jax.ShapeDtypeStruct(q.shape, BF16),
        scratch_shapes=[
            pltpu.VMEM((2, group * rows, 1), F32),
            pltpu.VMEM((2, group * rows, 1), F32),
            pltpu.VMEM((2, group * rows, V_HEAD_DIM), F32),
        ],
        compiler_params=_cparams(3),
        name="attn_decode",
    )(*lams, subln_w, q, k_new, v_new, cache_k, cache_v)


def _split3(x):
    hi = x.astype(BF16)
    r1 = x - hi.astype(F32)
    mid = r1.astype(BF16)
    lo = (r1 - mid.astype(F32)).astype(BF16)
    return hi, mid, lo


def _ssd_body(z_ref, x_ref, bc_ref, dt_ref, tail_ref, st0_ref, cw_ref, cb_ref, dtb_ref, alog_ref,
              dsk_ref, nw_ref, e_ref, tril_ref, y_ref, stout_ref, xpad_scr, st_scr,
              *, valid, heads_per_group):
    c = pl.program_id(1)
    L = CHUNK
    inner = z_ref.shape[1]
    gw = heads_per_group * SSD_HEAD_DIM
    n_groups = inner // gw
    gn = D_STATE
    tail_rows = V7X_SUBLANES

    @pl.when(c == 0)
    def _load_stream_state():
        xpad_scr[0:tail_rows, :] = tail_ref[...]
        st_scr[...] = st0_ref[...].T

    xpad_scr[tail_rows:tail_rows + L, 0:inner] = x_ref[...]
    xpad_scr[tail_rows:tail_rows + L, inner:] = bc_ref[...]
    conv = cb_ref[...]
    first = tail_rows - (CONV_WIDTH - 1)
    for i in range(CONV_WIDTH):
        conv = conv + cw_ref[i:i + 1, :] * xpad_scr[first + i:first + i + L, :]
    xpad_scr[0:tail_rows, :] = xpad_scr[L:L + tail_rows, :]
    act = conv * _sigmoid(conv)
    xs = act[:, :inner]
    bm = act[:, inner:inner + n_groups * gn]
    cm = act[:, inner + n_groups * gn:]

    dtv = dt_ref[...] + dtb_ref[...]
    dt = jnp.maximum(dtv, 0.0) + jnp.log1p(jnp.exp(-jnp.abs(dtv)))
    if valid < L:
        dt = jnp.where(lax.broadcasted_iota(jnp.int32, dt.shape, 0) < valid, dt, 0.0)
    d_a = dt * (-jnp.exp(alog_ref[...]))

    pieces = jnp.concatenate(_split3(d_a), axis=1)
    cs = jnp.dot(tril_ref[...], pieces, preferred_element_type=F32)
    w = d_a.shape[1]
    a_cs = (cs[:, :w] + cs[:, w:2 * w]) + cs[:, 2 * w:]

    both = jnp.concatenate([a_cs, dt], axis=0)
    pieces = jnp.concatenate(_split3(both), axis=0)
    ex = jnp.dot(pieces, e_ref[...], preferred_element_type=F32)
    ex = (ex[:2 * L] + ex[2 * L:4 * L]) + ex[4 * L:]
    acol = ex[:L]
    dtx = ex[L:]

    row = lax.broadcasted_iota(jnp.int32, (L, inner), 0)
    sub = lax.broadcasted_iota(jnp.int32, (L, inner), 1) & (SSD_HEAD_DIM - 1)
    arow = jnp.sum(jnp.where(row == sub, acol, 0.0), axis=0, keepdims=True)
    decay_in = jnp.exp(jnp.where(sub <= row, acol - arow, -jnp.inf))
    a_last = acol[L - 1:L, :]
    decay_to_end = jnp.exp(a_last - acol)
    decay_from_start = jnp.exp(acol)
    chunk_decay = jnp.exp(a_last)

    xdt = xs * dtx
    xdt_bf = xdt.astype(BF16)
    xd_end = (xdt * decay_to_end).astype(BF16)
    bm_t = jnp.concatenate([bm, jnp.zeros_like(bm)], axis=0).T
    zeros_rows = jnp.zeros((L, gw), BF16)
    lane_head = lax.broadcasted_iota(jnp.int32, (L, gw), 1) // SSD_HEAD_DIM

    for g in range(n_groups):
        hs = slice(g * gw, (g + 1) * gw)
        ns = slice(g * gn, (g + 1) * gn)
        b_bf = bm[:, ns].astype(BF16)
        c_bf = cm[:, ns].astype(BF16)
        cb = lax.dot_general(c_bf, jnp.concatenate([b_bf] * heads_per_group, axis=0), NT_DIMS,
                             preferred_element_type=F32)
        lhs = (cb * decay_in[:, hs]).astype(BF16)
        xg = xdt_bf[:, hs]
        zero = jnp.zeros_like(xg)
        block_diag = jnp.concatenate(
            [jnp.where(lane_head == r, xg, zero) for r in range(heads_per_group)], axis=0)
        y = jnp.dot(lhs, block_diag, preferred_element_type=F32)
        st = st_scr[:, hs]
        y = y + jnp.dot(c_bf, st.astype(BF16), preferred_element_type=F32) * decay_from_start[:, hs]
        bt = bm_t[ns, :].astype(BF16)
        st_scr[:, hs] = chunk_decay[:, hs] * st + jnp.dot(
            bt, jnp.concatenate([xd_end[:, hs], zeros_rows], axis=0), preferred_element_type=F32)
        y = y + dsk_ref[:, hs] * xs[:, hs]
        zg = z_ref[:, hs]
        y = y * (zg * _sigmoid(zg))
        y = y * lax.rsqrt(jnp.mean(y * y, axis=-1, keepdims=True) + EPS)
        y_ref[:, hs] = (y * nw_ref[:, hs]).astype(y_ref.dtype)

    @pl.when(c == pl.num_programs(1) - 1)
    def _store_stream_state():
        stout_ref[...] = st_scr[...].T


def _ssd(zx, dt_raw, tail, st0, p, *, batch, n_chunks, valid):
    rows = zx.shape[0]
    inner = zx.shape[1] // 3
    conv_dim = 2 * inner
    heads = inner // SSD_HEAD_DIM
    L = CHUNK
    assert SSD_HEAD_DIM == CHUNK and conv_dim == inner + 2 * SSD_GROUPS * D_STATE
    expand = np.zeros((V7X_LANES, inner), np.float32)
    for h in range(heads):
        expand[h, h * SSD_HEAD_DIM:(h + 1) * SSD_HEAD_DIM] = 1.0
    tril = np.tril(np.ones((L, L), np.float32))

    def row_blk(col):
        return pl.BlockSpec((L, inner), lambda b, c, col=col: (b * n_chunks + c, col))

    def const(shape):
        return pl.BlockSpec(shape, lambda b, c: (0,) * len(shape))

    return pl.pallas_call(
        functools.partial(_ssd_body, valid=valid, heads_per_group=heads // SSD_GROUPS),
        grid=(batch, n_chunks),
        in_specs=[
            row_blk(0), row_blk(1), row_blk(2),
            pl.BlockSpec((L, V7X_LANES), lambda b, c: (b * n_chunks + c, 0)),
            pl.BlockSpec((None, V7X_SUBLANES, conv_dim), lambda b, c: (b, 0, 0)),
            pl.BlockSpec((None, inner, D_STATE), lambda b, c: (b, 0, 0)),
            const((CONV_WIDTH, conv_dim)), const((1, conv_dim)),
            const((1, V7X_LANES)), const((1, V7X_LANES)),
            const((1, inner)), const((1, inner)),
            const((V7X_LANES, inner)), const((L, L)),
        ],
        out_specs=[
            pl.BlockSpec((L, inner), lambda b, c: (b * n_chunks + c, 0)),
            pl.BlockSpec((None, inner, D_STATE), lambda b, c: (b, 0, 0)),
        ],
        out_shape=[
            jax.ShapeDtypeStruct((rows, inner), BF16),
            jax.ShapeDtypeStruct((batch, inner, D_STATE), F32),
        ],
        scratch_shapes=[
            pltpu.VMEM((L + V7X_SUBLANES, conv_dim), F32),
            pltpu.VMEM((D_STATE, inner), F32),
        ],
        compiler_params=_cparams(2),
        name="ssd",
    )(zx, zx, zx, dt_raw, tail, st0, p["conv_w"], p["conv_b"], p["dt_bias"], p["a_log"],
      p["d_skip"], p["ssd_norm_w"], jnp.asarray(expand, BF16), jnp.asarray(tril, BF16))


def _pad_lanes(v, width):
    return jnp.pad(v.astype(F32), (0, width - v.shape[0])).reshape(1, width)


def _prepare_layer(layer, norm1_w, w_in, lambda_q1, lambda_k1, lambda_q2, lambda_k2, subln_w,
                   conv_w, conv_b, dt_bias, A_log, D_skip, ssd_norm_w, w_out, norm2_w,
                   w_gate, w_up, w_down, ff_tile):
    d_model = w_in.shape[1]
    attn_w = 3 * (w_out.shape[1] // 2)
    inner = w_out.shape[1] // 2
    main_cols = attn_w + inner + 2 * inner
    cast_rows = 128
    w_in_t = jnp.swapaxes(w_in[layer], 0, 1)
    w_in_bf = _cast_rows(w_in_t, main_cols, 2 * cast_rows)
    n_heads = w_in.shape[2] - main_cols
    w_dt = jnp.pad(w_in_t[main_cols:, :].astype(BF16), ((0, V7X_LANES - n_heads), (0, 0)))
    d_ff = w_gate.shape[2]
    ff_width = d_ff + (-d_ff % ff_tile)
    return {
        "lam0": _lambda_init(layer),
        "norm1_w": norm1_w[layer], "norm2_w": norm2_w[layer],
        "w_in": w_in_bf, "w_dt": w_dt, "main_cols": main_cols,
        "lams": [v[layer].reshape(1, -1).astype(F32)
                 for v in (lambda_q1, lambda_k1, lambda_q2, lambda_k2)],
        "subln_w": subln_w[layer].reshape(1, -1).astype(F32),
        "conv_w": conv_w[layer].astype(F32), "conv_b": conv_b[layer].reshape(1, -1).astype(F32),
        "dt_bias": _pad_lanes(dt_bias[layer], V7X_LANES),
        "a_log": _pad_lanes(A_log[layer], V7X_LANES),
        "d_skip": jnp.repeat(D_skip[layer].astype(F32), SSD_HEAD_DIM).reshape(1, -1),
        "ssd_norm_w": ssd_norm_w[layer].reshape(1, -1).astype(F32),
        "late_weights": [("w_out", w_out[layer], 2 * inner, d_model),
                         ("w_gate", w_gate[layer], d_model, ff_width),
                         ("w_up", w_up[layer], d_model, ff_width),
                         ("w_down", w_down[layer], ff_width, d_model)],
        "d_model": d_model, "inner": inner,
    }


def _cast_late_weights(p):
    cast_rows = 128
    for name, w, rows_out, cols_out in p["late_weights"]:
        if rows_out == w.shape[0]:
            p[name] = _cast_cols(w, w.shape[1], cols_out, cast_rows)
        else:
            p[name] = _cast_rows(w, rows_out, 2 * cast_rows)


def _tiles(m):
    return 1024 if m % 1024 == 0 else m


def _in_proj(x2d, p):
    m = x2d.shape[0]
    inner = p["inner"]
    tm = _tiles(m)
    h = _rmsnorm(x2d, p["norm1_w"], BF16, min(256, m))
    proj = functools.partial(_matmul, [h], tm=tm, b_transposed=True)
    q, = proj([p["w_in"]], n=inner, tn=1024, b_col_off=0,
              outs=[_Out(BF16, ATTN_SCALE * LOG2_E)], name="proj_q")
    k32, kbf = proj([p["w_in"]], n=inner, tn=1024, b_col_off=inner,
                    outs=[_Out(F32, by_head=True), _Out(BF16)], name="proj_k")
    v32, vbf = proj([p["w_in"]], n=inner, tn=1024, b_col_off=2 * inner,
                    outs=[_Out(F32, by_head=True), _Out(BF16)], name="proj_v")
    zx, = proj([p["w_in"]], n=3 * inner, tn=1024, b_col_off=3 * inner,
               outs=[_Out(F32)], name="proj_zx")
    dt, = proj([p["w_dt"]], n=V7X_LANES, tn=V7X_LANES, outs=[_Out(F32)], name="proj_dt")
    return q, k32, kbf, v32, vbf, zx, dt


def _out_and_ffn(x2d, attn, ssd_y, p):
    m = x2d.shape[0]
    tm = _tiles(m)
    x1, = _matmul([attn, ssd_y], [p["w_out"], p["w_out"]], b_row_blocks=[0, 1], n=p["d_model"],
                  tm=tm, tn=min(1024, p["d_model"]), outs=[_Out(F32)], residual=x2d, name="out_proj")
    h2 = _rmsnorm(x1, p["norm2_w"], BF16, min(256, m))
    act = _ffn_up(h2, p["w_gate"], p["w_up"], tm=tm, tn=512 if tm == 1024 else 1024)
    return _matmul_acc(act, p["w_down"], x1, tm=tm, tn=min(1024, p["d_model"]), tk=act.shape[1] // 4)


def kernel(x_prompt, x_sample, cache_k, cache_v, state_conv, state_ssm, norm1_w, w_in, lambda_q1, lambda_k1, lambda_q2, lambda_k2, subln_w, conv_w, conv_b, dt_bias, A_log, D_skip, ssd_norm_w, w_out, norm2_w, w_gate, w_up, w_down, final_norm_w):
    depth = w_in.shape[0]
    bp, seq, d_model = x_prompt.shape
    bs, rows, _ = x_sample.shape
    heads = cache_k.shape[3]
    inner = w_out.shape[1] // 2
    attn_w = heads * V_HEAD_DIM
    conv_dim = conv_w.shape[2]
    ssd_heads = state_ssm.shape[2]
    assert seq % CHUNK == 0 and rows <= CHUNK and rows >= CONV_WIDTH - 1 and attn_w == inner

    xp = x_prompt.reshape(bp * seq, d_model)
    xs = x_sample.reshape(bs * rows, d_model)
    outs_p = [[], [], [], []]
    outs_s = [[], [], [], []]
    for layer in range(depth):
        p = _prepare_layer(layer, norm1_w, w_in, lambda_q1, lambda_k1, lambda_q2, lambda_k2, subln_w,
                           conv_w, conv_b, dt_bias, A_log, D_skip, ssd_norm_w, w_out, norm2_w,
                           w_gate, w_up, w_down, ff_tile=1024)
        lam0 = p["lam0"]

        q, k32, kbf, v32, vbf, zx, dt = _in_proj(xp, p)
        attn_tk = 512
        n_steps = bp * heads * (seq // (2 * attn_tk))
        late = p["late_weights"]
        ride_along = all(_side_cast_rows(w.shape[0], rows, n_steps) is not None
                         and w.shape[1] % V7X_LANES == 0 for _, w, rows, _ in late)
        attn, casts = _attn_prompt(q, kbf, vbf, p["lams"], p["subln_w"],
                                   [(w, rows, cols) for _, w, rows, cols in late] if ride_along else [],
                                   batch=bp, seq=seq, heads=heads, tk=attn_tk, lam0=lam0)
        if ride_along:
            p.update({name: c for (name, _, _, _), c in zip(late, casts)})
        else:
            _cast_late_weights(p)
        ssd_y, st = _ssd(zx, dt, jnp.zeros((bp, V7X_SUBLANES, conv_dim), F32),
                         jnp.zeros((bp, inner, D_STATE), F32), p,
                         batch=bp, n_chunks=seq // CHUNK, valid=CHUNK)
        outs_p[0].append(k32.reshape(bp, seq, heads, V_HEAD_DIM))
        outs_p[1].append(v32.reshape(bp, seq, heads, V_HEAD_DIM))
        outs_p[2].append(zx.reshape(bp, seq, 3 * inner)[:, seq - (CONV_WIDTH - 1):, inner:])
        outs_p[3].append(st.reshape(bp, ssd_heads, SSD_HEAD_DIM, D_STATE))
        xp = _out_and_ffn(xp, attn, ssd_y, p)

        q, k32, kbf, v32, vbf, zx, dt = _in_proj(xs, p)
        attn = _attn_decode(q, kbf, vbf, cache_k[layer], cache_v[layer],
                            p["lams"], p["subln_w"], batch=bs, rows=rows, heads=heads, lam0=lam0)
        pad_chunk = ((0, 0), (0, CHUNK - rows), (0, 0))
        zx_c = jnp.pad(zx.reshape(bs, rows, 3 * inner), pad_chunk).reshape(bs * CHUNK, 3 * inner)
        dt_c = jnp.pad(dt.reshape(bs, rows, V7X_LANES), pad_chunk).reshape(bs * CHUNK, V7X_LANES)
        tail = jnp.pad(state_conv[layer].astype(F32),
                       ((0, 0), (V7X_SUBLANES - (CONV_WIDTH - 1), 0), (0, 0)))
        ssd_y, st = _ssd(zx_c, dt_c, tail, state_ssm[layer].astype(F32).reshape(bs, inner, D_STATE), p,
                         batch=bs, n_chunks=1, valid=rows)
        ssd_y = ssd_y.reshape(bs, CHUNK, inner)[:, :rows].reshape(bs * rows, inner)
        outs_s[0].append(k32.reshape(bs, rows, heads, V_HEAD_DIM))
        outs_s[1].append(v32.reshape(bs, rows, heads, V_HEAD_DIM))
        outs_s[2].append(zx.reshape(bs, rows, 3 * inner)[:, rows - (CONV_WIDTH - 1):, inner:])
        outs_s[3].append(st.reshape(bs, ssd_heads, SSD_HEAD_DIM, D_STATE))
        xs = _out_and_ffn(xs, attn, ssd_y, p)

    y_prompt = _rmsnorm(xp, final_norm_w, F32, 256).reshape(bp, seq, d_model)
    y_sample = _rmsnorm(xs, final_norm_w, F32, bs * rows).reshape(bs, rows, d_model)
    return (y_prompt, y_sample, *(jnp.stack(o) for o in outs_p), *(jnp.stack(o) for o in outs_s))
```

```python
import functools
import math
from typing import Any, NamedTuple

import jax
import jax.numpy as jnp
import numpy as np
from jax import lax
from jax.experimental import pallas as pl
from jax.experimental.pallas import tpu as pltpu

F32 = jnp.float32
BF16 = jnp.bfloat16

CHUNK = 64
V_HEAD_DIM = 128
QK_DIM = V_HEAD_DIM // 2
ATTN_SCALE = QK_DIM ** -0.5
LOG2_E = math.log2(math.e)
SSD_HEAD_DIM = 64
SSD_GROUPS = 8
D_STATE = 128
CONV_WIDTH = 4
EPS = 1e-6

V7X_LANES = 128
V7X_SUBLANES = 8
BF16_SUBLANES = 2 * V7X_SUBLANES
V7X_VMEM_BYTES = 64 * 1024 * 1024
VMEM_LIMIT_BYTES = V7X_VMEM_BYTES - 8 * 1024 * 1024

NT_DIMS = (((1,), (1,)), ((), ()))


def _cparams(n_axes, flags=None):
    return pltpu.CompilerParams(
        dimension_semantics=("arbitrary",) * n_axes, vmem_limit_bytes=VMEM_LIMIT_BYTES, flags=flags)


def _sigmoid(x):
    return 1.0 / (1.0 + jnp.exp(-x))


def _lambda_init(layer):
    return 0.8 - 0.6 * math.exp(-0.3 * layer)


def _rmsnorm_body(x_ref, w_ref, o_ref):
    x = x_ref[...]
    ms = jnp.mean(x * x, axis=-1, keepdims=True)
    o_ref[...] = (x * lax.rsqrt(ms + EPS) * w_ref[...]).astype(o_ref.dtype)


def _rmsnorm(x, w, out_dtype, tr):
    m, d = x.shape
    return pl.pallas_call(
        _rmsnorm_body,
        grid=(m // tr,),
        in_specs=[pl.BlockSpec((tr, d), lambda i: (i, 0)),
                  pl.BlockSpec((1, d), lambda i: (0, 0))],
        out_specs=pl.BlockSpec((tr, d), lambda i: (i, 0)),
        out_shape=jax.ShapeDtypeStruct((m, d), out_dtype),
        compiler_params=_cparams(1),
        name="rmsnorm",
    )(x, w.reshape(1, d).astype(F32))


def _cast_cols_body(x_ref, o_ref):
    n_in = x_ref.shape[1]
    o_ref[:, :n_in] = x_ref[...].astype(o_ref.dtype)
    if o_ref.shape[1] > n_in:
        o_ref[:, n_in:] = jnp.zeros((o_ref.shape[0], o_ref.shape[1] - n_in), o_ref.dtype)


def _cast_cols(w, n_in, n_out, tr):
    rows = w.shape[0]
    assert rows % tr == 0 and n_in % V7X_LANES == 0 and n_out >= n_in
    return pl.pallas_call(
        _cast_cols_body,
        grid=(rows // tr,),
        in_specs=[pl.BlockSpec((tr, n_in), lambda i: (i, 0))],
        out_specs=pl.BlockSpec((tr, n_out), lambda i: (i, 0)),
        out_shape=jax.ShapeDtypeStruct((rows, n_out), BF16),
        compiler_params=_cparams(1),
        name="cast_cols",
    )(w)


def _cast_rows_body(x_ref, o_ref, *, n_blocks_in):
    keep = pl.program_id(0) < n_blocks_in
    o_ref[...] = jnp.where(keep, x_ref[...], 0.0).astype(o_ref.dtype)


def _cast_rows(w, rows_out, tr):
    rows, n = w.shape
    assert rows_out % tr == 0 and (rows_out <= rows or rows % tr == 0)
    n_blocks_in = rows // tr
    return pl.pallas_call(
        functools.partial(_cast_rows_body, n_blocks_in=n_blocks_in),
        grid=(rows_out // tr,),
        in_specs=[pl.BlockSpec((tr, n), lambda i: (jnp.minimum(i, n_blocks_in - 1), 0))],
        out_specs=pl.BlockSpec((tr, n), lambda i: (i, 0)),
        out_shape=jax.ShapeDtypeStruct((rows_out, n), BF16),
        compiler_params=_cparams(1),
        name="cast_rows",
    )(w)


class _Out(NamedTuple):
    dtype: Any
    scale: float = 1.0
    by_head: bool = False


class _SideNorm(NamedTuple):
    x: Any
    w: Any
    row_off: int
    rows: int
    out_dtype: Any
    out_total_rows: int
    out_row_off: int


def _side_block_rows(rows_in, rows_out, n_steps):
    br = BF16_SUBLANES
    while rows_out > br * n_steps:
        br *= 2
    return br if rows_in % br == 0 and rows_out % br == 0 else None


def _side_norm_specs(sn, n_steps, step_of):
    br = _side_block_rows(sn.rows, sn.rows, n_steps)
    assert br is not None and sn.row_off % br == 0 and sn.out_row_off % br == 0
    last = sn.rows // br - 1
    d = sn.x.shape[1]

    def block_at(off_blocks, *grid_idx):
        return (jnp.minimum(step_of(*grid_idx), last) + off_blocks, 0)

    in_specs = [pl.BlockSpec((br, d), functools.partial(block_at, sn.row_off // br)),
                pl.BlockSpec((1, d), lambda *grid_idx: (0, 0))]
    out_spec = pl.BlockSpec((br, d), functools.partial(block_at, sn.out_row_off // br))
    out_shape = jax.ShapeDtypeStruct((sn.out_total_rows, d), sn.out_dtype)
    return in_specs, out_spec, out_shape, [sn.x, sn.w.reshape(1, d).astype(F32)]


def _mm_body(*refs, n_pairs, scales, has_res, has_side, b_transposed):
    a_refs = refs[:n_pairs]
    b_refs = refs[n_pairs:2 * n_pairs]
    pos = 2 * n_pairs
    res_ref = refs[pos] if has_res else None
    pos += int(has_res)
    n_out = len(scales)
    if has_side:
        _rmsnorm_body(refs[pos], refs[pos + 1], refs[pos + 2 + n_out])
        pos += 2
    out_refs = refs[pos:pos + n_out]
    acc = None
    for a_ref, b_ref in zip(a_refs, b_refs):
        if b_transposed:
            d = lax.dot_general(a_ref[...], b_ref[...], NT_DIMS, preferred_element_type=F32)
        else:
            d = jnp.dot(a_ref[...], b_ref[...], preferred_element_type=F32)
        acc = d if acc is None else acc + d
    if has_res:
        acc = acc + res_ref[...]
    for o_ref, s in zip(out_refs, scales):
        val = (acc if s == 1.0 else acc * s).astype(o_ref.dtype)
        o_ref[...] = val.reshape(o_ref.shape)


def _matmul(a_list, b_list, *, n, tm, tn, outs, b_row_blocks=None, b_col_off=0, residual=None,
            b_transposed=False, row_off=0, rows=None, side_norm=None, name="matmul"):
    m = a_list[0].shape[0] - row_off if rows is None else rows
    n_pairs = len(a_list)
    if b_row_blocks is None:
        b_row_blocks = [0] * n_pairs
    assert b_col_off % tn == 0 and n % tn == 0 and m % tm == 0 and row_off % tm == 0
    col_blk = b_col_off // tn
    row_blk = row_off // tm
    n_j = n // tn
    in_specs = []
    for a in a_list:
        in_specs.append(pl.BlockSpec((tm, a.shape[1]), lambda i, j: (row_blk + i, 0)))
    for a, rb in zip(a_list, b_row_blocks):
        if b_transposed:
            in_specs.append(pl.BlockSpec((tn, a.shape[1]), lambda i, j, rb=rb: (col_blk + j, rb)))
        else:
            in_specs.append(pl.BlockSpec((a.shape[1], tn), lambda i, j, rb=rb: (rb, col_blk + j)))
    args = list(a_list) + list(b_list)
    if residual is not None:
        in_specs.append(pl.BlockSpec((tm, tn), lambda i, j: (row_blk + i, j)))
        args.append(residual)
    out_shape, out_specs = [], []
    for dt, _, by_head in outs:
        if by_head:
            out_shape.append(jax.ShapeDtypeStruct((m, n // V_HEAD_DIM, V_HEAD_DIM), dt))
            out_specs.append(pl.BlockSpec((tm, tn // V_HEAD_DIM, V_HEAD_DIM), lambda i, j: (i, j, 0)))
        else:
            out_shape.append(jax.ShapeDtypeStruct((m, n), dt))
            out_specs.append(pl.BlockSpec((tm, tn), lambda i, j: (i, j)))
    if side_norm is not None:
        s_in, s_out, s_shape, s_args = _side_norm_specs(
            side_norm, (m // tm) * n_j, lambda i, j: i * n_j + j)
        in_specs += s_in
        args += s_args
        out_specs.append(s_out)
        out_shape.append(s_shape)
    res = pl.pallas_call(
        functools.partial(_mm_body, n_pairs=n_pairs, scales=tuple(o.scale for o in outs),
                          has_res=residual is not None, has_side=side_norm is not None,
                          b_transposed=b_transposed),
        grid=(m // tm, n_j),
        in_specs=in_specs,
        out_specs=out_specs,
        out_shape=out_shape,
        compiler_params=_cparams(2),
        name=name,
    )(*args)
    return res


def _ffn_up_body(h_ref, wg_ref, wu_ref, *refs):
    if len(refs) > 1:
        side_x, side_w, o_ref, side_o = refs
        _rmsnorm_body(side_x, side_w, side_o)
    else:
        o_ref, = refs
    h = h_ref[...]
    g = jnp.dot(h, wg_ref[...], preferred_element_type=F32)
    u = jnp.dot(h, wu_ref[...], preferred_element_type=F32)
    o_ref[...] = (g * _sigmoid(g) * u).astype(o_ref.dtype)


def _ffn_up(h, wg, wu, *, tm, tn, side_norm=None):
    m, d = h.shape
    n = wg.shape[1]
    n_j = n // tn
    in_specs = [pl.BlockSpec((tm, d), lambda i, j: (i, 0)),
                pl.BlockSpec((d, tn), lambda i, j: (0, j)),
                pl.BlockSpec((d, tn), lambda i, j: (0, j))]
    args = [h, wg, wu]
    out_specs = [pl.BlockSpec((tm, tn), lambda i, j: (i, j))]
    out_shape = [jax.ShapeDtypeStruct((m, n), BF16)]
    if side_norm is not None:
        s_in, s_out, s_shape, s_args = _side_norm_specs(
            side_norm, (m // tm) * n_j, lambda i, j: i * n_j + j)
        in_specs += s_in
        args += s_args
        out_specs.append(s_out)
        out_shape.append(s_shape)
    return pl.pallas_call(
        _ffn_up_body,
        grid=(m // tm, n_j),
        in_specs=in_specs,
        out_specs=out_specs,
        out_shape=out_shape,
        compiler_params=_cparams(2),
        name="ffn_up",
    )(*args)


def _mm_acc_body(a_ref, b_ref, r_ref, *refs):
    o_ref = refs[-2] if len(refs) > 1 else refs[0]

    @pl.when(pl.program_id(2) == 0)
    def _start_from_residual():
        o_ref[...] = r_ref[...]

    if len(refs) > 1:
        side_x, side_w, _, side_o = refs
        _rmsnorm_body(side_x, side_w, side_o)
    o_ref[...] += jnp.dot(a_ref[...], b_ref[...], preferred_element_type=F32)


def _matmul_acc(a, b, residual, *, tm, tn, tk, side_norm=None):
    m, kdim = a.shape
    n = b.shape[1]
    assert m % tm == 0 and n % tn == 0 and kdim % tk == 0
    n_j, n_k = n // tn, kdim // tk
    in_specs = [pl.BlockSpec((tm, tk), lambda i, j, k: (i, k)),
                pl.BlockSpec((tk, tn), lambda i, j, k: (k, j)),
                pl.BlockSpec((tm, tn), lambda i, j, k: (i, j))]
    args = [a, b, residual]
    out_specs = [pl.BlockSpec((tm, tn), lambda i, j, k: (i, j))]
    out_shape = [jax.ShapeDtypeStruct((m, n), F32)]
    if side_norm is not None:
        s_in, s_out, s_shape, s_args = _side_norm_specs(
            side_norm, (m // tm) * n_j * n_k, lambda i, j, k: (i * n_j + j) * n_k + k)
        in_specs += s_in
        args += s_args
        out_specs.append(s_out)
        out_shape.append(s_shape)
    return pl.pallas_call(
        _mm_acc_body,
        grid=(m // tm, n_j, n_k),
        in_specs=in_specs,
        out_specs=out_specs,
        out_shape=out_shape,
        compiler_params=_cparams(3),
        name="ffn_down",
    )(*args)


def _rmsnorm_rows_body(x_ref, w_ref, dst_ref, o_ref):
    del dst_ref
    _rmsnorm_body(x_ref, w_ref, o_ref)


def _rmsnorm_into(x, w, dst, row_off, tr):
    m, d = x.shape
    assert m % tr == 0 and row_off % tr == 0
    return pl.pallas_call(
        _rmsnorm_rows_body,
        grid=(m // tr,),
        in_specs=[pl.BlockSpec((tr, d), lambda i: (i, 0)),
                  pl.BlockSpec((1, d), lambda i: (0, 0)),
                  pl.BlockSpec(memory_space=pl.ANY)],
        out_specs=pl.BlockSpec((tr, d), lambda i: (row_off // tr + i, 0)),
        out_shape=jax.ShapeDtypeStruct(dst.shape, dst.dtype),
        input_output_aliases={2: 0},
        compiler_params=_cparams(1),
        name="rmsnorm_rows",
    )(x, w.reshape(1, d).astype(F32), dst)


def _lambda_value(lq1, lk1, lq2, lk2, lam0):
    d1 = jnp.sum(lq1[...] * lk1[...], axis=-1, keepdims=True)
    d2 = jnp.sum(lq2[...] * lk2[...], axis=-1, keepdims=True)
    return jnp.exp(d1) - jnp.exp(d2) + lam0


def _subln(o, w, lam0):
    ms = jnp.mean(o * o, axis=-1, keepdims=True)
    return (o * lax.rsqrt(ms + EPS) * w) * (1.0 - lam0)


def _side_cast(in_ref, out_ref, step, n_in_blocks, n_out_blocks):
    x = in_ref[...]
    if n_out_blocks > n_in_blocks:
        x = jnp.where(jnp.minimum(step, n_out_blocks - 1) < n_in_blocks, x, 0.0)
    n_in = in_ref.shape[1]
    out_ref[:, :n_in] = x.astype(out_ref.dtype)
    if out_ref.shape[1] > n_in:
        out_ref[:, n_in:] = jnp.zeros((out_ref.shape[0], out_ref.shape[1] - n_in), out_ref.dtype)


def _attn_prompt_body(*refs, side_blocks, tq, tk, lam0):
    n_side = len(side_blocks)
    lq1, lk1, lq2, lk2, w_ref, q_ref, k_ref, v_ref = refs[:8]
    side_in = refs[8:8 + n_side]
    o_ref = refs[8 + n_side]
    side_out = refs[9 + n_side:9 + 2 * n_side]
    (k1_scr, k2_scr, vt_scr, sa, sb, pa, pb, cma, cmb, ala, alb, m_scr,
     acc_scr) = refs[9 + 2 * n_side:]
    qi = pl.program_id(2)
    step = (pl.program_id(0) * pl.num_programs(1) + pl.program_id(1)) * pl.num_programs(2) + qi
    n_kv = k1_scr.shape[0]
    k_scrs = (k1_scr, k2_scr)

    @pl.when(qi == 0)
    def _prepare_head():
        first_half = lax.broadcasted_iota(jnp.int32, (tk, V_HEAD_DIM), 1) < QK_DIM
        for c in range(n_kv):
            kb = k_ref[pl.ds(c * tk, tk), :]
            zero = jnp.zeros_like(kb)
            k1_scr[c] = jnp.where(first_half, kb, zero)
            k2_scr[c] = jnp.where(first_half, zero, kb)
            vt_scr[c, :V_HEAD_DIM, :] = v_ref[pl.ds(c * tk, tk), :].T
            ones_row = lax.broadcasted_iota(jnp.int32, (BF16_SUBLANES, tk), 0) == 0
            vt_scr[c, V_HEAD_DIM:, :] = jnp.where(ones_row, 1.0, 0.0).astype(BF16)

    m_scr[...] = jnp.full(m_scr.shape, -jnp.inf, F32)
    acc_scr[...] = jnp.zeros(acc_scr.shape, F32)
    pb[...] = jnp.zeros(pb.shape, BF16)
    alb[...] = jnp.ones(alb.shape, F32)
    for in_ref, out_ref, (n_in_blocks, n_out_blocks) in zip(side_in, side_out, side_blocks):
        _side_cast(in_ref, out_ref, step, n_in_blocks, n_out_blocks)

    def scores(j, s_scr, cm_scr, c0):
        qv = q_ref[c0:, :]
        for idx in range(2):
            s = lax.dot_general(k_scrs[idx][j], qv, NT_DIMS, preferred_element_type=F32)
            s_scr[idx, :, c0:] = s
            cm_scr[idx, :, c0:] = jnp.max(s, axis=0, keepdims=True)

    def softmax(s_scr, cm_scr, p_scr, al_scr, c0, mask):
        for idx in range(2):
            s = s_scr[idx, :, c0:]
            if mask is None:
                cmax = cm_scr[idx, :, c0:]
            else:
                s = jnp.where(mask, s, -jnp.inf)
                cmax = jnp.max(s, axis=0, keepdims=True)
            m_old = m_scr[idx, :, c0:]
            m_new = jnp.maximum(m_old, cmax)
            alpha = jnp.exp2(m_old - m_new)
            p = jnp.exp2(s - m_new)
            p_scr[idx, :, c0:] = p.astype(BF16)
            al_scr[idx, :, c0:] = alpha
            m_scr[idx, :, c0:] = m_new

    def value_product(j, p_scr, al_scr, c0):
        vt = vt_scr[j]
        for idx in range(2):
            acc_scr[idx, :, c0:] = al_scr[idx, :, c0:] * acc_scr[idx, :, c0:] + jnp.dot(
                vt, p_scr[idx, :, c0:], preferred_element_type=F32)

    scores(0, sa, cma, 0)

    def pair(jj, carry):
        j = 2 * jj
        scores(j + 1, sb, cmb, 0)
        value_product(jnp.maximum(j - 1, 0), pb, alb, 0)
        softmax(sa, cma, pa, ala, 0, None)
        scores(j + 2, sa, cma, 0)
        value_product(j, pa, ala, 0)
        softmax(sb, cmb, pb, alb, 0, None)
        return carry

    lax.fori_loop(0, qi, pair, 0)

    j0 = 2 * qi
    key_chunk = lax.broadcasted_iota(jnp.int32, (tk, tq), 0) // CHUNK
    qry_chunk = lax.broadcasted_iota(jnp.int32, (tk, tq), 1) // CHUNK
    visible = key_chunk <= qry_chunk
    scores(j0 + 1, sb, cmb, tk)
    value_product(jnp.maximum(j0 - 1, 0), pb, alb, 0)
    softmax(sa, cma, pa, ala, 0, visible)
    softmax(sb, cmb, pb, alb, tk, visible[:, :tk])
    value_product(j0, pa, ala, 0)
    value_product(j0 + 1, pb, alb, tk)

    lam = _lambda_value(lq1, lk1, lq2, lk2, lam0)
    outs = [acc_scr[idx, :V_HEAD_DIM, :] * (1.0 / acc_scr[idx, V_HEAD_DIM:V_HEAD_DIM + 1, :])
            for idx in range(2)]
    o_t = outs[0] - lam * outs[1]
    ms = jnp.mean(o_t * o_t, axis=0, keepdims=True)
    o_t = (o_t * lax.rsqrt(ms + EPS) * w_ref[...]) * (1.0 - lam0)
    o_ref[...] = o_t.astype(o_ref.dtype).T


def _attn_prompt(q, k, v, lams, subln_w, side, *, batch, seq, heads, tk, lam0):
    tq = 2 * tk
    nq = seq // tq
    n_kv = seq // tk
    n_steps = batch * heads * nq
    lam_specs = [pl.BlockSpec((1, QK_DIM), lambda b, h, i: (0, 0))] * 4
    side_in_specs, side_out_specs, side_shapes, side_blocks = [], [], [], []
    for w, rows_out, cols_out in side:
        br = _side_block_rows(w.shape[0], rows_out, n_steps)
        assert br is not None and w.shape[1] % V7X_LANES == 0
        n_in_blocks, n_out_blocks = w.shape[0] // br, rows_out // br

        def block_of(b, h, i, last):
            return (jnp.minimum((b * heads + h) * nq + i, last), 0)

        side_in_specs.append(pl.BlockSpec(
            (br, w.shape[1]), functools.partial(block_of, last=n_in_blocks - 1)))
        side_out_specs.append(pl.BlockSpec(
            (br, cols_out), functools.partial(block_of, last=n_out_blocks - 1)))
        side_shapes.append(jax.ShapeDtypeStruct((rows_out, cols_out), BF16))
        side_blocks.append((n_in_blocks, n_out_blocks))
    out = pl.pallas_call(
        functools.partial(_attn_prompt_body, side_blocks=tuple(side_blocks), tq=tq, tk=tk, lam0=lam0),
        grid=(batch, heads, nq),
        in_specs=lam_specs + [
            pl.BlockSpec((V_HEAD_DIM, 1), lambda b, h, i: (0, 0)),
            pl.BlockSpec((tq, V_HEAD_DIM), lambda b, h, i: (b * nq + i, h)),
            pl.BlockSpec((seq, V_HEAD_DIM), lambda b, h, i: (b, h)),
            pl.BlockSpec((seq, V_HEAD_DIM), lambda b, h, i: (b, h)),
        ] + side_in_specs,
        out_specs=[pl.BlockSpec((tq, V_HEAD_DIM), lambda b, h, i: (b * nq + i, h))] + side_out_specs,
        out_shape=[jax.ShapeDtypeStruct(q.shape, BF16)] + side_shapes,
        scratch_shapes=[
            pltpu.VMEM((n_kv, tk, V_HEAD_DIM), BF16),
            pltpu.VMEM((n_kv, tk, V_HEAD_DIM), BF16),
            pltpu.VMEM((n_kv, V_HEAD_DIM + BF16_SUBLANES, tk), BF16),
            pltpu.VMEM((2, tk, tq), F32), pltpu.VMEM((2, tk, tq), F32),
            pltpu.VMEM((2, tk, tq), BF16), pltpu.VMEM((2, tk, tq), BF16),
            pltpu.VMEM((2, 1, tq), F32), pltpu.VMEM((2, 1, tq), F32),
            pltpu.VMEM((2, 1, tq), F32), pltpu.VMEM((2, 1, tq), F32),
            pltpu.VMEM((2, 1, tq), F32),
            pltpu.VMEM((2, V_HEAD_DIM + BF16_SUBLANES, tq), F32),
        ],
        compiler_params=_cparams(3),
        name="attn_prompt",
    )(*lams, subln_w.reshape(V_HEAD_DIM, 1), q, k, v, *(w for w, _, _ in side))
    return out[0], out[1:]


def _attn_decode_body(lq1, lk1, lq2, lk2, w_ref, q_ref, kn_ref, vn_ref, kc_ref, vc_ref, o_ref,
                      m_scr, l_scr, acc_scr, *, rows, lam0):
    kb = pl.program_id(2)
    group = kc_ref.shape[1]
    n_q = group * rows

    def stack_heads(ref):
        x = ref[...]
        return jnp.concatenate(
            [x[:, h * V_HEAD_DIM:(h + 1) * V_HEAD_DIM] for h in range(group)], axis=0)

    q = stack_heads(q_ref)
    first_half = lax.broadcasted_iota(jnp.int32, q.shape, 1) < QK_DIM
    zero = jnp.zeros_like(q)
    q_streams = (jnp.where(first_half, q, zero), jnp.where(first_half, zero, q))

    def accumulate(k2d, v2d, same_head):
        for idx in range(2):
            s = lax.dot_general(q_streams[idx], k2d, NT_DIMS, preferred_element_type=F32)
            s = jnp.where(same_head, s, -jnp.inf)
            m_old = m_scr[idx]
            m_new = jnp.maximum(m_old, jnp.max(s, axis=-1, keepdims=True))
            alpha = jnp.exp2(m_old - m_new)
            p = jnp.exp2(s - m_new)
            l_scr[idx] = alpha * l_scr[idx] + jnp.sum(p, axis=-1, keepdims=True)
            acc_scr[idx] = alpha * acc_scr[idx] + jnp.dot(
                p.astype(BF16), v2d, preferred_element_type=F32)
            m_scr[idx] = m_new

    @pl.when(kb == 0)
    def _new_rows():
        m_scr[...] = jnp.full(m_scr.shape, -jnp.inf, F32)
        l_scr[...] = jnp.zeros(l_scr.shape, F32)
        acc_scr[...] = jnp.zeros(acc_scr.shape, F32)
        q_head = lax.broadcasted_iota(jnp.int32, (n_q, n_q), 0) // rows
        k_head = lax.broadcasted_iota(jnp.int32, (n_q, n_q), 1) // rows
        accumulate(stack_heads(kn_ref), stack_heads(vn_ref), q_head == k_head)

    n_keys = kc_ref.shape[0] * group
    q_head = lax.broadcasted_iota(jnp.int32, (n_q, n_keys), 0) // rows
    k_head = lax.broadcasted_iota(jnp.int32, (n_q, n_keys), 1) % group
    accumulate(kc_ref[...].reshape(n_keys, V_HEAD_DIM).astype(BF16),
               vc_ref[...].reshape(n_keys, V_HEAD_DIM).astype(BF16), q_head == k_head)

    @pl.when(kb == pl.num_programs(2) - 1)
    def _finish():
        lam = _lambda_value(lq1, lk1, lq2, lk2, lam0)
        o = acc_scr[0] * (1.0 / l_scr[0]) - lam * (acc_scr[1] * (1.0 / l_scr[1]))
        o = _subln(o, w_ref[...], lam0).astype(o_ref.dtype)
        for h in range(group):
            o_ref[:, h * V_HEAD_DIM:(h + 1) * V_HEAD_DIM] = o[h * rows:(h + 1) * rows, :]


def _attn_decode(q, k_new, v_new, cache_k, cache_v, lams, subln_w, *, batch, rows, heads, lam0):
    past = cache_k.shape[1]
    group = V7X_SUBLANES
    t_kv = min(past, 512)
    assert heads % group == 0 and past % t_kv == 0
    gw = group * V_HEAD_DIM
    lam_specs = [pl.BlockSpec((1, QK_DIM), lambda b, g, k: (0, 0))] * 4
    new_spec = pl.BlockSpec((rows, gw), lambda b, g, k: (b, g))
    cache_spec = pl.BlockSpec((None, t_kv, group, V_HEAD_DIM), lambda b, g, k: (b, k, g, 0))
    return pl.pallas_call(
        functools.partial(_attn_decode_body, rows=rows, lam0=lam0),
        grid=(batch, heads // group, past // t_kv),
        in_specs=lam_specs + [
            pl.BlockSpec((1, V_HEAD_DIM), lambda b, g, k: (0, 0)),
            new_spec, new_spec, new_spec, cache_spec, cache_spec,
        ],
        out_specs=new_spec,
        out_shape=jax.ShapeDtypeStruct(q.shape, BF16),
        scratch_shapes=[
            pltpu.VMEM((2, group * rows, 1), F32),
            pltpu.VMEM((2, group * rows, 1), F32),
            pltpu.VMEM((2, group * rows, V_HEAD_DIM), F32),
        ],
        compiler_params=_cparams(3),
        name="attn_decode",
    )(*lams, subln_w, q, k_new, v_new, cache_k, cache_v)


def _split3(x):
    hi = x.astype(BF16)
    r1 = x - hi.astype(F32)
    mid = r1.astype(BF16)
    lo = (r1 - mid.astype(F32)).astype(BF16)
    return hi, mid, lo


def _ssd_body(z_ref, x_ref, bc_ref, dt_ref, tail_ref, st0_ref, cw_ref, cb_ref, dtb_ref, alog_ref,
              dsk_ref, nw_ref, e_ref, tril_ref, y_ref, stout_ref, xpad_scr, st_scr,
              *, valid, heads_per_group):
    c = pl.program_id(1)
    L = CHUNK
    inner = z_ref.shape[1]
    gw = heads_per_group * SSD_HEAD_DIM
    n_groups = inner // gw
    gn = D_STATE
    tail_rows = V7X_SUBLANES

    @pl.when(c == 0)
    def _load_stream_state():
        xpad_scr[0:tail_rows, :] = tail_ref[...]
        st_scr[...] = st0_ref[...].T

    xpad_scr[tail_rows:tail_rows + L, 0:inner] = x_ref[...]
    xpad_scr[tail_rows:tail_rows + L, inner:] = bc_ref[...]
    conv = cb_ref[...]
    first = tail_rows - (CONV_WIDTH - 1)
    for i in range(CONV_WIDTH):
        conv = conv + cw_ref[i:i + 1, :] * xpad_scr[first + i:first + i + L, :]
    xpad_scr[0:tail_rows, :] = xpad_scr[L:L + tail_rows, :]
    act = conv * _sigmoid(conv)
    xs = act[:, :inner]
    bm = act[:, inner:inner + n_groups * gn]
    cm = act[:, inner + n_groups * gn:]

    dtv = dt_ref[...] + dtb_ref[...]
    dt = jnp.maximum(dtv, 0.0) + jnp.log1p(jnp.exp(-jnp.abs(dtv)))
    if valid < L:
        dt = jnp.where(lax.broadcasted_iota(jnp.int32, dt.shape, 0) < valid, dt, 0.0)
    d_a = dt * (-jnp.exp(alog_ref[...]))

    pieces = jnp.concatenate(_split3(d_a), axis=1)
    cs = jnp.dot(tril_ref[...], pieces, preferred_element_type=F32)
    w = d_a.shape[1]
    a_cs = (cs[:, :w] + cs[:, w:2 * w]) + cs[:, 2 * w:]

    both = jnp.concatenate([a_cs, dt], axis=0)
    pieces = jnp.concatenate(_split3(both), axis=0)
    ex = jnp.dot(pieces, e_ref[...], preferred_element_type=F32)
    ex = (ex[:2 * L] + ex[2 * L:4 * L]) + ex[4 * L:]
    acol = ex[:L]
    dtx = ex[L:]

    row = lax.broadcasted_iota(jnp.int32, (L, inner), 0)
    sub = lax.broadcasted_iota(jnp.int32, (L, inner), 1) & (SSD_HEAD_DIM - 1)
    arow = jnp.sum(jnp.where(row == sub, acol, 0.0), axis=0, keepdims=True)
    decay_in = jnp.exp(jnp.where(sub <= row, acol - arow, -jnp.inf))
    a_last = acol[L - 1:L, :]
    decay_to_end = jnp.exp(a_last - acol)
    decay_from_start = jnp.exp(acol)
    chunk_decay = jnp.exp(a_last)

    xdt = xs * dtx
    xdt_bf = xdt.astype(BF16)
    xd_end = (xdt * decay_to_end).astype(BF16)
    bm_t = jnp.concatenate([bm, jnp.zeros_like(bm)], axis=0).T
    zeros_rows = jnp.zeros((L, gw), BF16)
    lane_head = lax.broadcasted_iota(jnp.int32, (L, gw), 1) // SSD_HEAD_DIM

    for g in range(n_groups):
        hs = slice(g * gw, (g + 1) * gw)
        ns = slice(g * gn, (g + 1) * gn)
        b_bf = bm[:, ns].astype(BF16)
        c_bf = cm[:, ns].astype(BF16)
        cb = lax.dot_general(c_bf, jnp.concatenate([b_bf] * heads_per_group, axis=0), NT_DIMS,
                             preferred_element_type=F32)
        lhs = (cb * decay_in[:, hs]).astype(BF16)
        xg = xdt_bf[:, hs]
        zero = jnp.zeros_like(xg)
        block_diag = jnp.concatenate(
            [jnp.where(lane_head == r, xg, zero) for r in range(heads_per_group)], axis=0)
        y = jnp.dot(lhs, block_diag, preferred_element_type=F32)
        st = st_scr[:, hs]
        y = y + jnp.dot(c_bf, st.astype(BF16), preferred_element_type=F32) * decay_from_start[:, hs]
        bt = bm_t[ns, :].astype(BF16)
        st_scr[:, hs] = chunk_decay[:, hs] * st + jnp.dot(
            bt, jnp.concatenate([xd_end[:, hs], zeros_rows], axis=0), preferred_element_type=F32)
        y = y + dsk_ref[:, hs] * xs[:, hs]
        zg = z_ref[:, hs]
        y = y * (zg * _sigmoid(zg))
        y = y * lax.rsqrt(jnp.mean(y * y, axis=-1, keepdims=True) + EPS)
        y_ref[:, hs] = (y * nw_ref[:, hs]).astype(y_ref.dtype)

    @pl.when(c == pl.num_programs(1) - 1)
    def _store_stream_state():
        stout_ref[...] = st_scr[...].T


def _ssd(zx, dt_raw, tail, st0, p, *, batch, n_chunks, valid):
    rows = zx.shape[0]
    inner = zx.shape[1] // 3
    conv_dim = 2 * inner
    heads = inner // SSD_HEAD_DIM
    L = CHUNK
    assert SSD_HEAD_DIM == CHUNK and conv_dim == inner + 2 * SSD_GROUPS * D_STATE
    expand = np.zeros((V7X_LANES, inner), np.float32)
    for h in range(heads):
        expand[h, h * SSD_HEAD_DIM:(h + 1) * SSD_HEAD_DIM] = 1.0
    tril = np.tril(np.ones((L, L), np.float32))

    def row_blk(col):
        return pl.BlockSpec((L, inner), lambda b, c, col=col: (b * n_chunks + c, col))

    def const(shape):
        return pl.BlockSpec(shape, lambda b, c: (0,) * len(shape))

    return pl.pallas_call(
        functools.partial(_ssd_body, valid=valid, heads_per_group=heads // SSD_GROUPS),
        grid=(batch, n_chunks),
        in_specs=[
            row_blk(0), row_blk(1), row_blk(2),
            pl.BlockSpec((L, V7X_LANES), lambda b, c: (b * n_chunks + c, 0)),
            pl.BlockSpec((None, V7X_SUBLANES, conv_dim), lambda b, c: (b, 0, 0)),
            pl.BlockSpec((None, inner, D_STATE), lambda b, c: (b, 0, 0)),
            const((CONV_WIDTH, conv_dim)), const((1, conv_dim)),
            const((1, V7X_LANES)), const((1, V7X_LANES)),
            const((1, inner)), const((1, inner)),
            const((V7X_LANES, inner)), const((L, L)),
        ],
        out_specs=[
            pl.BlockSpec((L, inner), lambda b, c: (b * n_chunks + c, 0)),
            pl.BlockSpec((None, inner, D_STATE), lambda b, c: (b, 0, 0)),
        ],
        out_shape=[
            jax.ShapeDtypeStruct((rows, inner), BF16),
            jax.ShapeDtypeStruct((batch, inner, D_STATE), F32),
        ],
        scratch_shapes=[
            pltpu.VMEM((L + V7X_SUBLANES, conv_dim), F32),
            pltpu.VMEM((D_STATE, inner), F32),
        ],
        compiler_params=_cparams(2),
        name="ssd",
    )(zx, zx, zx, dt_raw, tail, st0, p["conv_w"], p["conv_b"], p["dt_bias"], p["a_log"],
      p["d_skip"], p["ssd_norm_w"], jnp.asarray(expand, BF16), jnp.asarray(tril, BF16))


def _pad_lanes(v, width):
    return jnp.pad(v.astype(F32), (0, width - v.shape[0])).reshape(1, width)


def _prepare_layer(layer, norm1_w, w_in, lambda_q1, lambda_k1, lambda_q2, lambda_k2, subln_w,
                   conv_w, conv_b, dt_bias, A_log, D_skip, ssd_norm_w, w_out, norm2_w,
                   w_gate, w_up, w_down, ff_tile):
    d_model = w_in.shape[1]
    attn_w = 3 * (w_out.shape[1] // 2)
    inner = w_out.shape[1] // 2
    main_cols = attn_w + inner + 2 * inner
    cast_rows = 128
    w_in_t = jnp.swapaxes(w_in[layer], 0, 1)
    w_in_bf = _cast_rows(w_in_t, main_cols, 2 * cast_rows)
    n_heads = w_in.shape[2] - main_cols
    w_dt = jnp.pad(w_in_t[main_cols:, :].astype(BF16), ((0, V7X_LANES - n_heads), (0, 0)))
    d_ff = w_gate.shape[2]
    ff_width = d_ff + (-d_ff % ff_tile)
    return {
        "lam0": _lambda_init(layer),
        "norm1_w": norm1_w[layer], "norm2_w": norm2_w[layer],
        "w_in": w_in_bf, "w_dt": w_dt, "main_cols": main_cols,
        "lams": [v[layer].reshape(1, -1).astype(F32)
                 for v in (lambda_q1, lambda_k1, lambda_q2, lambda_k2)],
        "subln_w": subln_w[layer].reshape(1, -1).astype(F32),
        "conv_w": conv_w[layer].astype(F32), "conv_b": conv_b[layer].reshape(1, -1).astype(F32),
        "dt_bias": _pad_lanes(dt_bias[layer], V7X_LANES),
        "a_log": _pad_lanes(A_log[layer], V7X_LANES),
        "d_skip": jnp.repeat(D_skip[layer].astype(F32), SSD_HEAD_DIM).reshape(1, -1),
        "ssd_norm_w": ssd_norm_w[layer].reshape(1, -1).astype(F32),
        "late_weights": [("w_out", w_out[layer], 2 * inner, d_model),
                         ("w_gate", w_gate[layer], d_model, ff_width),
                         ("w_up", w_up[layer], d_model, ff_width),
                         ("w_down", w_down[layer], ff_width, d_model)],
        "d_model": d_model, "inner": inner,
    }


def _cast_late_weights(p):
    cast_rows = 128
    for name, w, rows_out, cols_out in p["late_weights"]:
        if rows_out == w.shape[0]:
            p[name] = _cast_cols(w, w.shape[1], cols_out, cast_rows)
        else:
            p[name] = _cast_rows(w, rows_out, 2 * cast_rows)


def _tiles(m):
    return 1024 if m % 1024 == 0 else m


def _in_proj(x2d, p):
    m = x2d.shape[0]
    inner = p["inner"]
    tm = _tiles(m)
    h = _rmsnorm(x2d, p["norm1_w"], BF16, min(256, m))
    proj = functools.partial(_matmul, [h], tm=tm, b_transposed=True)
    q, = proj([p["w_in"]], n=inner, tn=1024, b_col_off=0,
              outs=[_Out(BF16, ATTN_SCALE * LOG2_E)], name="proj_q")
    k32, kbf = proj([p["w_in"]], n=inner, tn=1024, b_col_off=inner,
                    outs=[_Out(F32, by_head=True), _Out(BF16)], name="proj_k")
    v32, vbf = proj([p["w_in"]], n=inner, tn=1024, b_col_off=2 * inner,
                    outs=[_Out(F32, by_head=True), _Out(BF16)], name="proj_v")
    zx, = proj([p["w_in"]], n=3 * inner, tn=1024, b_col_off=3 * inner,
               outs=[_Out(F32)], name="proj_zx")
    dt, = proj([p["w_dt"]], n=V7X_LANES, tn=V7X_LANES, outs=[_Out(F32)], name="proj_dt")
    return q, k32, kbf, v32, vbf, zx, dt


def _out_and_ffn(x2d, attn, ssd_y, p, final_w):
    m = x2d.shape[0]
    d_model = p["d_model"]
    tm = _tiles(m)
    out_proj = functools.partial(
        _matmul, [attn, ssd_y], [p["w_out"], p["w_out"]], b_row_blocks=[0, 1], n=d_model,
        tm=tm, outs=[_Out(F32)], residual=x2d, name="out_proj")
    ffn_up = functools.partial(_ffn_up, tm=tm, tn=512 if tm == 1024 else 1024)
    ffn_down = functools.partial(_matmul_acc, tm=tm, tn=min(1024, d_model), tk=p["w_down"].shape[0] // 4)
    half = m // 2
    if m % (2 * tm) != 0:
        x1, = out_proj(tn=min(1024, d_model))
        h2 = _rmsnorm(x1, p["norm2_w"], BF16, min(256, m))
        x2, = ffn_down(ffn_up(h2, p["w_gate"], p["w_up"])[0], p["w_down"], x1)
        y = None if final_w is None else _rmsnorm(x2, final_w, F32, min(256, m))
        return x2, y

    def norm_of(x, w, dtype, total_rows):
        return _SideNorm(x, w, 0, half, dtype, total_rows, 0)

    tn_out = min(512, d_model)
    x1_a, = out_proj(tn=tn_out, row_off=0, rows=half)
    x1_b, h2_a = out_proj(tn=tn_out, row_off=half, rows=half,
                          side_norm=norm_of(x1_a, p["norm2_w"], BF16, half))
    act_a, h2_b = ffn_up(h2_a, p["w_gate"], p["w_up"], side_norm=norm_of(x1_b, p["norm2_w"], BF16, half))
    act_b, = ffn_up(h2_b, p["w_gate"], p["w_up"])
    x2_a, = ffn_down(act_a, p["w_down"], x1_a)
    if final_w is None:
        x2_b, = ffn_down(act_b, p["w_down"], x1_b)
        return jnp.concatenate([x2_a, x2_b], axis=0), None
    x2_b, y = ffn_down(act_b, p["w_down"], x1_b, side_norm=norm_of(x2_a, final_w, F32, m))
    return None, _rmsnorm_into(x2_b, final_w, y, half, 256)


def kernel(x_prompt, x_sample, cache_k, cache_v, state_conv, state_ssm, norm1_w, w_in, lambda_q1, lambda_k1, lambda_q2, lambda_k2, subln_w, conv_w, conv_b, dt_bias, A_log, D_skip, ssd_norm_w, w_out, norm2_w, w_gate, w_up, w_down, final_norm_w):
    depth = w_in.shape[0]
    bp, seq, d_model = x_prompt.shape
    bs, rows, _ = x_sample.shape
    heads = cache_k.shape[3]
    inner = w_out.shape[1] // 2
    attn_w = heads * V_HEAD_DIM
    conv_dim = conv_w.shape[2]
    ssd_heads = state_ssm.shape[2]
    assert seq % CHUNK == 0 and rows <= CHUNK and rows >= CONV_WIDTH - 1 and attn_w == inner

    xp = x_prompt.reshape(bp * seq, d_model)
    xs = x_sample.reshape(bs * rows, d_model)
    outs_p = [[], [], [], []]
    outs_s = [[], [], [], []]
    for layer in range(depth):
        p = _prepare_layer(layer, norm1_w, w_in, lambda_q1, lambda_k1, lambda_q2, lambda_k2, subln_w,
                           conv_w, conv_b, dt_bias, A_log, D_skip, ssd_norm_w, w_out, norm2_w,
                           w_gate, w_up, w_down, ff_tile=1024)
        lam0 = p["lam0"]

        q, k32, kbf, v32, vbf, zx, dt = _in_proj(xp, p)
        attn_tk = 512
        n_steps = bp * heads * (seq // (2 * attn_tk))
        late = p["late_weights"]
        ride_along = all(_side_block_rows(w.shape[0], rows_out, n_steps) is not None
                         and w.shape[1] % V7X_LANES == 0 for _, w, rows_out, _ in late)
        attn, casts = _attn_prompt(q, kbf, vbf, p["lams"], p["subln_w"],
                                   [(w, r, c) for _, w, r, c in late] if ride_along else [],
                                   batch=bp, seq=seq, heads=heads, tk=attn_tk, lam0=lam0)
        if ride_along:
            p.update({name: c for (name, _, _, _), c in zip(late, casts)})
        else:
            _cast_late_weights(p)
        ssd_y, st = _ssd(zx, dt, jnp.zeros((bp, V7X_SUBLANES, conv_dim), F32),
                         jnp.zeros((bp, inner, D_STATE), F32), p,
                         batch=bp, n_chunks=seq // CHUNK, valid=CHUNK)
        outs_p[0].append(k32.reshape(bp, seq, heads, V_HEAD_DIM))
        outs_p[1].append(v32.reshape(bp, seq, heads, V_HEAD_DIM))
        outs_p[2].append(zx.reshape(bp, seq, 3 * inner)[:, seq - (CONV_WIDTH - 1):, inner:])
        outs_p[3].append(st.reshape(bp, ssd_heads, SSD_HEAD_DIM, D_STATE))
        xp, y_prompt = _out_and_ffn(xp, attn, ssd_y, p, final_norm_w if layer == depth - 1 else None)

        q, k32, kbf, v32, vbf, zx, dt = _in_proj(xs, p)
        attn = _attn_decode(q, kbf, vbf, cache_k[layer], cache_v[layer],
                            p["lams"], p["subln_w"], batch=bs, rows=rows, heads=heads, lam0=lam0)
        pad_chunk = ((0, 0), (0, CHUNK - rows), (0, 0))
        zx_c = jnp.pad(zx.reshape(bs, rows, 3 * inner), pad_chunk).reshape(bs * CHUNK, 3 * inner)
        dt_c = jnp.pad(dt.reshape(bs, rows, V7X_LANES), pad_chunk).reshape(bs * CHUNK, V7X_LANES)
        tail = jnp.pad(state_conv[layer].astype(F32),
                       ((0, 0), (V7X_SUBLANES - (CONV_WIDTH - 1), 0), (0, 0)))
        ssd_y, st = _ssd(zx_c, dt_c, tail, state_ssm[layer].astype(F32).reshape(bs, inner, D_STATE), p,
                         batch=bs, n_chunks=1, valid=rows)
        ssd_y = ssd_y.reshape(bs, CHUNK, inner)[:, :rows].reshape(bs * rows, inner)
        outs_s[0].append(k32.reshape(bs, rows, heads, V_HEAD_DIM))
        outs_s[1].append(v32.reshape(bs, rows, heads, V_HEAD_DIM))
        outs_s[2].append(zx.reshape(bs, rows, 3 * inner)[:, rows - (CONV_WIDTH - 1):, inner:])
        outs_s[3].append(st.reshape(bs, ssd_heads, SSD_HEAD_DIM, D_STATE))
        xs, y_sample = _out_and_ffn(xs, attn, ssd_y, p, final_norm_w if layer == depth - 1 else None)

    y_prompt = y_prompt.reshape(bp, seq, d_model)
    y_sample = y_sample.reshape(bs, rows, d_model)
    return (y_prompt, y_sample, *(jnp.stack(o) for o in outs_p), *(jnp.stack(o) for o in outs_s))
```

```python
import functools
import math
from typing import Any, NamedTuple

import jax
import jax.numpy as jnp
import numpy as np
from jax import lax
from jax.experimental import pallas as pl
from jax.experimental.pallas import tpu as pltpu

F32 = jnp.float32
BF16 = jnp.bfloat16

CHUNK = 64
V_HEAD_DIM = 128
QK_DIM = V_HEAD_DIM // 2
ATTN_SCALE = QK_DIM ** -0.5
LOG2_E = math.log2(math.e)
SSD_HEAD_DIM = 64
SSD_GROUPS = 8
D_STATE = 128
CONV_WIDTH = 4
EPS = 1e-6

V7X_LANES = 128
V7X_SUBLANES = 8
BF16_SUBLANES = 2 * V7X_SUBLANES
V7X_VMEM_BYTES = 64 * 1024 * 1024
VMEM_LIMIT_BYTES = V7X_VMEM_BYTES - 8 * 1024 * 1024

NT_DIMS = (((1,), (1,)), ((), ()))


def _cparams(n_axes, flags=None):
    return pltpu.CompilerParams(
        dimension_semantics=("arbitrary",) * n_axes, vmem_limit_bytes=VMEM_LIMIT_BYTES, flags=flags)


def _silu(x):
    h = 0.5 * x
    return h + h * jnp.tanh(h)


def _lambda_init(layer):
    return 0.8 - 0.6 * math.exp(-0.3 * layer)


def _rmsnorm_body(x_ref, w_ref, o_ref):
    x = x_ref[...]
    ms = jnp.mean(x * x, axis=-1, keepdims=True)
    o_ref[...] = (x * lax.rsqrt(ms + EPS) * w_ref[...]).astype(o_ref.dtype)


def _rmsnorm(x, w, out_dtype, tr):
    m, d = x.shape
    return pl.pallas_call(
        _rmsnorm_body,
        grid=(m // tr,),
        in_specs=[pl.BlockSpec((tr, d), lambda i: (i, 0)),
                  pl.BlockSpec((1, d), lambda i: (0, 0))],
        out_specs=pl.BlockSpec((tr, d), lambda i: (i, 0)),
        out_shape=jax.ShapeDtypeStruct((m, d), out_dtype),
        compiler_params=_cparams(1),
        name="rmsnorm",
    )(x, w.reshape(1, d).astype(F32))


def _cast_cols_body(x_ref, o_ref):
    n_in = x_ref.shape[1]
    o_ref[:, :n_in] = x_ref[...].astype(o_ref.dtype)
    if o_ref.shape[1] > n_in:
        o_ref[:, n_in:] = jnp.zeros((o_ref.shape[0], o_ref.shape[1] - n_in), o_ref.dtype)


def _cast_cols(w, n_in, n_out, tr):
    rows = w.shape[0]
    assert rows % tr == 0 and n_in % V7X_LANES == 0 and n_out >= n_in
    return pl.pallas_call(
        _cast_cols_body,
        grid=(rows // tr,),
        in_specs=[pl.BlockSpec((tr, n_in), lambda i: (i, 0))],
        out_specs=pl.BlockSpec((tr, n_out), lambda i: (i, 0)),
        out_shape=jax.ShapeDtypeStruct((rows, n_out), BF16),
        compiler_params=_cparams(1),
        name="cast_cols",
    )(w)


def _cast_rows_body(x_ref, o_ref, *, n_blocks_in):
    keep = pl.program_id(0) < n_blocks_in
    o_ref[...] = jnp.where(keep, x_ref[...], 0.0).astype(o_ref.dtype)


def _cast_rows(w, rows_out, tr):
    rows, n = w.shape
    assert rows_out % tr == 0 and (rows_out <= rows or rows % tr == 0)
    n_blocks_in = rows // tr
    return pl.pallas_call(
        functools.partial(_cast_rows_body, n_blocks_in=n_blocks_in),
        grid=(rows_out // tr,),
        in_specs=[pl.BlockSpec((tr, n), lambda i: (jnp.minimum(i, n_blocks_in - 1), 0))],
        out_specs=pl.BlockSpec((tr, n), lambda i: (i, 0)),
        out_shape=jax.ShapeDtypeStruct((rows_out, n), BF16),
        compiler_params=_cparams(1),
        name="cast_rows",
    )(w)


class _Out(NamedTuple):
    dtype: Any
    scale: float = 1.0
    by_head: bool = False


class _SideNorm(NamedTuple):
    x: Any
    w: Any
    row_off: int
    rows: int
    out_dtype: Any
    out_total_rows: int
    out_row_off: int


def _side_block_rows(rows_in, rows_out, n_steps):
    br = BF16_SUBLANES
    while rows_out > br * n_steps:
        br *= 2
    return br if rows_in % br == 0 and rows_out % br == 0 else None


def _side_norm_specs(sn, n_steps, step_of):
    br = _side_block_rows(sn.rows, sn.rows, n_steps)
    assert br is not None and sn.row_off % br == 0 and sn.out_row_off % br == 0
    last = sn.rows // br - 1
    d = sn.x.shape[1]

    def block_at(off_blocks, *grid_idx):
        return (jnp.minimum(step_of(*grid_idx), last) + off_blocks, 0)

    in_specs = [pl.BlockSpec((br, d), functools.partial(block_at, sn.row_off // br)),
                pl.BlockSpec((1, d), lambda *grid_idx: (0, 0))]
    out_spec = pl.BlockSpec((br, d), functools.partial(block_at, sn.out_row_off // br))
    out_shape = jax.ShapeDtypeStruct((sn.out_total_rows, d), sn.out_dtype)
    return in_specs, out_spec, out_shape, [sn.x, sn.w.reshape(1, d).astype(F32)]


def _mm_body(*refs, n_pairs, scales, has_res, has_side, b_transposed):
    a_refs = refs[:n_pairs]
    b_refs = refs[n_pairs:2 * n_pairs]
    pos = 2 * n_pairs
    res_ref = refs[pos] if has_res else None
    pos += int(has_res)
    n_out = len(scales)
    if has_side:
        _rmsnorm_body(refs[pos], refs[pos + 1], refs[pos + 2 + n_out])
        pos += 2
    out_refs = refs[pos:pos + n_out]
    acc = None
    for a_ref, b_ref in zip(a_refs, b_refs):
        if b_transposed:
            d = lax.dot_general(a_ref[...], b_ref[...], NT_DIMS, preferred_element_type=F32)
        else:
            d = jnp.dot(a_ref[...], b_ref[...], preferred_element_type=F32)
        acc = d if acc is None else acc + d
    if has_res:
        acc = acc + res_ref[...]
    for o_ref, s in zip(out_refs, scales):
        val = (acc if s == 1.0 else acc * s).astype(o_ref.dtype)
        o_ref[...] = val.reshape(o_ref.shape)


def _matmul(a_list, b_list, *, n, tm, tn, outs, b_row_blocks=None, b_col_off=0, residual=None,
            b_transposed=False, row_off=0, rows=None, side_norm=None, name="matmul"):
    m = a_list[0].shape[0] - row_off if rows is None else rows
    n_pairs = len(a_list)
    if b_row_blocks is None:
        b_row_blocks = [0] * n_pairs
    assert b_col_off % tn == 0 and n % tn == 0 and m % tm == 0 and row_off % tm == 0
    col_blk = b_col_off // tn
    row_blk = row_off // tm
    n_j = n // tn
    in_specs = []
    for a in a_list:
        in_specs.append(pl.BlockSpec((tm, a.shape[1]), lambda i, j: (row_blk + i, 0)))
    for a, rb in zip(a_list, b_row_blocks):
        if b_transposed:
            in_specs.append(pl.BlockSpec((tn, a.shape[1]), lambda i, j, rb=rb: (col_blk + j, rb)))
        else:
            in_specs.append(pl.BlockSpec((a.shape[1], tn), lambda i, j, rb=rb: (rb, col_blk + j)))
    args = list(a_list) + list(b_list)
    if residual is not None:
        in_specs.append(pl.BlockSpec((tm, tn), lambda i, j: (row_blk + i, j)))
        args.append(residual)
    out_shape, out_specs = [], []
    for dt, _, by_head in outs:
        if by_head:
            out_shape.append(jax.ShapeDtypeStruct((m, n // V_HEAD_DIM, V_HEAD_DIM), dt))
            out_specs.append(pl.BlockSpec((tm, tn // V_HEAD_DIM, V_HEAD_DIM), lambda i, j: (i, j, 0)))
        else:
            out_shape.append(jax.ShapeDtypeStruct((m, n), dt))
            out_specs.append(pl.BlockSpec((tm, tn), lambda i, j: (i, j)))
    if side_norm is not None:
        s_in, s_out, s_shape, s_args = _side_norm_specs(
            side_norm, (m // tm) * n_j, lambda i, j: i * n_j + j)
        in_specs += s_in
        args += s_args
        out_specs.append(s_out)
        out_shape.append(s_shape)
    res = pl.pallas_call(
        functools.partial(_mm_body, n_pairs=n_pairs, scales=tuple(o.scale for o in outs),
                          has_res=residual is not None, has_side=side_norm is not None,
                          b_transposed=b_transposed),
        grid=(m // tm, n_j),
        in_specs=in_specs,
        out_specs=out_specs,
        out_shape=out_shape,
        compiler_params=_cparams(2),
        name=name,
    )(*args)
    return res


def _ffn_up_body(h_ref, wg_ref, wu_ref, *refs):
    if len(refs) > 1:
        side_x, side_w, o_ref, side_o = refs
        _rmsnorm_body(side_x, side_w, side_o)
    else:
        o_ref, = refs
    h = h_ref[...]
    g = jnp.dot(h, wg_ref[...], preferred_element_type=F32)
    u = jnp.dot(h, wu_ref[...], preferred_element_type=F32)
    o_ref[...] = (_silu(g) * u).astype(o_ref.dtype)


def _ffn_up(h, wg, wu, *, tm, tn, side_norm=None):
    m, d = h.shape
    n = wg.shape[1]
    n_j = n // tn
    in_specs = [pl.BlockSpec((tm, d), lambda i, j: (i, 0)),
                pl.BlockSpec((d, tn), lambda i, j: (0, j)),
                pl.BlockSpec((d, tn), lambda i, j: (0, j))]
    args = [h, wg, wu]
    out_specs = [pl.BlockSpec((tm, tn), lambda i, j: (i, j))]
    out_shape = [jax.ShapeDtypeStruct((m, n), BF16)]
    if side_norm is not None:
        s_in, s_out, s_shape, s_args = _side_norm_specs(
            side_norm, (m // tm) * n_j, lambda i, j: i * n_j + j)
        in_specs += s_in
        args += s_args
        out_specs.append(s_out)
        out_shape.append(s_shape)
    return pl.pallas_call(
        _ffn_up_body,
        grid=(m // tm, n_j),
        in_specs=in_specs,
        out_specs=out_specs,
        out_shape=out_shape,
        compiler_params=_cparams(2),
        name="ffn_up",
    )(*args)


def _mm_acc_body(a_ref, b_ref, r_ref, *refs):
    o_ref = refs[-2] if len(refs) > 1 else refs[0]

    @pl.when(pl.program_id(2) == 0)
    def _start_from_residual():
        o_ref[...] = r_ref[...]

    if len(refs) > 1:
        side_x, side_w, _, side_o = refs
        _rmsnorm_body(side_x, side_w, side_o)
    o_ref[...] += jnp.dot(a_ref[...], b_ref[...], preferred_element_type=F32)


def _matmul_acc(a, b, residual, *, tm, tn, tk, side_norm=None):
    m, kdim = a.shape
    n = b.shape[1]
    assert m % tm == 0 and n % tn == 0 and kdim % tk == 0
    n_j, n_k = n // tn, kdim // tk
    in_specs = [pl.BlockSpec((tm, tk), lambda i, j, k: (i, k)),
                pl.BlockSpec((tk, tn), lambda i, j, k: (k, j)),
                pl.BlockSpec((tm, tn), lambda i, j, k: (i, j))]
    args = [a, b, residual]
    out_specs = [pl.BlockSpec((tm, tn), lambda i, j, k: (i, j))]
    out_shape = [jax.ShapeDtypeStruct((m, n), F32)]
    if side_norm is not None:
        s_in, s_out, s_shape, s_args = _side_norm_specs(
            side_norm, (m // tm) * n_j * n_k, lambda i, j, k: (i * n_j + j) * n_k + k)
        in_specs += s_in
        args += s_args
        out_specs.append(s_out)
        out_shape.append(s_shape)
    return pl.pallas_call(
        _mm_acc_body,
        grid=(m // tm, n_j, n_k),
        in_specs=in_specs,
        out_specs=out_specs,
        out_shape=out_shape,
        compiler_params=_cparams(3),
        name="ffn_down",
    )(*args)


def _rmsnorm_rows_body(x_ref, w_ref, dst_ref, o_ref):
    del dst_ref
    _rmsnorm_body(x_ref, w_ref, o_ref)


def _rmsnorm_into(x, w, dst, row_off, tr):
    m, d = x.shape
    assert m % tr == 0 and row_off % tr == 0
    return pl.pallas_call(
        _rmsnorm_rows_body,
        grid=(m // tr,),
        in_specs=[pl.BlockSpec((tr, d), lambda i: (i, 0)),
                  pl.BlockSpec((1, d), lambda i: (0, 0)),
                  pl.BlockSpec(memory_space=pl.ANY)],
        out_specs=pl.BlockSpec((tr, d), lambda i: (row_off // tr + i, 0)),
        out_shape=jax.ShapeDtypeStruct(dst.shape, dst.dtype),
        input_output_aliases={2: 0},
        compiler_params=_cparams(1),
        name="rmsnorm_rows",
    )(x, w.reshape(1, d).astype(F32), dst)


def _lambda_value(lq1, lk1, lq2, lk2, lam0):
    d1 = jnp.sum(lq1[...] * lk1[...], axis=-1, keepdims=True)
    d2 = jnp.sum(lq2[...] * lk2[...], axis=-1, keepdims=True)
    return jnp.exp(d1) - jnp.exp(d2) + lam0


def _subln(o, w, lam0):
    ms = jnp.mean(o * o, axis=-1, keepdims=True)
    return (o * lax.rsqrt(ms + EPS) * w) * (1.0 - lam0)


def _side_cast(in_ref, out_ref, step, n_in_blocks, n_out_blocks):
    x = in_ref[...]
    if n_out_blocks > n_in_blocks:
        x = jnp.where(jnp.minimum(step, n_out_blocks - 1) < n_in_blocks, x, 0.0)
    n_in = in_ref.shape[1]
    out_ref[:, :n_in] = x.astype(out_ref.dtype)
    if out_ref.shape[1] > n_in:
        out_ref[:, n_in:] = jnp.zeros((out_ref.shape[0], out_ref.shape[1] - n_in), out_ref.dtype)


def _attn_prompt_body(*refs, side_blocks, tq, tk, lam0):
    n_side = len(side_blocks)
    lq1, lk1, lq2, lk2, w_ref, q_ref, k_ref, v_ref = refs[:8]
    side_in = refs[8:8 + n_side]
    o_ref = refs[8 + n_side]
    side_out = refs[9 + n_side:9 + 2 * n_side]
    (k1_scr, k2_scr, vt_scr, sa, sb, pa, pb, cma, cmb, ala, alb, m_scr,
     acc_scr) = refs[9 + 2 * n_side:]
    qi = pl.program_id(2)
    step = (pl.program_id(0) * pl.num_programs(1) + pl.program_id(1)) * pl.num_programs(2) + qi
    n_kv = k1_scr.shape[0]
    k_scrs = (k1_scr, k2_scr)

    @pl.when(qi == 0)
    def _prepare_head():
        first_half = lax.broadcasted_iota(jnp.int32, (tk, V_HEAD_DIM), 1) < QK_DIM
        for c in range(n_kv):
            kb = k_ref[pl.ds(c * tk, tk), :]
            zero = jnp.zeros_like(kb)
            k1_scr[c] = jnp.where(first_half, kb, zero)
            k2_scr[c] = jnp.where(first_half, zero, kb)
            vt_scr[c, :V_HEAD_DIM, :] = v_ref[pl.ds(c * tk, tk), :].T
            ones_row = lax.broadcasted_iota(jnp.int32, (BF16_SUBLANES, tk), 0) == 0
            vt_scr[c, V_HEAD_DIM:, :] = jnp.where(ones_row, 1.0, 0.0).astype(BF16)

    m_scr[...] = jnp.full(m_scr.shape, -jnp.inf, F32)
    acc_scr[...] = jnp.zeros(acc_scr.shape, F32)
    pb[...] = jnp.zeros(pb.shape, BF16)
    alb[...] = jnp.ones(alb.shape, F32)
    for in_ref, out_ref, (n_in_blocks, n_out_blocks) in zip(side_in, side_out, side_blocks):
        _side_cast(in_ref, out_ref, step, n_in_blocks, n_out_blocks)

    def scores(j, s_scr, cm_scr, c0):
        qv = q_ref[c0:, :]
        for idx in range(2):
            s = lax.dot_general(k_scrs[idx][j], qv, NT_DIMS, preferred_element_type=F32)
            s_scr[idx, :, c0:] = s
            cm_scr[idx, :, c0:] = jnp.max(s, axis=0, keepdims=True)

    def softmax(s_scr, cm_scr, p_scr, al_scr, c0, mask, c1=tq):
        for idx in range(2):
            s = s_scr[idx, :, c0:c1]
            if mask is None:
                cmax = cm_scr[idx, :, c0:c1]
            else:
                s = jnp.where(mask, s, -jnp.inf)
                cmax = jnp.max(s, axis=0, keepdims=True)
            m_old = m_scr[idx, :, c0:c1]
            m_new = jnp.maximum(m_old, cmax)
            alpha = jnp.exp2(m_old - m_new)
            p = jnp.exp2(s - m_new)
            p_scr[idx, :, c0:c1] = p.astype(BF16)
            al_scr[idx, :, c0:c1] = alpha
            m_scr[idx, :, c0:c1] = m_new

    def value_product(j, p_scr, al_scr, c0):
        vt = vt_scr[j]
        for idx in range(2):
            acc_scr[idx, :, c0:] = al_scr[idx, :, c0:] * acc_scr[idx, :, c0:] + jnp.dot(
                vt, p_scr[idx, :, c0:], preferred_element_type=F32)

    scores(0, sa, cma, 0)

    def pair(jj, carry):
        j = 2 * jj
        scores(j + 1, sb, cmb, 0)
        value_product(jnp.maximum(j - 1, 0), pb, alb, 0)
        softmax(sa, cma, pa, ala, 0, None)
        scores(j + 2, sa, cma, 0)
        value_product(j, pa, ala, 0)
        softmax(sb, cmb, pb, alb, 0, None)
        return carry

    lax.fori_loop(0, qi, pair, 0)

    j0 = 2 * qi
    key_chunk = lax.broadcasted_iota(jnp.int32, (tk, tk), 0) // CHUNK
    qry_chunk = lax.broadcasted_iota(jnp.int32, (tk, tk), 1) // CHUNK
    visible = key_chunk <= qry_chunk
    scores(j0 + 1, sb, cmb, tk)
    value_product(jnp.maximum(j0 - 1, 0), pb, alb, 0)
    softmax(sa, cma, pa, ala, 0, visible, c1=tk)
    softmax(sa, cma, pa, ala, tk, None)
    softmax(sb, cmb, pb, alb, tk, visible)
    value_product(j0, pa, ala, 0)
    value_product(j0 + 1, pb, alb, tk)

    lam = _lambda_value(lq1, lk1, lq2, lk2, lam0)
    outs = [acc_scr[idx, :V_HEAD_DIM, :] * (1.0 / acc_scr[idx, V_HEAD_DIM:V_HEAD_DIM + 1, :])
            for idx in range(2)]
    o_t = outs[0] - lam * outs[1]
    ms = jnp.mean(o_t * o_t, axis=0, keepdims=True)
    o_t = (o_t * lax.rsqrt(ms + EPS) * w_ref[...]) * (1.0 - lam0)
    o_ref[...] = o_t.astype(o_ref.dtype).T


def _attn_prompt(q, k, v, lams, subln_w, side, *, batch, seq, heads, tk, lam0):
    tq = 2 * tk
    nq = seq // tq
    n_kv = seq // tk
    n_steps = batch * heads * nq
    lam_specs = [pl.BlockSpec((1, QK_DIM), lambda b, h, i: (0, 0))] * 4
    side_in_specs, side_out_specs, side_shapes, side_blocks = [], [], [], []
    for w, rows_out, cols_out in side:
        br = _side_block_rows(w.shape[0], rows_out, n_steps)
        assert br is not None and w.shape[1] % V7X_LANES == 0
        n_in_blocks, n_out_blocks = w.shape[0] // br, rows_out // br

        def block_of(b, h, i, last):
            return (jnp.minimum((b * heads + h) * nq + i, last), 0)

        side_in_specs.append(pl.BlockSpec(
            (br, w.shape[1]), functools.partial(block_of, last=n_in_blocks - 1)))
        side_out_specs.append(pl.BlockSpec(
            (br, cols_out), functools.partial(block_of, last=n_out_blocks - 1)))
        side_shapes.append(jax.ShapeDtypeStruct((rows_out, cols_out), BF16))
        side_blocks.append((n_in_blocks, n_out_blocks))
    out = pl.pallas_call(
        functools.partial(_attn_prompt_body, side_blocks=tuple(side_blocks), tq=tq, tk=tk, lam0=lam0),
        grid=(batch, heads, nq),
        in_specs=lam_specs + [
            pl.BlockSpec((V_HEAD_DIM, 1), lambda b, h, i: (0, 0)),
            pl.BlockSpec((tq, V_HEAD_DIM), lambda b, h, i: (b * nq + i, h)),
            pl.BlockSpec((seq, V_HEAD_DIM), lambda b, h, i: (b, h)),
            pl.BlockSpec((seq, V_HEAD_DIM), lambda b, h, i: (b, h)),
        ] + side_in_specs,
        out_specs=[pl.BlockSpec((tq, V_HEAD_DIM), lambda b, h, i: (b * nq + i, h))] + side_out_specs,
        out_shape=[jax.ShapeDtypeStruct(q.shape, BF16)] + side_shapes,
        scratch_shapes=[
            pltpu.VMEM((n_kv, tk, V_HEAD_DIM), BF16),
            pltpu.VMEM((n_kv, tk, V_HEAD_DIM), BF16),
            pltpu.VMEM((n_kv, V_HEAD_DIM + BF16_SUBLANES, tk), BF16),
            pltpu.VMEM((2, tk, tq), F32), pltpu.VMEM((2, tk, tq), F32),
            pltpu.VMEM((2, tk, tq), BF16), pltpu.VMEM((2, tk, tq), BF16),
            pltpu.VMEM((2, 1, tq), F32), pltpu.VMEM((2, 1, tq), F32),
            pltpu.VMEM((2, 1, tq), F32), pltpu.VMEM((2, 1, tq), F32),
            pltpu.VMEM((2, 1, tq), F32),
            pltpu.VMEM((2, V_HEAD_DIM + BF16_SUBLANES, tq), F32),
        ],
        compiler_params=_cparams(3),
        name="attn_prompt",
    )(*lams, subln_w.reshape(V_HEAD_DIM, 1), q, k, v, *(w for w, _, _ in side))
    return out[0], out[1:]


def _attn_decode_body(lq1, lk1, lq2, lk2, w_ref, q_ref, kn_ref, vn_ref, kc_ref, vc_ref, o_ref,
                      m_scr, l_scr, acc_scr, *, rows, lam0):
    kb = pl.program_id(2)
    group = kc_ref.shape[1]
    n_q = group * rows

    def stack_heads(ref):
        x = ref[...]
        return jnp.concatenate(
            [x[:, h * V_HEAD_DIM:(h + 1) * V_HEAD_DIM] for h in range(group)], axis=0)

    q = stack_heads(q_ref)
    first_half = lax.broadcasted_iota(jnp.int32, q.shape, 1) < QK_DIM
    zero = jnp.zeros_like(q)
    q_both = jnp.concatenate([jnp.where(first_half, q, zero), jnp.where(first_half, zero, q)], axis=0)

    def accumulate(k2d, v2d, same_head):
        s = lax.dot_general(q_both, k2d, NT_DIMS, preferred_element_type=F32)
        s = jnp.where(jnp.concatenate([same_head, same_head], axis=0), s, -jnp.inf)
        m_old = m_scr[...]
        m_new = jnp.maximum(m_old, jnp.max(s, axis=-1, keepdims=True))
        alpha = jnp.exp2(m_old - m_new)
        p = jnp.exp2(s - m_new)
        l_scr[...] = alpha * l_scr[...] + jnp.sum(p, axis=-1, keepdims=True)
        acc_scr[...] = alpha * acc_scr[...] + jnp.dot(p.astype(BF16), v2d, preferred_element_type=F32)
        m_scr[...] = m_new

    @pl.when(kb == 0)
    def _new_rows():
        m_scr[...] = jnp.full(m_scr.shape, -jnp.inf, F32)
        l_scr[...] = jnp.zeros(l_scr.shape, F32)
        acc_scr[...] = jnp.zeros(acc_scr.shape, F32)
        q_head = lax.broadcasted_iota(jnp.int32, (n_q, n_q), 0) // rows
        k_head = lax.broadcasted_iota(jnp.int32, (n_q, n_q), 1) // rows
        accumulate(stack_heads(kn_ref), stack_heads(vn_ref), q_head == k_head)

    n_keys = kc_ref.shape[0] * group
    q_head = lax.broadcasted_iota(jnp.int32, (n_q, n_keys), 0) // rows
    k_head = lax.broadcasted_iota(jnp.int32, (n_q, n_keys), 1) % group
    accumulate(kc_ref[...].reshape(n_keys, V_HEAD_DIM).astype(BF16),
               vc_ref[...].reshape(n_keys, V_HEAD_DIM).astype(BF16), q_head == k_head)

    @pl.when(kb == pl.num_programs(2) - 1)
    def _finish():
        lam = _lambda_value(lq1, lk1, lq2, lk2, lam0)
        o = acc_scr[...] * (1.0 / l_scr[...])
        o = o[:n_q] - lam * o[n_q:]
        o = _subln(o, w_ref[...], lam0).astype(o_ref.dtype)
        for h in range(group):
            o_ref[:, h * V_HEAD_DIM:(h + 1) * V_HEAD_DIM] = o[h * rows:(h + 1) * rows, :]


def _attn_decode(q, k_new, v_new, cache_k, cache_v, lams, subln_w, *, batch, rows, heads, lam0):
    past = cache_k.shape[1]
    group = V7X_SUBLANES
    t_kv = min(past, 512)
    assert heads % group == 0 and past % t_kv == 0
    gw = group * V_HEAD_DIM
    lam_specs = [pl.BlockSpec((1, QK_DIM), lambda b, g, k: (0, 0))] * 4
    new_spec = pl.BlockSpec((rows, gw), lambda b, g, k: (b, g))
    cache_spec = pl.BlockSpec((None, t_kv, group, V_HEAD_DIM), lambda b, g, k: (b, k, g, 0))
    return pl.pallas_call(
        functools.partial(_attn_decode_body, rows=rows, lam0=lam0),
        grid=(batch, heads // group, past // t_kv),
        in_specs=lam_specs + [
            pl.BlockSpec((1, V_HEAD_DIM), lambda b, g, k: (0, 0)),
            new_spec, new_spec, new_spec, cache_spec, cache_spec,
        ],
        out_specs=new_spec,
        out_shape=jax.ShapeDtypeStruct(q.shape, BF16),
        scratch_shapes=[
            pltpu.VMEM((2 * group * rows, 1), F32),
            pltpu.VMEM((2 * group * rows, 1), F32),
            pltpu.VMEM((2 * group * rows, V_HEAD_DIM), F32),
        ],
        compiler_params=_cparams(3),
        name="attn_decode",
    )(*lams, subln_w, q, k_new, v_new, cache_k, cache_v)


def _split3(x):
    hi = x.astype(BF16)
    r1 = x - hi.astype(F32)
    mid = r1.astype(BF16)
    lo = (r1 - mid.astype(F32)).astype(BF16)
    return hi, mid, lo


def _ssd_body(z_ref, x_ref, bc_ref, dt_ref, tail_ref, st0_ref, cw_ref, cb_ref, dtb_ref, alog_ref,
              dsk_ref, nw_ref, e_ref, tril_ref, y_ref, stout_ref, xpad_scr, st_scr,
              *, valid, heads_per_group):
    c = pl.program_id(1)
    L = CHUNK
    inner = z_ref.shape[1]
    gw = heads_per_group * SSD_HEAD_DIM
    n_groups = inner // gw
    gn = D_STATE
    tail_rows = V7X_SUBLANES

    @pl.when(c == 0)
    def _load_stream_state():
        xpad_scr[0:tail_rows, :] = tail_ref[...]
        st_scr[...] = st0_ref[...].T

    xpad_scr[tail_rows:tail_rows + L, 0:inner] = x_ref[...]
    xpad_scr[tail_rows:tail_rows + L, inner:] = bc_ref[...]
    conv = cb_ref[...]
    first = tail_rows - (CONV_WIDTH - 1)
    for i in range(CONV_WIDTH):
        conv = conv + cw_ref[i:i + 1, :] * xpad_scr[first + i:first + i + L, :]
    xpad_scr[0:tail_rows, :] = xpad_scr[L:L + tail_rows, :]
    act = _silu(conv)
    xs = act[:, :inner]
    bm = act[:, inner:inner + n_groups * gn]
    cm = act[:, inner + n_groups * gn:]

    dtv = dt_ref[...] + dtb_ref[...]
    dt = jnp.maximum(dtv, 0.0) + jnp.log1p(jnp.exp(-jnp.abs(dtv)))
    if valid < L:
        dt = jnp.where(lax.broadcasted_iota(jnp.int32, dt.shape, 0) < valid, dt, 0.0)
    d_a = dt * (-jnp.exp(alog_ref[...]))

    pieces = jnp.concatenate(_split3(d_a), axis=1)
    cs = jnp.dot(tril_ref[...], pieces, preferred_element_type=F32)
    w = d_a.shape[1]
    a_cs = (cs[:, :w] + cs[:, w:2 * w]) + cs[:, 2 * w:]

    both = jnp.concatenate([a_cs, dt], axis=0)
    pieces = jnp.concatenate(_split3(both), axis=0)
    ex = jnp.dot(pieces, e_ref[...], preferred_element_type=F32)
    ex = (ex[:2 * L] + ex[2 * L:4 * L]) + ex[4 * L:]
    acol = ex[:L]
    dtx = ex[L:]

    row = lax.broadcasted_iota(jnp.int32, (L, inner), 0)
    sub = lax.broadcasted_iota(jnp.int32, (L, inner), 1) & (SSD_HEAD_DIM - 1)
    arow = jnp.sum(jnp.where(row == sub, acol, 0.0), axis=0, keepdims=True)
    decay_in = jnp.exp(jnp.where(sub <= row, acol - arow, -jnp.inf))
    a_last = acol[L - 1:L, :]
    decay_to_end = jnp.exp(a_last - acol)
    decay_from_start = jnp.exp(acol)
    chunk_decay = jnp.exp(a_last)

    xdt = xs * dtx
    xdt_bf = xdt.astype(BF16)
    xd_end = (xdt * decay_to_end).astype(BF16)
    bm_t = jnp.concatenate([bm, jnp.zeros_like(bm)], axis=0).T
    zeros_rows = jnp.zeros((L, gw), BF16)
    lane_head = lax.broadcasted_iota(jnp.int32, (L, gw), 1) // SSD_HEAD_DIM

    for g in range(n_groups):
        hs = slice(g * gw, (g + 1) * gw)
        ns = slice(g * gn, (g + 1) * gn)
        b_bf = bm[:, ns].astype(BF16)
        c_bf = cm[:, ns].astype(BF16)
        cb = lax.dot_general(c_bf, jnp.concatenate([b_bf] * heads_per_group, axis=0), NT_DIMS,
                             preferred_element_type=F32)
        lhs = (cb * decay_in[:, hs]).astype(BF16)
        xg = xdt_bf[:, hs]
        zero = jnp.zeros_like(xg)
        block_diag = jnp.concatenate(
            [jnp.where(lane_head == r, xg, zero) for r in range(heads_per_group)], axis=0)
        y = jnp.dot(lhs, block_diag, preferred_element_type=F32)
        st = st_scr[:, hs]
        y = y + jnp.dot(c_bf, st.astype(BF16), preferred_element_type=F32) * decay_from_start[:, hs]
        bt = bm_t[ns, :].astype(BF16)
        st_scr[:, hs] = chunk_decay[:, hs] * st + jnp.dot(
            bt, jnp.concatenate([xd_end[:, hs], zeros_rows], axis=0), preferred_element_type=F32)
        y = y + dsk_ref[:, hs] * xs[:, hs]
        y = y * _silu(z_ref[:, hs])
        y = y * lax.rsqrt(jnp.mean(y * y, axis=-1, keepdims=True) + EPS)
        y_ref[:, hs] = (y * nw_ref[:, hs]).astype(y_ref.dtype)

    @pl.when(c == pl.num_programs(1) - 1)
    def _store_stream_state():
        stout_ref[...] = st_scr[...].T


def _ssd(zx, dt_raw, tail, st0, p, *, batch, n_chunks, valid):
    rows = zx.shape[0]
    inner = zx.shape[1] // 3
    conv_dim = 2 * inner
    heads = inner // SSD_HEAD_DIM
    L = CHUNK
    assert SSD_HEAD_DIM == CHUNK and conv_dim == inner + 2 * SSD_GROUPS * D_STATE
    expand = np.zeros((V7X_LANES, inner), np.float32)
    for h in range(heads):
        expand[h, h * SSD_HEAD_DIM:(h + 1) * SSD_HEAD_DIM] = 1.0
    tril = np.tril(np.ones((L, L), np.float32))

    def row_blk(col):
        return pl.BlockSpec((L, inner), lambda b, c, col=col: (b * n_chunks + c, col))

    def const(shape):
        return pl.BlockSpec(shape, lambda b, c: (0,) * len(shape))

    return pl.pallas_call(
        functools.partial(_ssd_body, valid=valid, heads_per_group=heads // SSD_GROUPS),
        grid=(batch, n_chunks),
        in_specs=[
            row_blk(0), row_blk(1), row_blk(2),
            pl.BlockSpec((L, V7X_LANES), lambda b, c: (b * n_chunks + c, 0)),
            pl.BlockSpec((None, V7X_SUBLANES, conv_dim), lambda b, c: (b, 0, 0)),
            pl.BlockSpec((None, inner, D_STATE), lambda b, c: (b, 0, 0)),
            const((CONV_WIDTH, conv_dim)), const((1, conv_dim)),
            const((1, V7X_LANES)), const((1, V7X_LANES)),
            const((1, inner)), const((1, inner)),
            const((V7X_LANES, inner)), const((L, L)),
        ],
        out_specs=[
            pl.BlockSpec((L, inner), lambda b, c: (b * n_chunks + c, 0)),
            pl.BlockSpec((None, inner, D_STATE), lambda b, c: (b, 0, 0)),
        ],
        out_shape=[
            jax.ShapeDtypeStruct((rows, inner), BF16),
            jax.ShapeDtypeStruct((batch, inner, D_STATE), F32),
        ],
        scratch_shapes=[
            pltpu.VMEM((L + V7X_SUBLANES, conv_dim), F32),
            pltpu.VMEM((D_STATE, inner), F32),
        ],
        compiler_params=_cparams(2),
        name="ssd",
    )(zx, zx, zx, dt_raw, tail, st0, p["conv_w"], p["conv_b"], p["dt_bias"], p["a_log"],
      p["d_skip"], p["ssd_norm_w"], jnp.asarray(expand, BF16), jnp.asarray(tril, BF16))


def _pad_lanes(v, width):
    return jnp.pad(v.astype(F32), (0, width - v.shape[0])).reshape(1, width)


def _prepare_layer(layer, norm1_w, w_in, lambda_q1, lambda_k1, lambda_q2, lambda_k2, subln_w,
                   conv_w, conv_b, dt_bias, A_log, D_skip, ssd_norm_w, w_out, norm2_w,
                   w_gate, w_up, w_down, ff_tile):
    d_model = w_in.shape[1]
    attn_w = 3 * (w_out.shape[1] // 2)
    inner = w_out.shape[1] // 2
    main_cols = attn_w + inner + 2 * inner
    cast_rows = 128
    w_in_t = jnp.swapaxes(w_in[layer], 0, 1)
    w_in_bf = _cast_rows(w_in_t, main_cols, 2 * cast_rows)
    n_heads = w_in.shape[2] - main_cols
    w_dt = jnp.pad(w_in_t[main_cols:, :].astype(BF16), ((0, V7X_LANES - n_heads), (0, 0)))
    d_ff = w_gate.shape[2]
    ff_width = d_ff + (-d_ff % ff_tile)
    return {
        "lam0": _lambda_init(layer),
        "norm1_w": norm1_w[layer], "norm2_w": norm2_w[layer],
        "w_in": w_in_bf, "w_dt": w_dt, "main_cols": main_cols,
        "lams": [v[layer].reshape(1, -1).astype(F32)
                 for v in (lambda_q1, lambda_k1, lambda_q2, lambda_k2)],
        "subln_w": subln_w[layer].reshape(1, -1).astype(F32),
        "conv_w": conv_w[layer].astype(F32), "conv_b": conv_b[layer].reshape(1, -1).astype(F32),
        "dt_bias": _pad_lanes(dt_bias[layer], V7X_LANES),
        "a_log": _pad_lanes(A_log[layer], V7X_LANES),
        "d_skip": jnp.repeat(D_skip[layer].astype(F32), SSD_HEAD_DIM).reshape(1, -1),
        "ssd_norm_w": ssd_norm_w[layer].reshape(1, -1).astype(F32),
        "late_weights": [("w_out", w_out[layer], 2 * inner, d_model),
                         ("w_gate", w_gate[layer], d_model, ff_width),
                         ("w_up", w_up[layer], d_model, ff_width),
                         ("w_down", w_down[layer], ff_width, d_model)],
        "d_model": d_model, "inner": inner,
    }


def _cast_late_weights(p):
    cast_rows = 128
    for name, w, rows_out, cols_out in p["late_weights"]:
        if rows_out == w.shape[0]:
            p[name] = _cast_cols(w, w.shape[1], cols_out, cast_rows)
        else:
            p[name] = _cast_rows(w, rows_out, 2 * cast_rows)


def _tiles(m):
    return 1024 if m % 1024 == 0 else m


def _in_proj(x2d, p):
    m = x2d.shape[0]
    inner = p["inner"]
    tm = _tiles(m)
    h = _rmsnorm(x2d, p["norm1_w"], BF16, min(256, m))
    proj = functools.partial(_matmul, [h], tm=tm, b_transposed=True)
    q, = proj([p["w_in"]], n=inner, tn=1024, b_col_off=0,
              outs=[_Out(BF16, ATTN_SCALE * LOG2_E)], name="proj_q")
    k32, kbf = proj([p["w_in"]], n=inner, tn=1024, b_col_off=inner,
                    outs=[_Out(F32, by_head=True), _Out(BF16)], name="proj_k")
    v32, vbf = proj([p["w_in"]], n=inner, tn=1024, b_col_off=2 * inner,
                    outs=[_Out(F32, by_head=True), _Out(BF16)], name="proj_v")
    zx, = proj([p["w_in"]], n=3 * inner, tn=1024, b_col_off=3 * inner,
               outs=[_Out(F32)], name="proj_zx")
    dt, = proj([p["w_dt"]], n=V7X_LANES, tn=V7X_LANES, outs=[_Out(F32)], name="proj_dt")
    return q, k32, kbf, v32, vbf, zx, dt


def _out_and_ffn(x2d, attn, ssd_y, p, final_w):
    m = x2d.shape[0]
    d_model = p["d_model"]
    tm = _tiles(m)
    out_proj = functools.partial(
        _matmul, [attn, ssd_y], [p["w_out"], p["w_out"]], b_row_blocks=[0, 1], n=d_model,
        tm=tm, outs=[_Out(F32)], residual=x2d, name="out_proj")
    ffn_up = functools.partial(_ffn_up, tm=tm, tn=512 if tm == 1024 else 1024)
    ffn_down = functools.partial(_matmul_acc, tm=tm, tn=min(1024, d_model), tk=p["w_down"].shape[0] // 4)
    half = m // 2
    if m % (2 * tm) != 0:
        x1, = out_proj(tn=min(1024, d_model))
        h2 = _rmsnorm(x1, p["norm2_w"], BF16, min(256, m))
        x2, = ffn_down(ffn_up(h2, p["w_gate"], p["w_up"])[0], p["w_down"], x1)
        y = None if final_w is None else _rmsnorm(x2, final_w, F32, min(256, m))
        return x2, y

    def norm_of(x, w, dtype, total_rows):
        return _SideNorm(x, w, 0, half, dtype, total_rows, 0)

    tn_out = min(512, d_model)
    x1_a, = out_proj(tn=tn_out, row_off=0, rows=half)
    x1_b, h2_a = out_proj(tn=tn_out, row_off=half, rows=half,
                          side_norm=norm_of(x1_a, p["norm2_w"], BF16, half))
    act_a, h2_b = ffn_up(h2_a, p["w_gate"], p["w_up"], side_norm=norm_of(x1_b, p["norm2_w"], BF16, half))
    act_b, = ffn_up(h2_b, p["w_gate"], p["w_up"])
    x2_a, = ffn_down(act_a, p["w_down"], x1_a)
    if final_w is None:
        x2_b, = ffn_down(act_b, p["w_down"], x1_b)
        return jnp.concatenate([x2_a, x2_b], axis=0), None
    x2_b, y = ffn_down(act_b, p["w_down"], x1_b, side_norm=norm_of(x2_a, final_w, F32, m))
    return None, _rmsnorm_into(x2_b, final_w, y, half, 256)


def kernel(x_prompt, x_sample, cache_k, cache_v, state_conv, state_ssm, norm1_w, w_in, lambda_q1, lambda_k1, lambda_q2, lambda_k2, subln_w, conv_w, conv_b, dt_bias, A_log, D_skip, ssd_norm_w, w_out, norm2_w, w_gate, w_up, w_down, final_norm_w):
    depth = w_in.shape[0]
    bp, seq, d_model = x_prompt.shape
    bs, rows, _ = x_sample.shape
    heads = cache_k.shape[3]
    inner = w_out.shape[1] // 2
    attn_w = heads * V_HEAD_DIM
    conv_dim = conv_w.shape[2]
    ssd_heads = state_ssm.shape[2]
    assert seq % CHUNK == 0 and rows <= CHUNK and rows >= CONV_WIDTH - 1 and attn_w == inner

    xp = x_prompt.reshape(bp * seq, d_model)
    xs = x_sample.reshape(bs * rows, d_model)
    outs_p = [[], [], [], []]
    outs_s = [[], [], [], []]
    for layer in range(depth):
        p = _prepare_layer(layer, norm1_w, w_in, lambda_q1, lambda_k1, lambda_q2, lambda_k2, subln_w,
                           conv_w, conv_b, dt_bias, A_log, D_skip, ssd_norm_w, w_out, norm2_w,
                           w_gate, w_up, w_down, ff_tile=1024)
        lam0 = p["lam0"]

        q, k32, kbf, v32, vbf, zx, dt = _in_proj(xp, p)
        attn_tk = 512
        n_steps = bp * heads * (seq // (2 * attn_tk))
        late = p["late_weights"]
        ride_along = all(_side_block_rows(w.shape[0], rows_out, n_steps) is not None
                         and w.shape[1] % V7X_LANES == 0 for _, w, rows_out, _ in late)
        attn, casts = _attn_prompt(q, kbf, vbf, p["lams"], p["subln_w"],
                                   [(w, r, c) for _, w, r, c in late] if ride_along else [],
                                   batch=bp, seq=seq, heads=heads, tk=attn_tk, lam0=lam0)
        if ride_along:
            p.update({name: c for (name, _, _, _), c in zip(late, casts)})
        else:
            _cast_late_weights(p)
        ssd_y, st = _ssd(zx, dt, jnp.zeros((bp, V7X_SUBLANES, conv_dim), F32),
                         jnp.zeros((bp, inner, D_STATE), F32), p,
                         batch=bp, n_chunks=seq // CHUNK, valid=CHUNK)
        outs_p[0].append(k32.reshape(bp, seq, heads, V_HEAD_DIM))
        outs_p[1].append(v32.reshape(bp, seq, heads, V_HEAD_DIM))
        outs_p[2].append(zx.reshape(bp, seq, 3 * inner)[:, seq - (CONV_WIDTH - 1):, inner:])
        outs_p[3].append(st.reshape(bp, ssd_heads, SSD_HEAD_DIM, D_STATE))
        xp, y_prompt = _out_and_ffn(xp, attn, ssd_y, p, final_norm_w if layer == depth - 1 else None)

        q, k32, kbf, v32, vbf, zx, dt = _in_proj(xs, p)
        attn = _attn_decode(q, kbf, vbf, cache_k[layer], cache_v[layer],
                            p["lams"], p["subln_w"], batch=bs, rows=rows, heads=heads, lam0=lam0)
        pad_chunk = ((0, 0), (0, CHUNK - rows), (0, 0))
        zx_c = jnp.pad(zx.reshape(bs, rows, 3 * inner), pad_chunk).reshape(bs * CHUNK, 3 * inner)
        dt_c = jnp.pad(dt.reshape(bs, rows, V7X_LANES), pad_chunk).reshape(bs * CHUNK, V7X_LANES)
        tail = jnp.pad(state_conv[layer].astype(F32),
                       ((0, 0), (V7X_SUBLANES - (CONV_WIDTH - 1), 0), (0, 0)))
        ssd_y, st = _ssd(zx_c, dt_c, tail, state_ssm[layer].astype(F32).reshape(bs, inner, D_STATE), p,
                         batch=bs, n_chunks=1, valid=rows)
        ssd_y = ssd_y.reshape(bs, CHUNK, inner)[:, :rows].reshape(bs * rows, inner)
        outs_s[0].append(k32.reshape(bs, rows, heads, V_HEAD_DIM))
        outs_s[1].append(v32.reshape(bs, rows, heads, V_HEAD_DIM))
        outs_s[2].append(zx.reshape(bs, rows, 3 * inner)[:, rows - (CONV_WIDTH - 1):, inner:])
        outs_s[3].append(st.reshape(bs, ssd_heads, SSD_HEAD_DIM, D_STATE))
        xs, y_sample = _out_and_ffn(xs, attn, ssd_y, p, final_norm_w if layer == depth - 1 else None)

    y_prompt = y_prompt.reshape(bp, seq, d_model)
    y_sample = y_sample.reshape(bs, rows, d_model)
    return (y_prompt, y_sample, *(jnp.stack(o) for o in outs_p), *(jnp.stack(o) for o in outs_s))
```

```python
import functools
import math
from typing import Any, NamedTuple

import jax
import jax.numpy as jnp
import numpy as np
from jax import lax
from jax.experimental import pallas as pl
from jax.experimental.pallas import tpu as pltpu

F32 = jnp.float32
BF16 = jnp.bfloat16

CHUNK = 64
V_HEAD_DIM = 128
QK_DIM = V_HEAD_DIM // 2
ATTN_SCALE = QK_DIM ** -0.5
LOG2_E = math.log2(math.e)
SSD_HEAD_DIM = 64
SSD_GROUPS = 8
D_STATE = 128
CONV_WIDTH = 4
EPS = 1e-6

V7X_LANES = 128
V7X_SUBLANES = 8
BF16_SUBLANES = 2 * V7X_SUBLANES
V7X_VMEM_BYTES = 64 * 1024 * 1024
VMEM_LIMIT_BYTES = V7X_VMEM_BYTES - 8 * 1024 * 1024

NT_DIMS = (((1,), (1,)), ((), ()))


def _cparams(n_axes, flags=None):
    return pltpu.CompilerParams(
        dimension_semantics=("arbitrary",) * n_axes, vmem_limit_bytes=VMEM_LIMIT_BYTES, flags=flags)


def _silu(x):
    h = 0.5 * x
    return h + h * jnp.tanh(h)


def _lambda_init(layer):
    return 0.8 - 0.6 * math.exp(-0.3 * layer)


def _rmsnorm_body(x_ref, w_ref, o_ref):
    x = x_ref[...]
    ms = jnp.mean(x * x, axis=-1, keepdims=True)
    o_ref[...] = (x * lax.rsqrt(ms + EPS) * w_ref[...]).astype(o_ref.dtype)


def _rmsnorm(x, w, out_dtype, tr):
    m, d = x.shape
    return pl.pallas_call(
        _rmsnorm_body,
        grid=(m // tr,),
        in_specs=[pl.BlockSpec((tr, d), lambda i: (i, 0)),
                  pl.BlockSpec((1, d), lambda i: (0, 0))],
        out_specs=pl.BlockSpec((tr, d), lambda i: (i, 0)),
        out_shape=jax.ShapeDtypeStruct((m, d), out_dtype),
        compiler_params=_cparams(1),
        name="rmsnorm",
    )(x, w.reshape(1, d).astype(F32))


def _cast_cols_body(x_ref, o_ref):
    n_in = x_ref.shape[1]
    o_ref[:, :n_in] = x_ref[...].astype(o_ref.dtype)
    if o_ref.shape[1] > n_in:
        o_ref[:, n_in:] = jnp.zeros((o_ref.shape[0], o_ref.shape[1] - n_in), o_ref.dtype)


def _cast_cols(w, n_in, n_out, tr):
    rows = w.shape[0]
    assert rows % tr == 0 and n_in % V7X_LANES == 0 and n_out >= n_in
    return pl.pallas_call(
        _cast_cols_body,
        grid=(rows // tr,),
        in_specs=[pl.BlockSpec((tr, n_in), lambda i: (i, 0))],
        out_specs=pl.BlockSpec((tr, n_out), lambda i: (i, 0)),
        out_shape=jax.ShapeDtypeStruct((rows, n_out), BF16),
        compiler_params=_cparams(1),
        name="cast_cols",
    )(w)


def _cast_rows_body(x_ref, o_ref, *, n_blocks_in):
    keep = pl.program_id(0) < n_blocks_in
    o_ref[...] = jnp.where(keep, x_ref[...], 0.0).astype(o_ref.dtype)


def _cast_rows(w, rows_out, tr):
    rows, n = w.shape
    assert rows_out % tr == 0 and (rows_out <= rows or rows % tr == 0)
    n_blocks_in = rows // tr
    return pl.pallas_call(
        functools.partial(_cast_rows_body, n_blocks_in=n_blocks_in),
        grid=(rows_out // tr,),
        in_specs=[pl.BlockSpec((tr, n), lambda i: (jnp.minimum(i, n_blocks_in - 1), 0))],
        out_specs=pl.BlockSpec((tr, n), lambda i: (i, 0)),
        out_shape=jax.ShapeDtypeStruct((rows_out, n), BF16),
        compiler_params=_cparams(1),
        name="cast_rows",
    )(w)


class _Out(NamedTuple):
    dtype: Any
    scale: float = 1.0
    by_head: bool = False


class _SideNorm(NamedTuple):
    x: Any
    w: Any
    row_off: int
    rows: int
    out_dtype: Any
    out_total_rows: int
    out_row_off: int


def _side_block_rows(rows_in, rows_out, n_steps):
    br = BF16_SUBLANES
    while rows_out > br * n_steps:
        br *= 2
    return br if rows_in % br == 0 and rows_out % br == 0 else None


def _side_norm_specs(sn, n_steps, step_of):
    br = _side_block_rows(sn.rows, sn.rows, n_steps)
    assert br is not None and sn.row_off % br == 0 and sn.out_row_off % br == 0
    last = sn.rows // br - 1
    d = sn.x.shape[1]

    def block_at(off_blocks, *grid_idx):
        return (jnp.minimum(step_of(*grid_idx), last) + off_blocks, 0)

    in_specs = [pl.BlockSpec((br, d), functools.partial(block_at, sn.row_off // br)),
                pl.BlockSpec((1, d), lambda *grid_idx: (0, 0))]
    out_spec = pl.BlockSpec((br, d), functools.partial(block_at, sn.out_row_off // br))
    out_shape = jax.ShapeDtypeStruct((sn.out_total_rows, d), sn.out_dtype)
    return in_specs, out_spec, out_shape, [sn.x, sn.w.reshape(1, d).astype(F32)]


def _mm_body(*refs, n_pairs, scales, has_res, has_side, b_transposed):
    a_refs = refs[:n_pairs]
    b_refs = refs[n_pairs:2 * n_pairs]
    pos = 2 * n_pairs
    res_ref = refs[pos] if has_res else None
    pos += int(has_res)
    n_out = len(scales)
    if has_side:
        _rmsnorm_body(refs[pos], refs[pos + 1], refs[pos + 2 + n_out])
        pos += 2
    out_refs = refs[pos:pos + n_out]
    acc = None
    for a_ref, b_ref in zip(a_refs, b_refs):
        if b_transposed:
            d = lax.dot_general(a_ref[...], b_ref[...], NT_DIMS, preferred_element_type=F32)
        else:
            d = jnp.dot(a_ref[...], b_ref[...], preferred_element_type=F32)
        acc = d if acc is None else acc + d
    if has_res:
        acc = acc + res_ref[...]
    for o_ref, s in zip(out_refs, scales):
        val = (acc if s == 1.0 else acc * s).astype(o_ref.dtype)
        o_ref[...] = val.reshape(o_ref.shape)


def _matmul(a_list, b_list, *, n, tm, tn, outs, b_row_blocks=None, b_col_off=0, residual=None,
            b_transposed=False, row_off=0, rows=None, side_norm=None, name="matmul"):
    m = a_list[0].shape[0] - row_off if rows is None else rows
    n_pairs = len(a_list)
    if b_row_blocks is None:
        b_row_blocks = [0] * n_pairs
    assert b_col_off % tn == 0 and n % tn == 0 and m % tm == 0 and row_off % tm == 0
    col_blk = b_col_off // tn
    row_blk = row_off // tm
    n_j = n // tn
    in_specs = []
    for a in a_list:
        in_specs.append(pl.BlockSpec((tm, a.shape[1]), lambda i, j: (row_blk + i, 0)))
    for a, rb in zip(a_list, b_row_blocks):
        if b_transposed:
            in_specs.append(pl.BlockSpec((tn, a.shape[1]), lambda i, j, rb=rb: (col_blk + j, rb)))
        else:
            in_specs.append(pl.BlockSpec((a.shape[1], tn), lambda i, j, rb=rb: (rb, col_blk + j)))
    args = list(a_list) + list(b_list)
    if residual is not None:
        in_specs.append(pl.BlockSpec((tm, tn), lambda i, j: (row_blk + i, j)))
        args.append(residual)
    out_shape, out_specs = [], []
    for dt, _, by_head in outs:
        if by_head:
            out_shape.append(jax.ShapeDtypeStruct((m, n // V_HEAD_DIM, V_HEAD_DIM), dt))
            out_specs.append(pl.BlockSpec((tm, tn // V_HEAD_DIM, V_HEAD_DIM), lambda i, j: (i, j, 0)))
        else:
            out_shape.append(jax.ShapeDtypeStruct((m, n), dt))
            out_specs.append(pl.BlockSpec((tm, tn), lambda i, j: (i, j)))
    if side_norm is not None:
        s_in, s_out, s_shape, s_args = _side_norm_specs(
            side_norm, (m // tm) * n_j, lambda i, j: i * n_j + j)
        in_specs += s_in
        args += s_args
        out_specs.append(s_out)
        out_shape.append(s_shape)
    res = pl.pallas_call(
        functools.partial(_mm_body, n_pairs=n_pairs, scales=tuple(o.scale for o in outs),
                          has_res=residual is not None, has_side=side_norm is not None,
                          b_transposed=b_transposed),
        grid=(m // tm, n_j),
        in_specs=in_specs,
        out_specs=out_specs,
        out_shape=out_shape,
        compiler_params=_cparams(2),
        name=name,
    )(*args)
    return res


def _ffn_up_body(h_ref, wg_ref, wu_ref, *refs):
    if len(refs) > 1:
        side_x, side_w, o_ref, side_o = refs
        _rmsnorm_body(side_x, side_w, side_o)
    else:
        o_ref, = refs
    h = h_ref[...]
    g = jnp.dot(h, wg_ref[...], preferred_element_type=F32)
    u = jnp.dot(h, wu_ref[...], preferred_element_type=F32)
    o_ref[...] = (_silu(g) * u).astype(o_ref.dtype)


def _ffn_up(h, wg, wu, *, tm, tn, side_norm=None):
    m, d = h.shape
    n = wg.shape[1]
    n_j = n // tn
    in_specs = [pl.BlockSpec((tm, d), lambda i, j: (i, 0)),
                pl.BlockSpec((d, tn), lambda i, j: (0, j)),
                pl.BlockSpec((d, tn), lambda i, j: (0, j))]
    args = [h, wg, wu]
    out_specs = [pl.BlockSpec((tm, tn), lambda i, j: (i, j))]
    out_shape = [jax.ShapeDtypeStruct((m, n), BF16)]
    if side_norm is not None:
        s_in, s_out, s_shape, s_args = _side_norm_specs(
            side_norm, (m // tm) * n_j, lambda i, j: i * n_j + j)
        in_specs += s_in
        args += s_args
        out_specs.append(s_out)
        out_shape.append(s_shape)
    return pl.pallas_call(
        _ffn_up_body,
        grid=(m // tm, n_j),
        in_specs=in_specs,
        out_specs=out_specs,
        out_shape=out_shape,
        compiler_params=_cparams(2),
        name="ffn_up",
    )(*args)


def _mm_acc_body(a_ref, b_ref, r_ref, *refs):
    o_ref = refs[-2] if len(refs) > 1 else refs[0]

    def side_job():
        if len(refs) > 1:
            side_x, side_w, _, side_o = refs
            _rmsnorm_body(side_x, side_w, side_o)

    @pl.when(pl.program_id(2) == 0)
    def _first_step():
        side_job()
        o_ref[...] = r_ref[...] + jnp.dot(a_ref[...], b_ref[...], preferred_element_type=F32)

    @pl.when(pl.program_id(2) > 0)
    def _later_steps():
        side_job()
        o_ref[...] += jnp.dot(a_ref[...], b_ref[...], preferred_element_type=F32)


def _matmul_acc(a, b, residual, *, tm, tn, tk, side_norm=None):
    m, kdim = a.shape
    n = b.shape[1]
    assert m % tm == 0 and n % tn == 0 and kdim % tk == 0
    n_j, n_k = n // tn, kdim // tk
    in_specs = [pl.BlockSpec((tm, tk), lambda i, j, k: (i, k)),
                pl.BlockSpec((tk, tn), lambda i, j, k: (k, j)),
                pl.BlockSpec((tm, tn), lambda i, j, k: (i, j))]
    args = [a, b, residual]
    out_specs = [pl.BlockSpec((tm, tn), lambda i, j, k: (i, j))]
    out_shape = [jax.ShapeDtypeStruct((m, n), F32)]
    if side_norm is not None:
        s_in, s_out, s_shape, s_args = _side_norm_specs(
            side_norm, (m // tm) * n_j * n_k, lambda i, j, k: (i * n_j + j) * n_k + k)
        in_specs += s_in
        args += s_args
        out_specs.append(s_out)
        out_shape.append(s_shape)
    return pl.pallas_call(
        _mm_acc_body,
        grid=(m // tm, n_j, n_k),
        in_specs=in_specs,
        out_specs=out_specs,
        out_shape=out_shape,
        compiler_params=_cparams(3),
        name="ffn_down",
    )(*args)


def _rmsnorm_rows_body(x_ref, w_ref, dst_ref, o_ref):
    del dst_ref
    _rmsnorm_body(x_ref, w_ref, o_ref)


def _rmsnorm_into(x, w, dst, row_off, tr):
    m, d = x.shape
    assert m % tr == 0 and row_off % tr == 0
    return pl.pallas_call(
        _rmsnorm_rows_body,
        grid=(m // tr,),
        in_specs=[pl.BlockSpec((tr, d), lambda i: (i, 0)),
                  pl.BlockSpec((1, d), lambda i: (0, 0)),
                  pl.BlockSpec(memory_space=pl.ANY)],
        out_specs=pl.BlockSpec((tr, d), lambda i: (row_off // tr + i, 0)),
        out_shape=jax.ShapeDtypeStruct(dst.shape, dst.dtype),
        input_output_aliases={2: 0},
        compiler_params=_cparams(1),
        name="rmsnorm_rows",
    )(x, w.reshape(1, d).astype(F32), dst)


def _lambda_value(lq1, lk1, lq2, lk2, lam0):
    d1 = jnp.sum(lq1[...] * lk1[...], axis=-1, keepdims=True)
    d2 = jnp.sum(lq2[...] * lk2[...], axis=-1, keepdims=True)
    return jnp.exp(d1) - jnp.exp(d2) + lam0


def _subln(o, w, lam0):
    ms = jnp.mean(o * o, axis=-1, keepdims=True)
    return (o * lax.rsqrt(ms + EPS) * w) * (1.0 - lam0)


def _side_cast(in_ref, out_ref, step, n_in_blocks, n_out_blocks):
    x = in_ref[...]
    if n_out_blocks > n_in_blocks:
        x = jnp.where(jnp.minimum(step, n_out_blocks - 1) < n_in_blocks, x, 0.0)
    n_in = in_ref.shape[1]
    out_ref[:, :n_in] = x.astype(out_ref.dtype)
    if out_ref.shape[1] > n_in:
        out_ref[:, n_in:] = jnp.zeros((out_ref.shape[0], out_ref.shape[1] - n_in), out_ref.dtype)


def _attn_prompt_body(*refs, side_blocks, tq, tk, lam0):
    n_side = len(side_blocks)
    lq1, lk1, lq2, lk2, w_ref, q_ref, k_ref, v_ref = refs[:8]
    side_in = refs[8:8 + n_side]
    o_ref = refs[8 + n_side]
    side_out = refs[9 + n_side:9 + 2 * n_side]
    (k1_scr, k2_scr, vt_scr, sa, sb, pa, pb, cma, cmb, ala, alb, m_scr,
     acc_scr) = refs[9 + 2 * n_side:]
    qi = pl.program_id(2)
    step = (pl.program_id(0) * pl.num_programs(1) + pl.program_id(1)) * pl.num_programs(2) + qi
    n_kv = k1_scr.shape[0]
    k_scrs = (k1_scr, k2_scr)

    @pl.when(qi == 0)
    def _prepare_head():
        first_half = lax.broadcasted_iota(jnp.int32, (tk, V_HEAD_DIM), 1) < QK_DIM
        for c in range(n_kv):
            kb = k_ref[pl.ds(c * tk, tk), :]
            zero = jnp.zeros_like(kb)
            k1_scr[c] = jnp.where(first_half, kb, zero)
            k2_scr[c] = jnp.where(first_half, zero, kb)
            vt_scr[c, :V_HEAD_DIM, :] = v_ref[pl.ds(c * tk, tk), :].T
            ones_row = lax.broadcasted_iota(jnp.int32, (BF16_SUBLANES, tk), 0) == 0
            vt_scr[c, V_HEAD_DIM:, :] = jnp.where(ones_row, 1.0, 0.0).astype(BF16)
        vt_scr[n_kv] = jnp.zeros(vt_scr.shape[1:], BF16)

    m_scr[...] = jnp.full(m_scr.shape, -jnp.inf, F32)
    acc_scr[...] = jnp.zeros(acc_scr.shape, F32)
    @pl.when(step == 0)
    def _define_pb():
        pb[...] = jnp.zeros(pb.shape, BF16)

    alb[...] = jnp.ones(alb.shape, F32)
    for in_ref, out_ref, (n_in_blocks, n_out_blocks) in zip(side_in, side_out, side_blocks):
        _side_cast(in_ref, out_ref, step, n_in_blocks, n_out_blocks)

    def scores(j, s_scr, cm_scr, c0):
        qv = q_ref[c0:, :]
        for idx in range(2):
            s = lax.dot_general(k_scrs[idx][j], qv, NT_DIMS, preferred_element_type=F32)
            s_scr[idx, :, c0:] = s
            cm_scr[idx, :, c0:] = jnp.max(s, axis=0, keepdims=True)

    def softmax(s_scr, cm_scr, p_scr, al_scr, c0, mask, c1=tq):
        for idx in range(2):
            s = s_scr[idx, :, c0:c1]
            if mask is None:
                cmax = cm_scr[idx, :, c0:c1]
            else:
                s = jnp.where(mask, s, -jnp.inf)
                cmax = jnp.max(s, axis=0, keepdims=True)
            m_old = m_scr[idx, :, c0:c1]
            m_new = jnp.maximum(m_old, cmax)
            alpha = jnp.exp2(m_old - m_new)
            p = jnp.exp2(s - m_new)
            p_scr[idx, :, c0:c1] = p.astype(BF16)
            al_scr[idx, :, c0:c1] = alpha
            m_scr[idx, :, c0:c1] = m_new

    def value_product(j, p_scr, al_scr, c0):
        vt = vt_scr[j]
        for idx in range(2):
            acc_scr[idx, :, c0:] = al_scr[idx, :, c0:] * acc_scr[idx, :, c0:] + jnp.dot(
                vt, p_scr[idx, :, c0:], preferred_element_type=F32)

    scores(0, sa, cma, 0)

    def pair(jj, carry):
        j = 2 * jj
        scores(j + 1, sb, cmb, 0)
        value_product(jnp.where(j == 0, n_kv, j - 1), pb, alb, 0)
        softmax(sa, cma, pa, ala, 0, None)
        scores(j + 2, sa, cma, 0)
        value_product(j, pa, ala, 0)
        softmax(sb, cmb, pb, alb, 0, None)
        return carry

    lax.fori_loop(0, qi, pair, 0)

    j0 = 2 * qi
    key_chunk = lax.broadcasted_iota(jnp.int32, (tk, tk), 0) // CHUNK
    qry_chunk = lax.broadcasted_iota(jnp.int32, (tk, tk), 1) // CHUNK
    visible = key_chunk <= qry_chunk
    scores(j0 + 1, sb, cmb, tk)
    value_product(jnp.where(j0 == 0, n_kv, j0 - 1), pb, alb, 0)
    softmax(sa, cma, pa, ala, 0, visible, c1=tk)
    softmax(sa, cma, pa, ala, tk, None)
    softmax(sb, cmb, pb, alb, tk, visible)
    value_product(j0, pa, ala, 0)
    value_product(j0 + 1, pb, alb, tk)

    lam = _lambda_value(lq1, lk1, lq2, lk2, lam0)
    outs = [acc_scr[idx, :V_HEAD_DIM, :] * (1.0 / acc_scr[idx, V_HEAD_DIM:V_HEAD_DIM + 1, :])
            for idx in range(2)]
    o_t = outs[0] - lam * outs[1]
    ms = jnp.mean(o_t * o_t, axis=0, keepdims=True)
    o_t = (o_t * lax.rsqrt(ms + EPS) * w_ref[...]) * (1.0 - lam0)
    o_ref[...] = o_t.astype(o_ref.dtype).T


def _attn_prompt(q, k, v, lams, subln_w, side, *, batch, seq, heads, tk, lam0):
    tq = 2 * tk
    nq = seq // tq
    n_kv = seq // tk
    n_steps = batch * heads * nq
    lam_specs = [pl.BlockSpec((1, QK_DIM), lambda b, h, i: (0, 0))] * 4
    side_in_specs, side_out_specs, side_shapes, side_blocks = [], [], [], []
    for w, rows_out, cols_out in side:
        br = _side_block_rows(w.shape[0], rows_out, n_steps)
        assert br is not None and w.shape[1] % V7X_LANES == 0
        n_in_blocks, n_out_blocks = w.shape[0] // br, rows_out // br

        def block_of(b, h, i, last):
            return (jnp.minimum((b * heads + h) * nq + i, last), 0)

        side_in_specs.append(pl.BlockSpec(
            (br, w.shape[1]), functools.partial(block_of, last=n_in_blocks - 1)))
        side_out_specs.append(pl.BlockSpec(
            (br, cols_out), functools.partial(block_of, last=n_out_blocks - 1)))
        side_shapes.append(jax.ShapeDtypeStruct((rows_out, cols_out), BF16))
        side_blocks.append((n_in_blocks, n_out_blocks))
    out = pl.pallas_call(
        functools.partial(_attn_prompt_body, side_blocks=tuple(side_blocks), tq=tq, tk=tk, lam0=lam0),
        grid=(batch, heads, nq),
        in_specs=lam_specs + [
            pl.BlockSpec((V_HEAD_DIM, 1), lambda b, h, i: (0, 0)),
            pl.BlockSpec((tq, V_HEAD_DIM), lambda b, h, i: (b * nq + i, h)),
            pl.BlockSpec((seq, V_HEAD_DIM), lambda b, h, i: (b, h)),
            pl.BlockSpec((seq, V_HEAD_DIM), lambda b, h, i: (b, h)),
        ] + side_in_specs,
        out_specs=[pl.BlockSpec((tq, V_HEAD_DIM), lambda b, h, i: (b * nq + i, h))] + side_out_specs,
        out_shape=[jax.ShapeDtypeStruct(q.shape, BF16)] + side_shapes,
        scratch_shapes=[
            pltpu.VMEM((n_kv, tk, V_HEAD_DIM), BF16),
            pltpu.VMEM((n_kv, tk, V_HEAD_DIM), BF16),
            pltpu.VMEM((n_kv + 1, V_HEAD_DIM + BF16_SUBLANES, tk), BF16),
            pltpu.VMEM((2, tk, tq), F32), pltpu.VMEM((2, tk, tq), F32),
            pltpu.VMEM((2, tk, tq), BF16), pltpu.VMEM((2, tk, tq), BF16),
            pltpu.VMEM((2, 1, tq), F32), pltpu.VMEM((2, 1, tq), F32),
            pltpu.VMEM((2, 1, tq), F32), pltpu.VMEM((2, 1, tq), F32),
            pltpu.VMEM((2, 1, tq), F32),
            pltpu.VMEM((2, V_HEAD_DIM + BF16_SUBLANES, tq), F32),
        ],
        compiler_params=_cparams(3),
        name="attn_prompt",
    )(*lams, subln_w.reshape(V_HEAD_DIM, 1), q, k, v, *(w for w, _, _ in side))
    return out[0], out[1:]


def _attn_decode_body(lq1, lk1, lq2, lk2, w_ref, q_ref, kn_ref, vn_ref, kc_ref, vc_ref, o_ref,
                      m_scr, l_scr, acc_scr, *, rows, lam0):
    kb = pl.program_id(2)
    group = kc_ref.shape[1]
    n_q = group * rows

    def stack_heads(ref):
        x = ref[...]
        return jnp.concatenate(
            [x[:, h * V_HEAD_DIM:(h + 1) * V_HEAD_DIM] for h in range(group)], axis=0)

    q = stack_heads(q_ref)
    first_half = lax.broadcasted_iota(jnp.int32, q.shape, 1) < QK_DIM
    zero = jnp.zeros_like(q)
    q_both = jnp.concatenate([jnp.where(first_half, q, zero), jnp.where(first_half, zero, q)], axis=0)

    def accumulate(k2d, v2d, same_head):
        s = lax.dot_general(q_both, k2d, NT_DIMS, preferred_element_type=F32)
        s = jnp.where(jnp.concatenate([same_head, same_head], axis=0), s, -jnp.inf)
        m_old = m_scr[...]
        m_new = jnp.maximum(m_old, jnp.max(s, axis=-1, keepdims=True))
        alpha = jnp.exp2(m_old - m_new)
        p = jnp.exp2(s - m_new)
        l_scr[...] = alpha * l_scr[...] + jnp.sum(p, axis=-1, keepdims=True)
        acc_scr[...] = alpha * acc_scr[...] + jnp.dot(p.astype(BF16), v2d, preferred_element_type=F32)
        m_scr[...] = m_new

    @pl.when(kb == 0)
    def _new_rows():
        m_scr[...] = jnp.full(m_scr.shape, -jnp.inf, F32)
        l_scr[...] = jnp.zeros(l_scr.shape, F32)
        acc_scr[...] = jnp.zeros(acc_scr.shape, F32)
        q_head = lax.broadcasted_iota(jnp.int32, (n_q, n_q), 0) // rows
        k_head = lax.broadcasted_iota(jnp.int32, (n_q, n_q), 1) // rows
        accumulate(stack_heads(kn_ref), stack_heads(vn_ref), q_head == k_head)

    n_keys = kc_ref.shape[0] * group
    q_head = lax.broadcasted_iota(jnp.int32, (n_q, n_keys), 0) // rows
    k_head = lax.broadcasted_iota(jnp.int32, (n_q, n_keys), 1) % group
    accumulate(kc_ref[...].reshape(n_keys, V_HEAD_DIM).astype(BF16),
               vc_ref[...].reshape(n_keys, V_HEAD_DIM).astype(BF16), q_head == k_head)

    @pl.when(kb == pl.num_programs(2) - 1)
    def _finish():
        lam = _lambda_value(lq1, lk1, lq2, lk2, lam0)
        o = acc_scr[...] * (1.0 / l_scr[...])
        o = o[:n_q] - lam * o[n_q:]
        o = _subln(o, w_ref[...], lam0).astype(o_ref.dtype)
        for h in range(group):
            o_ref[:, h * V_HEAD_DIM:(h + 1) * V_HEAD_DIM] = o[h * rows:(h + 1) * rows, :]


def _attn_decode(q, k_new, v_new, cache_k, cache_v, lams, subln_w, *, batch, rows, heads, lam0):
    past = cache_k.shape[1]
    group = V7X_SUBLANES
    t_kv = min(past, 512)
    assert heads % group == 0 and past % t_kv == 0
    gw = group * V_HEAD_DIM
    lam_specs = [pl.BlockSpec((1, QK_DIM), lambda b, g, k: (0, 0))] * 4
    new_spec = pl.BlockSpec((rows, gw), lambda b, g, k: (b, g))
    cache_spec = pl.BlockSpec((None, t_kv, group, V_HEAD_DIM), lambda b, g, k: (b, k, g, 0))
    return pl.pallas_call(
        functools.partial(_attn_decode_body, rows=rows, lam0=lam0),
        grid=(batch, heads // group, past // t_kv),
        in_specs=lam_specs + [
            pl.BlockSpec((1, V_HEAD_DIM), lambda b, g, k: (0, 0)),
            new_spec, new_spec, new_spec, cache_spec, cache_spec,
        ],
        out_specs=new_spec,
        out_shape=jax.ShapeDtypeStruct(q.shape, BF16),
        scratch_shapes=[
            pltpu.VMEM((2 * group * rows, 1), F32),
            pltpu.VMEM((2 * group * rows, 1), F32),
            pltpu.VMEM((2 * group * rows, V_HEAD_DIM), F32),
        ],
        compiler_params=_cparams(3),
        name="attn_decode",
    )(*lams, subln_w, q, k_new, v_new, cache_k, cache_v)


def _split3(x):
    hi = x.astype(BF16)
    r1 = x - hi.astype(F32)
    mid = r1.astype(BF16)
    lo = (r1 - mid.astype(F32)).astype(BF16)
    return hi, mid, lo


def _ssd_body(z_ref, x_ref, bc_ref, dt_ref, tail_ref, st0_ref, cw_ref, cb_ref, dtb_ref, alog_ref,
              dsk_ref, nw_ref, e_ref, tril_ref, y_ref, stout_ref, xpad_scr, st_scr,
              *, valid, heads_per_group):
    c = pl.program_id(1)
    L = CHUNK
    inner = z_ref.shape[1]
    gw = heads_per_group * SSD_HEAD_DIM
    n_groups = inner // gw
    gn = D_STATE
    tail_rows = V7X_SUBLANES

    @pl.when(c == 0)
    def _load_stream_state():
        xpad_scr[0:tail_rows, :] = tail_ref[...]
        st_scr[...] = st0_ref[...].T

    xpad_scr[tail_rows:tail_rows + L, 0:inner] = x_ref[...]
    xpad_scr[tail_rows:tail_rows + L, inner:] = bc_ref[...]
    conv = cb_ref[...]
    first = tail_rows - (CONV_WIDTH - 1)
    for i in range(CONV_WIDTH):
        conv = conv + cw_ref[i:i + 1, :] * xpad_scr[first + i:first + i + L, :]
    xpad_scr[0:tail_rows, :] = xpad_scr[L:L + tail_rows, :]
    act = _silu(conv)
    xs = act[:, :inner]
    bm = act[:, inner:inner + n_groups * gn]
    cm = act[:, inner + n_groups * gn:]

    dtv = dt_ref[...] + dtb_ref[...]
    dt = jnp.maximum(dtv, 0.0) + jnp.log1p(jnp.exp(-jnp.abs(dtv)))
    if valid < L:
        dt = jnp.where(lax.broadcasted_iota(jnp.int32, dt.shape, 0) < valid, dt, 0.0)
    d_a = dt * (-jnp.exp(alog_ref[...]))

    pieces = jnp.concatenate(_split3(d_a), axis=1)
    cs = jnp.dot(tril_ref[...], pieces, preferred_element_type=F32)
    w = d_a.shape[1]
    a_cs = (cs[:, :w] + cs[:, w:2 * w]) + cs[:, 2 * w:]

    both = jnp.concatenate([a_cs, dt], axis=0)
    pieces = jnp.concatenate(_split3(both), axis=0)
    ex = jnp.dot(pieces, e_ref[...], preferred_element_type=F32)
    ex = (ex[:2 * L] + ex[2 * L:4 * L]) + ex[4 * L:]
    acol = ex[:L]
    dtx = ex[L:]

    row = lax.broadcasted_iota(jnp.int32, (L, inner), 0)
    sub = lax.broadcasted_iota(jnp.int32, (L, inner), 1) & (SSD_HEAD_DIM - 1)
    arow = jnp.sum(jnp.where(row == sub, acol, 0.0), axis=0, keepdims=True)
    decay_in = jnp.exp(jnp.where(sub <= row, acol - arow, -jnp.inf))
    a_last = acol[L - 1:L, :]
    decay_to_end = jnp.exp(a_last - acol)
    decay_from_start = jnp.exp(acol)
    chunk_decay = jnp.exp(a_last)

    xdt = xs * dtx
    xdt_bf = xdt.astype(BF16)
    xd_end = (xdt * decay_to_end).astype(BF16)
    bm_t = jnp.concatenate([bm, jnp.zeros_like(bm)], axis=0).T
    zeros_rows = jnp.zeros((L, gw), BF16)
    lane_head = lax.broadcasted_iota(jnp.int32, (L, gw), 1) // SSD_HEAD_DIM

    for g in range(n_groups):
        hs = slice(g * gw, (g + 1) * gw)
        ns = slice(g * gn, (g + 1) * gn)
        b_bf = bm[:, ns].astype(BF16)
        c_bf = cm[:, ns].astype(BF16)
        cb = lax.dot_general(c_bf, jnp.concatenate([b_bf] * heads_per_group, axis=0), NT_DIMS,
                             preferred_element_type=F32)
        lhs = (cb * decay_in[:, hs]).astype(BF16)
        xg = xdt_bf[:, hs]
        zero = jnp.zeros_like(xg)
        block_diag = jnp.concatenate(
            [jnp.where(lane_head == r, xg, zero) for r in range(heads_per_group)], axis=0)
        y = jnp.dot(lhs, block_diag, preferred_element_type=F32)
        st = st_scr[:, hs]
        y = y + jnp.dot(c_bf, st.astype(BF16), preferred_element_type=F32) * decay_from_start[:, hs]
        bt = bm_t[ns, :].astype(BF16)
        st_scr[:, hs] = chunk_decay[:, hs] * st + jnp.dot(
            bt, jnp.concatenate([xd_end[:, hs], zeros_rows], axis=0), preferred_element_type=F32)
        y = y + dsk_ref[:, hs] * xs[:, hs]
        y = y * _silu(z_ref[:, hs])
        y = y * lax.rsqrt(jnp.mean(y * y, axis=-1, keepdims=True) + EPS)
        y_ref[:, hs] = (y * nw_ref[:, hs]).astype(y_ref.dtype)

    @pl.when(c == pl.num_programs(1) - 1)
    def _store_stream_state():
        stout_ref[...] = st_scr[...].T


def _ssd(zx, dt_raw, tail, st0, p, *, batch, n_chunks, valid):
    rows = zx.shape[0]
    inner = zx.shape[1] // 3
    conv_dim = 2 * inner
    heads = inner // SSD_HEAD_DIM
    L = CHUNK
    assert SSD_HEAD_DIM == CHUNK and conv_dim == inner + 2 * SSD_GROUPS * D_STATE
    expand = np.zeros((V7X_LANES, inner), np.float32)
    for h in range(heads):
        expand[h, h * SSD_HEAD_DIM:(h + 1) * SSD_HEAD_DIM] = 1.0
    tril = np.tril(np.ones((L, L), np.float32))

    def row_blk(col):
        return pl.BlockSpec((L, inner), lambda b, c, col=col: (b * n_chunks + c, col))

    def const(shape):
        return pl.BlockSpec(shape, lambda b, c: (0,) * len(shape))

    return pl.pallas_call(
        functools.partial(_ssd_body, valid=valid, heads_per_group=heads // SSD_GROUPS),
        grid=(batch, n_chunks),
        in_specs=[
            row_blk(0), row_blk(1), row_blk(2),
            pl.BlockSpec((L, V7X_LANES), lambda b, c: (b * n_chunks + c, 0)),
            pl.BlockSpec((None, V7X_SUBLANES, conv_dim), lambda b, c: (b, 0, 0)),
            pl.BlockSpec((None, inner, D_STATE), lambda b, c: (b, 0, 0)),
            const((CONV_WIDTH, conv_dim)), const((1, conv_dim)),
            const((1, V7X_LANES)), const((1, V7X_LANES)),
            const((1, inner)), const((1, inner)),
            const((V7X_LANES, inner)), const((L, L)),
        ],
        out_specs=[
            pl.BlockSpec((L, inner), lambda b, c: (b * n_chunks + c, 0)),
            pl.BlockSpec((None, inner, D_STATE), lambda b, c: (b, 0, 0)),
        ],
        out_shape=[
            jax.ShapeDtypeStruct((rows, inner), BF16),
            jax.ShapeDtypeStruct((batch, inner, D_STATE), F32),
        ],
        scratch_shapes=[
            pltpu.VMEM((L + V7X_SUBLANES, conv_dim), F32),
            pltpu.VMEM((D_STATE, inner), F32),
        ],
        compiler_params=_cparams(2),
        name="ssd",
    )(zx, zx, zx, dt_raw, tail, st0, p["conv_w"], p["conv_b"], p["dt_bias"], p["a_log"],
      p["d_skip"], p["ssd_norm_w"], jnp.asarray(expand, BF16), jnp.asarray(tril, BF16))


def _pad_lanes(v, width):
    return jnp.pad(v.astype(F32), (0, width - v.shape[0])).reshape(1, width)


def _prepare_layer(layer, norm1_w, w_in, lambda_q1, lambda_k1, lambda_q2, lambda_k2, subln_w,
                   conv_w, conv_b, dt_bias, A_log, D_skip, ssd_norm_w, w_out, norm2_w,
                   w_gate, w_up, w_down, ff_tile):
    d_model = w_in.shape[1]
    attn_w = 3 * (w_out.shape[1] // 2)
    inner = w_out.shape[1] // 2
    main_cols = attn_w + inner + 2 * inner
    cast_rows = 128
    w_in_t = jnp.swapaxes(w_in[layer], 0, 1)
    w_in_bf = _cast_rows(w_in_t, main_cols, 2 * cast_rows)
    n_heads = w_in.shape[2] - main_cols
    w_dt = jnp.pad(w_in_t[main_cols:, :].astype(BF16), ((0, V7X_LANES - n_heads), (0, 0)))
    d_ff = w_gate.shape[2]
    ff_width = d_ff + (-d_ff % ff_tile)
    return {
        "lam0": _lambda_init(layer),
        "norm1_w": norm1_w[layer], "norm2_w": norm2_w[layer],
        "w_in": w_in_bf, "w_dt": w_dt, "main_cols": main_cols,
        "lams": [v[layer].reshape(1, -1).astype(F32)
                 for v in (lambda_q1, lambda_k1, lambda_q2, lambda_k2)],
        "subln_w": subln_w[layer].reshape(1, -1).astype(F32),
        "conv_w": conv_w[layer].astype(F32), "conv_b": conv_b[layer].reshape(1, -1).astype(F32),
        "dt_bias": _pad_lanes(dt_bias[layer], V7X_LANES),
        "a_log": _pad_lanes(A_log[layer], V7X_LANES),
        "d_skip": jnp.repeat(D_skip[layer].astype(F32), SSD_HEAD_DIM).reshape(1, -1),
        "ssd_norm_w": ssd_norm_w[layer].reshape(1, -1).astype(F32),
        "late_weights": [("w_out", w_out[layer], 2 * inner, d_model),
                         ("w_gate", w_gate[layer], d_model, ff_width),
                         ("w_up", w_up[layer], d_model, ff_width),
                         ("w_down", w_down[layer], ff_width, d_model)],
        "d_model": d_model, "inner": inner,
    }


def _cast_late_weights(p):
    cast_rows = 128
    for name, w, rows_out, cols_out in p["late_weights"]:
        if rows_out == w.shape[0]:
            p[name] = _cast_cols(w, w.shape[1], cols_out, cast_rows)
        else:
            p[name] = _cast_rows(w, rows_out, 2 * cast_rows)


def _tiles(m):
    return 1024 if m % 1024 == 0 else m


def _in_proj(x2d, p):
    m = x2d.shape[0]
    inner = p["inner"]
    tm = _tiles(m)
    h = _rmsnorm(x2d, p["norm1_w"], BF16, min(256, m))
    proj = functools.partial(_matmul, [h], tm=tm, b_transposed=True)
    q, = proj([p["w_in"]], n=inner, tn=1024, b_col_off=0,
              outs=[_Out(BF16, ATTN_SCALE * LOG2_E)], name="proj_q")
    k32, kbf = proj([p["w_in"]], n=inner, tn=1024, b_col_off=inner,
                    outs=[_Out(F32, by_head=True), _Out(BF16)], name="proj_k")
    v32, vbf = proj([p["w_in"]], n=inner, tn=1024, b_col_off=2 * inner,
                    outs=[_Out(F32, by_head=True), _Out(BF16)], name="proj_v")
    zx, = proj([p["w_in"]], n=3 * inner, tn=1024, b_col_off=3 * inner,
               outs=[_Out(F32)], name="proj_zx")
    dt, = proj([p["w_dt"]], n=V7X_LANES, tn=V7X_LANES, outs=[_Out(F32)], name="proj_dt")
    return q, k32, kbf, v32, vbf, zx, dt


def _out_and_ffn(x2d, attn, ssd_y, p, final_w):
    m = x2d.shape[0]
    d_model = p["d_model"]
    tm = _tiles(m)
    out_proj = functools.partial(
        _matmul, [attn, ssd_y], [p["w_out"], p["w_out"]], b_row_blocks=[0, 1], n=d_model,
        tm=tm, outs=[_Out(F32)], residual=x2d, name="out_proj")
    ffn_up = functools.partial(_ffn_up, tm=tm, tn=512 if tm == 1024 else 1024)
    ffn_down = functools.partial(_matmul_acc, tm=tm, tn=min(1024, d_model), tk=p["w_down"].shape[0] // 4)
    half = m // 2
    if m % (2 * tm) != 0:
        x1, = out_proj(tn=min(1024, d_model))
        h2 = _rmsnorm(x1, p["norm2_w"], BF16, min(256, m))
        x2, = ffn_down(ffn_up(h2, p["w_gate"], p["w_up"])[0], p["w_down"], x1)
        y = None if final_w is None else _rmsnorm(x2, final_w, F32, min(256, m))
        return x2, y

    def norm_of(x, w, dtype, total_rows):
        return _SideNorm(x, w, 0, half, dtype, total_rows, 0)

    tn_out = min(512, d_model)
    x1_a, = out_proj(tn=min(1024, d_model), row_off=0, rows=half)
    x1_b, h2_a = out_proj(tn=tn_out, row_off=half, rows=half,
                          side_norm=norm_of(x1_a, p["norm2_w"], BF16, half))
    act_a, h2_b = ffn_up(h2_a, p["w_gate"], p["w_up"], side_norm=norm_of(x1_b, p["norm2_w"], BF16, half))
    act_b, = ffn_up(h2_b, p["w_gate"], p["w_up"])
    x2_a, = ffn_down(act_a, p["w_down"], x1_a)
    if final_w is None:
        x2_b, = ffn_down(act_b, p["w_down"], x1_b)
        return jnp.concatenate([x2_a, x2_b], axis=0), None
    x2_b, y = ffn_down(act_b, p["w_down"], x1_b, side_norm=norm_of(x2_a, final_w, F32, m))
    return None, _rmsnorm_into(x2_b, final_w, y, half, 256)


def kernel(x_prompt, x_sample, cache_k, cache_v, state_conv, state_ssm, norm1_w, w_in, lambda_q1, lambda_k1, lambda_q2, lambda_k2, subln_w, conv_w, conv_b, dt_bias, A_log, D_skip, ssd_norm_w, w_out, norm2_w, w_gate, w_up, w_down, final_norm_w):
    depth = w_in.shape[0]
    bp, seq, d_model = x_prompt.shape
    bs, rows, _ = x_sample.shape
    heads = cache_k.shape[3]
    inner = w_out.shape[1] // 2
    attn_w = heads * V_HEAD_DIM
    conv_dim = conv_w.shape[2]
    ssd_heads = state_ssm.shape[2]
    assert seq % CHUNK == 0 and rows <= CHUNK and rows >= CONV_WIDTH - 1 and attn_w == inner

    xp = x_prompt.reshape(bp * seq, d_model)
    xs = x_sample.reshape(bs * rows, d_model)
    outs_p = [[], [], [], []]
    outs_s = [[], [], [], []]
    for layer in range(depth):
        p = _prepare_layer(layer, norm1_w, w_in, lambda_q1, lambda_k1, lambda_q2, lambda_k2, subln_w,
                           conv_w, conv_b, dt_bias, A_log, D_skip, ssd_norm_w, w_out, norm2_w,
                           w_gate, w_up, w_down, ff_tile=1024)
        lam0 = p["lam0"]

        q, k32, kbf, v32, vbf, zx, dt = _in_proj(xp, p)
        attn_tk = 512
        n_steps = bp * heads * (seq // (2 * attn_tk))
        late = p["late_weights"]
        ride_along = all(_side_block_rows(w.shape[0], rows_out, n_steps) is not None
                         and w.shape[1] % V7X_LANES == 0 for _, w, rows_out, _ in late)
        attn, casts = _attn_prompt(q, kbf, vbf, p["lams"], p["subln_w"],
                                   [(w, r, c) for _, w, r, c in late] if ride_along else [],
                                   batch=bp, seq=seq, heads=heads, tk=attn_tk, lam0=lam0)
        if ride_along:
            p.update({name: c for (name, _, _, _), c in zip(late, casts)})
        else:
            _cast_late_weights(p)
        ssd_y, st = _ssd(zx, dt, jnp.zeros((bp, V7X_SUBLANES, conv_dim), F32),
                         jnp.zeros((bp, inner, D_STATE), F32), p,
                         batch=bp, n_chunks=seq // CHUNK, valid=CHUNK)
        outs_p[0].append(k32.reshape(bp, seq, heads, V_HEAD_DIM))
        outs_p[1].append(v32.reshape(bp, seq, heads, V_HEAD_DIM))
        outs_p[2].append(zx.reshape(bp, seq, 3 * inner)[:, seq - (CONV_WIDTH - 1):, inner:])
        outs_p[3].append(st.reshape(bp, ssd_heads, SSD_HEAD_DIM, D_STATE))
        xp, y_prompt = _out_and_ffn(xp, attn, ssd_y, p, final_norm_w if layer == depth - 1 else None)

        q, k32, kbf, v32, vbf, zx, dt = _in_proj(xs, p)
        attn = _attn_decode(q, kbf, vbf, cache_k[layer], cache_v[layer],
                            p["lams"], p["subln_w"], batch=bs, rows=rows, heads=heads, lam0=lam0)
        pad_chunk = ((0, 0), (0, CHUNK - rows), (0, 0))
        zx_c = jnp.pad(zx.reshape(bs, rows, 3 * inner), pad_chunk).reshape(bs * CHUNK, 3 * inner)
        dt_c = jnp.pad(dt.reshape(bs, rows, V7X_LANES), pad_chunk).reshape(bs * CHUNK, V7X_LANES)
        tail = jnp.pad(state_conv[layer].astype(F32),
                       ((0, 0), (V7X_SUBLANES - (CONV_WIDTH - 1), 0), (0, 0)))
        ssd_y, st = _ssd(zx_c, dt_c, tail, state_ssm[layer].astype(F32).reshape(bs, inner, D_STATE), p,
                         batch=bs, n_chunks=1, valid=rows)
        ssd_y = ssd_y.reshape(bs, CHUNK, inner)[:, :rows].reshape(bs * rows, inner)
        outs_s[0].append(k32.reshape(bs, rows, heads, V_HEAD_DIM))
        outs_s[1].append(v32.reshape(bs, rows, heads, V_HEAD_DIM))
        outs_s[2].append(zx.reshape(bs, rows, 3 * inner)[:, rows - (CONV_WIDTH - 1):, inner:])
        outs_s[3].append(st.reshape(bs, ssd_heads, SSD_HEAD_DIM, D_STATE))
        xs, y_sample = _out_and_ffn(xs, attn, ssd_y, p, final_norm_w if layer == depth - 1 else None)

    y_prompt = y_prompt.reshape(bp, seq, d_model)
    y_sample = y_sample.reshape(bs, rows, d_model)
    return (y_prompt, y_sample, *(jnp.stack(o) for o in outs_p), *(jnp.stack(o) for o in outs_s))
```

```python
import functools
import math
from typing import Any, NamedTuple

import jax
import jax.numpy as jnp
import numpy as np
from jax import lax
from jax.experimental import pallas as pl
from jax.experimental.pallas import tpu as pltpu

F32 = jnp.float32
BF16 = jnp.bfloat16

CHUNK = 64
V_HEAD_DIM = 128
QK_DIM = V_HEAD_DIM // 2
ATTN_SCALE = QK_DIM ** -0.5
LOG2_E = math.log2(math.e)
SSD_HEAD_DIM = 64
SSD_GROUPS = 8
D_STATE = 128
CONV_WIDTH = 4
EPS = 1e-6

V7X_LANES = 128
V7X_SUBLANES = 8
BF16_SUBLANES = 2 * V7X_SUBLANES
V7X_VMEM_BYTES = 64 * 1024 * 1024
VMEM_LIMIT_BYTES = V7X_VMEM_BYTES - 8 * 1024 * 1024

NT_DIMS = (((1,), (1,)), ((), ()))


def _cparams(n_axes, flags=None):
    return pltpu.CompilerParams(
        dimension_semantics=("arbitrary",) * n_axes, vmem_limit_bytes=VMEM_LIMIT_BYTES, flags=flags)


def _silu(x):
    h = 0.5 * x
    return h + h * jnp.tanh(h)


def _lambda_init(layer):
    return 0.8 - 0.6 * math.exp(-0.3 * layer)


def _rmsnorm_body(x_ref, w_ref, o_ref):
    x = x_ref[...]
    ms = jnp.mean(x * x, axis=-1, keepdims=True)
    o_ref[...] = (x * lax.rsqrt(ms + EPS) * w_ref[...]).astype(o_ref.dtype)


def _rmsnorm(x, w, out_dtype, tr):
    m, d = x.shape
    return pl.pallas_call(
        _rmsnorm_body,
        grid=(m // tr,),
        in_specs=[pl.BlockSpec((tr, d), lambda i: (i, 0)),
                  pl.BlockSpec((1, d), lambda i: (0, 0))],
        out_specs=pl.BlockSpec((tr, d), lambda i: (i, 0)),
        out_shape=jax.ShapeDtypeStruct((m, d), out_dtype),
        compiler_params=_cparams(1),
        name="rmsnorm",
    )(x, w.reshape(1, d).astype(F32))


def _cast_cols_body(x_ref, o_ref):
    n_in = x_ref.shape[1]
    o_ref[:, :n_in] = x_ref[...].astype(o_ref.dtype)
    if o_ref.shape[1] > n_in:
        o_ref[:, n_in:] = jnp.zeros((o_ref.shape[0], o_ref.shape[1] - n_in), o_ref.dtype)


def _cast_cols(w, n_in, n_out, tr):
    rows = w.shape[0]
    assert rows % tr == 0 and n_in % V7X_LANES == 0 and n_out >= n_in
    return pl.pallas_call(
        _cast_cols_body,
        grid=(rows // tr,),
        in_specs=[pl.BlockSpec((tr, n_in), lambda i: (i, 0))],
        out_specs=pl.BlockSpec((tr, n_out), lambda i: (i, 0)),
        out_shape=jax.ShapeDtypeStruct((rows, n_out), BF16),
        compiler_params=_cparams(1),
        name="cast_cols",
    )(w)


def _cast_rows_body(x_ref, o_ref, *, n_blocks_in):
    keep = pl.program_id(0) < n_blocks_in
    o_ref[...] = jnp.where(keep, x_ref[...], 0.0).astype(o_ref.dtype)


def _cast_rows(w, rows_out, tr):
    rows, n = w.shape
    assert rows_out % tr == 0 and (rows_out <= rows or rows % tr == 0)
    n_blocks_in = rows // tr
    return pl.pallas_call(
        functools.partial(_cast_rows_body, n_blocks_in=n_blocks_in),
        grid=(rows_out // tr,),
        in_specs=[pl.BlockSpec((tr, n), lambda i: (jnp.minimum(i, n_blocks_in - 1), 0))],
        out_specs=pl.BlockSpec((tr, n), lambda i: (i, 0)),
        out_shape=jax.ShapeDtypeStruct((rows_out, n), BF16),
        compiler_params=_cparams(1),
        name="cast_rows",
    )(w)


class _Out(NamedTuple):
    dtype: Any
    scale: float = 1.0
    by_head: bool = False


class _SideNorm(NamedTuple):
    x: Any
    w: Any
    row_off: int
    rows: int
    out_dtype: Any
    out_total_rows: int
    out_row_off: int


def _side_block_rows(rows_in, rows_out, n_steps):
    br = BF16_SUBLANES
    while rows_out > br * n_steps:
        br *= 2
    return br if rows_in % br == 0 and rows_out % br == 0 else None


def _side_norm_specs(sn, n_steps, step_of):
    br = _side_block_rows(sn.rows, sn.rows, n_steps)
    assert br is not None and sn.row_off % br == 0 and sn.out_row_off % br == 0
    last = sn.rows // br - 1
    d = sn.x.shape[1]

    def block_at(off_blocks, *grid_idx):
        return (jnp.minimum(step_of(*grid_idx), last) + off_blocks, 0)

    in_specs = [pl.BlockSpec((br, d), functools.partial(block_at, sn.row_off // br)),
                pl.BlockSpec((1, d), lambda *grid_idx: (0, 0))]
    out_spec = pl.BlockSpec((br, d), functools.partial(block_at, sn.out_row_off // br))
    out_shape = jax.ShapeDtypeStruct((sn.out_total_rows, d), sn.out_dtype)
    return in_specs, out_spec, out_shape, [sn.x, sn.w.reshape(1, d).astype(F32)]


def _mm_body(*refs, n_pairs, scales, has_res, has_side, b_transposed):
    a_refs = refs[:n_pairs]
    b_refs = refs[n_pairs:2 * n_pairs]
    pos = 2 * n_pairs
    res_ref = refs[pos] if has_res else None
    pos += int(has_res)
    n_out = len(scales)
    if has_side:
        _rmsnorm_body(refs[pos], refs[pos + 1], refs[pos + 2 + n_out])
        pos += 2
    out_refs = refs[pos:pos + n_out]
    acc = None
    for a_ref, b_ref in zip(a_refs, b_refs):
        if b_transposed:
            d = lax.dot_general(a_ref[...], b_ref[...], NT_DIMS, preferred_element_type=F32)
        else:
            d = jnp.dot(a_ref[...], b_ref[...], preferred_element_type=F32)
        acc = d if acc is None else acc + d
    if has_res:
        acc = acc + res_ref[...]
    for o_ref, s in zip(out_refs, scales):
        val = (acc if s == 1.0 else acc * s).astype(o_ref.dtype)
        o_ref[...] = val.reshape(o_ref.shape)


def _matmul(a_list, b_list, *, n, tm, tn, outs, b_row_blocks=None, b_col_off=0, residual=None,
            b_transposed=False, row_off=0, rows=None, side_norm=None, name="matmul"):
    m = a_list[0].shape[0] - row_off if rows is None else rows
    n_pairs = len(a_list)
    if b_row_blocks is None:
        b_row_blocks = [0] * n_pairs
    assert b_col_off % tn == 0 and n % tn == 0 and m % tm == 0 and row_off % tm == 0
    col_blk = b_col_off // tn
    row_blk = row_off // tm
    n_j = n // tn
    in_specs = []
    for a in a_list:
        in_specs.append(pl.BlockSpec((tm, a.shape[1]), lambda i, j: (row_blk + i, 0)))
    for a, rb in zip(a_list, b_row_blocks):
        if b_transposed:
            in_specs.append(pl.BlockSpec((tn, a.shape[1]), lambda i, j, rb=rb: (col_blk + j, rb)))
        else:
            in_specs.append(pl.BlockSpec((a.shape[1], tn), lambda i, j, rb=rb: (rb, col_blk + j)))
    args = list(a_list) + list(b_list)
    if residual is not None:
        in_specs.append(pl.BlockSpec((tm, tn), lambda i, j: (row_blk + i, j)))
        args.append(residual)
    out_shape, out_specs = [], []
    for dt, _, by_head in outs:
        if by_head:
            out_shape.append(jax.ShapeDtypeStruct((m, n // V_HEAD_DIM, V_HEAD_DIM), dt))
            out_specs.append(pl.BlockSpec((tm, tn // V_HEAD_DIM, V_HEAD_DIM), lambda i, j: (i, j, 0)))
        else:
            out_shape.append(jax.ShapeDtypeStruct((m, n), dt))
            out_specs.append(pl.BlockSpec((tm, tn), lambda i, j: (i, j)))
    if side_norm is not None:
        s_in, s_out, s_shape, s_args = _side_norm_specs(
            side_norm, (m // tm) * n_j, lambda i, j: i * n_j + j)
        in_specs += s_in
        args += s_args
        out_specs.append(s_out)
        out_shape.append(s_shape)
    res = pl.pallas_call(
        functools.partial(_mm_body, n_pairs=n_pairs, scales=tuple(o.scale for o in outs),
                          has_res=residual is not None, has_side=side_norm is not None,
                          b_transposed=b_transposed),
        grid=(m // tm, n_j),
        in_specs=in_specs,
        out_specs=out_specs,
        out_shape=out_shape,
        compiler_params=_cparams(2),
        name=name,
    )(*args)
    return res


def _ffn_up_body(h_ref, wg_ref, wu_ref, *refs):
    if len(refs) > 1:
        side_x, side_w, o_ref, side_o = refs
        _rmsnorm_body(side_x, side_w, side_o)
    else:
        o_ref, = refs
    h = h_ref[...]
    g = jnp.dot(h, wg_ref[...], preferred_element_type=F32)
    u = jnp.dot(h, wu_ref[...], preferred_element_type=F32)
    o_ref[...] = (_silu(g) * u).astype(o_ref.dtype)


def _ffn_up(h, wg, wu, *, tm, tn, side_norm=None):
    m, d = h.shape
    n = wg.shape[1]
    n_j = n // tn
    in_specs = [pl.BlockSpec((tm, d), lambda i, j: (i, 0)),
                pl.BlockSpec((d, tn), lambda i, j: (0, j)),
                pl.BlockSpec((d, tn), lambda i, j: (0, j))]
    args = [h, wg, wu]
    out_specs = [pl.BlockSpec((tm, tn), lambda i, j: (i, j))]
    out_shape = [jax.ShapeDtypeStruct((m, n), BF16)]
    if side_norm is not None:
        s_in, s_out, s_shape, s_args = _side_norm_specs(
            side_norm, (m // tm) * n_j, lambda i, j: i * n_j + j)
        in_specs += s_in
        args += s_args
        out_specs.append(s_out)
        out_shape.append(s_shape)
    return pl.pallas_call(
        _ffn_up_body,
        grid=(m // tm, n_j),
        in_specs=in_specs,
        out_specs=out_specs,
        out_shape=out_shape,
        compiler_params=_cparams(2),
        name="ffn_up",
    )(*args)


def _mm_acc_body(a_ref, b_ref, r_ref, *refs):
    o_ref = refs[-2] if len(refs) > 1 else refs[0]

    def side_job():
        if len(refs) > 1:
            side_x, side_w, _, side_o = refs
            _rmsnorm_body(side_x, side_w, side_o)

    @pl.when(pl.program_id(2) == 0)
    def _first_step():
        side_job()
        o_ref[...] = r_ref[...] + jnp.dot(a_ref[...], b_ref[...], preferred_element_type=F32)

    @pl.when(pl.program_id(2) > 0)
    def _later_steps():
        side_job()
        o_ref[...] += jnp.dot(a_ref[...], b_ref[...], preferred_element_type=F32)


def _matmul_acc(a, b, residual, *, tm, tn, tk, side_norm=None):
    m, kdim = a.shape
    n = b.shape[1]
    assert m % tm == 0 and n % tn == 0 and kdim % tk == 0
    n_j, n_k = n // tn, kdim // tk
    in_specs = [pl.BlockSpec((tm, tk), lambda i, j, k: (i, k)),
                pl.BlockSpec((tk, tn), lambda i, j, k: (k, j)),
                pl.BlockSpec((tm, tn), lambda i, j, k: (i, j))]
    args = [a, b, residual]
    out_specs = [pl.BlockSpec((tm, tn), lambda i, j, k: (i, j))]
    out_shape = [jax.ShapeDtypeStruct((m, n), F32)]
    if side_norm is not None:
        s_in, s_out, s_shape, s_args = _side_norm_specs(
            side_norm, (m // tm) * n_j * n_k, lambda i, j, k: (i * n_j + j) * n_k + k)
        in_specs += s_in
        args += s_args
        out_specs.append(s_out)
        out_shape.append(s_shape)
    return pl.pallas_call(
        _mm_acc_body,
        grid=(m // tm, n_j, n_k),
        in_specs=in_specs,
        out_specs=out_specs,
        out_shape=out_shape,
        compiler_params=_cparams(3),
        name="ffn_down",
    )(*args)


def _rmsnorm_rows_body(x_ref, w_ref, dst_ref, o_ref):
    del dst_ref
    _rmsnorm_body(x_ref, w_ref, o_ref)


def _rmsnorm_into(x, w, dst, row_off, tr):
    m, d = x.shape
    assert m % tr == 0 and row_off % tr == 0
    return pl.pallas_call(
        _rmsnorm_rows_body,
        grid=(m // tr,),
        in_specs=[pl.BlockSpec((tr, d), lambda i: (i, 0)),
                  pl.BlockSpec((1, d), lambda i: (0, 0)),
                  pl.BlockSpec(memory_space=pl.ANY)],
        out_specs=pl.BlockSpec((tr, d), lambda i: (row_off // tr + i, 0)),
        out_shape=jax.ShapeDtypeStruct(dst.shape, dst.dtype),
        input_output_aliases={2: 0},
        compiler_params=_cparams(1),
        name="rmsnorm_rows",
    )(x, w.reshape(1, d).astype(F32), dst)


def _lambda_value(lq1, lk1, lq2, lk2, lam0):
    d1 = jnp.sum(lq1[...] * lk1[...], axis=-1, keepdims=True)
    d2 = jnp.sum(lq2[...] * lk2[...], axis=-1, keepdims=True)
    return jnp.exp(d1) - jnp.exp(d2) + lam0


def _subln(o, w, lam0):
    ms = jnp.mean(o * o, axis=-1, keepdims=True)
    return (o * lax.rsqrt(ms + EPS) * w) * (1.0 - lam0)


def _side_cast(in_ref, out_ref, step, n_in_blocks, n_out_blocks):
    x = in_ref[...]
    if n_out_blocks > n_in_blocks:
        x = jnp.where(jnp.minimum(step, n_out_blocks - 1) < n_in_blocks, x, 0.0)
    n_in = in_ref.shape[1]
    out_ref[:, :n_in] = x.astype(out_ref.dtype)
    if out_ref.shape[1] > n_in:
        out_ref[:, n_in:] = jnp.zeros((out_ref.shape[0], out_ref.shape[1] - n_in), out_ref.dtype)


def _attn_prompt_body(*refs, side_blocks, tq, tk, lam0):
    n_side = len(side_blocks)
    lq1, lk1, lq2, lk2, w_ref, q_ref, k_ref, v_ref = refs[:8]
    side_in = refs[8:8 + n_side]
    o_ref = refs[8 + n_side]
    side_out = refs[9 + n_side:9 + 2 * n_side]
    (k1_scr, k2_scr, vt_scr, sa, sb, pa, pb, cma, cmb, ala, alb, m_scr,
     acc_scr) = refs[9 + 2 * n_side:]
    qi = pl.program_id(2)
    step = (pl.program_id(0) * pl.num_programs(1) + pl.program_id(1)) * pl.num_programs(2) + qi
    n_kv = k1_scr.shape[0]
    k_scrs = (k1_scr, k2_scr)

    @pl.when(qi == 0)
    def _prepare_head():
        first_half = lax.broadcasted_iota(jnp.int32, (tk, V_HEAD_DIM), 1) < QK_DIM
        for c in range(n_kv):
            kb = k_ref[pl.ds(c * tk, tk), :]
            zero = jnp.zeros_like(kb)
            k1_scr[c] = jnp.where(first_half, kb, zero)
            k2_scr[c] = jnp.where(first_half, zero, kb)
            vt_scr[c, :V_HEAD_DIM, :] = v_ref[pl.ds(c * tk, tk), :].T
            ones_row = lax.broadcasted_iota(jnp.int32, (BF16_SUBLANES, tk), 0) == 0
            vt_scr[c, V_HEAD_DIM:, :] = jnp.where(ones_row, 1.0, 0.0).astype(BF16)
        vt_scr[n_kv] = jnp.zeros(vt_scr.shape[1:], BF16)

    m_scr[...] = jnp.full(m_scr.shape, -jnp.inf, F32)
    acc_scr[...] = jnp.zeros(acc_scr.shape, F32)
    @pl.when(step == 0)
    def _define_pb():
        pb[...] = jnp.zeros(pb.shape, BF16)

    alb[...] = jnp.ones(alb.shape, F32)
    for in_ref, out_ref, (n_in_blocks, n_out_blocks) in zip(side_in, side_out, side_blocks):
        _side_cast(in_ref, out_ref, step, n_in_blocks, n_out_blocks)

    def scores(j, s_scr, cm_scr, c0):
        qv = q_ref[c0:, :]
        for idx in range(2):
            s = lax.dot_general(k_scrs[idx][j], qv, NT_DIMS, preferred_element_type=F32)
            s_scr[idx, :, c0:] = s
            cm_scr[idx, :, c0:] = jnp.max(s, axis=0, keepdims=True)

    def softmax(s_scr, cm_scr, p_scr, al_scr, c0, mask, c1=tq):
        for idx in range(2):
            s = s_scr[idx, :, c0:c1]
            if mask is None:
                cmax = cm_scr[idx, :, c0:c1]
            else:
                s = jnp.where(mask, s, -jnp.inf)
                cmax = jnp.max(s, axis=0, keepdims=True)
            m_old = m_scr[idx, :, c0:c1]
            m_new = jnp.maximum(m_old, cmax)
            alpha = jnp.exp2(m_old - m_new)
            p = jnp.exp2(s - m_new)
            p_scr[idx, :, c0:c1] = p.astype(BF16)
            al_scr[idx, :, c0:c1] = alpha
            m_scr[idx, :, c0:c1] = m_new

    def value_product(j, p_scr, al_scr, c0):
        vt = vt_scr[j]
        for idx in range(2):
            acc_scr[idx, :, c0:] = al_scr[idx, :, c0:] * acc_scr[idx, :, c0:] + jnp.dot(
                vt, p_scr[idx, :, c0:], preferred_element_type=F32)

    scores(0, sa, cma, 0)

    def pair(jj, carry):
        j = 2 * jj
        scores(j + 1, sb, cmb, 0)
        value_product(jnp.where(j == 0, n_kv, j - 1), pb, alb, 0)
        softmax(sa, cma, pa, ala, 0, None)
        scores(j + 2, sa, cma, 0)
        value_product(j, pa, ala, 0)
        softmax(sb, cmb, pb, alb, 0, None)
        return carry

    lax.fori_loop(0, qi, pair, 0)

    j0 = 2 * qi
    key_chunk = lax.broadcasted_iota(jnp.int32, (tk, tk), 0) // CHUNK
    qry_chunk = lax.broadcasted_iota(jnp.int32, (tk, tk), 1) // CHUNK
    visible = key_chunk <= qry_chunk
    scores(j0 + 1, sb, cmb, tk)
    value_product(jnp.where(j0 == 0, n_kv, j0 - 1), pb, alb, 0)
    softmax(sa, cma, pa, ala, 0, visible, c1=tk)
    softmax(sa, cma, pa, ala, tk, None)
    softmax(sb, cmb, pb, alb, tk, visible)
    value_product(j0, pa, ala, 0)
    value_product(j0 + 1, pb, alb, tk)

    lam = _lambda_value(lq1, lk1, lq2, lk2, lam0)
    outs = [acc_scr[idx, :V_HEAD_DIM, :] * (1.0 / acc_scr[idx, V_HEAD_DIM:V_HEAD_DIM + 1, :])
            for idx in range(2)]
    o_t = outs[0] - lam * outs[1]
    ms = jnp.mean(o_t * o_t, axis=0, keepdims=True)
    o_t = (o_t * lax.rsqrt(ms + EPS) * w_ref[...]) * (1.0 - lam0)
    o_ref[...] = o_t.astype(o_ref.dtype).T


def _attn_prompt(q, k, v, lams, subln_w, side, *, batch, seq, heads, tk, lam0):
    tq = 2 * tk
    nq = seq // tq
    n_kv = seq // tk
    n_steps = batch * heads * nq
    lam_specs = [pl.BlockSpec((1, QK_DIM), lambda b, h, i: (0, 0))] * 4
    side_in_specs, side_out_specs, side_shapes, side_blocks = [], [], [], []
    for w, rows_out, cols_out in side:
        br = _side_block_rows(w.shape[0], rows_out, n_steps)
        assert br is not None and w.shape[1] % V7X_LANES == 0
        n_in_blocks, n_out_blocks = w.shape[0] // br, rows_out // br

        def block_of(b, h, i, last):
            return (jnp.minimum((b * heads + h) * nq + i, last), 0)

        side_in_specs.append(pl.BlockSpec(
            (br, w.shape[1]), functools.partial(block_of, last=n_in_blocks - 1)))
        side_out_specs.append(pl.BlockSpec(
            (br, cols_out), functools.partial(block_of, last=n_out_blocks - 1)))
        side_shapes.append(jax.ShapeDtypeStruct((rows_out, cols_out), BF16))
        side_blocks.append((n_in_blocks, n_out_blocks))
    out = pl.pallas_call(
        functools.partial(_attn_prompt_body, side_blocks=tuple(side_blocks), tq=tq, tk=tk, lam0=lam0),
        grid=(batch, heads, nq),
        in_specs=lam_specs + [
            pl.BlockSpec((V_HEAD_DIM, 1), lambda b, h, i: (0, 0)),
            pl.BlockSpec((tq, V_HEAD_DIM), lambda b, h, i: (b * nq + i, h)),
            pl.BlockSpec((seq, V_HEAD_DIM), lambda b, h, i: (b, h)),
            pl.BlockSpec((seq, V_HEAD_DIM), lambda b, h, i: (b, h)),
        ] + side_in_specs,
        out_specs=[pl.BlockSpec((tq, V_HEAD_DIM), lambda b, h, i: (b * nq + i, h))] + side_out_specs,
        out_shape=[jax.ShapeDtypeStruct(q.shape, BF16)] + side_shapes,
        scratch_shapes=[
            pltpu.VMEM((n_kv, tk, V_HEAD_DIM), BF16),
            pltpu.VMEM((n_kv, tk, V_HEAD_DIM), BF16),
            pltpu.VMEM((n_kv + 1, V_HEAD_DIM + BF16_SUBLANES, tk), BF16),
            pltpu.VMEM((2, tk, tq), F32), pltpu.VMEM((2, tk, tq), F32),
            pltpu.VMEM((2, tk, tq), BF16), pltpu.VMEM((2, tk, tq), BF16),
            pltpu.VMEM((2, 1, tq), F32), pltpu.VMEM((2, 1, tq), F32),
            pltpu.VMEM((2, 1, tq), F32), pltpu.VMEM((2, 1, tq), F32),
            pltpu.VMEM((2, 1, tq), F32),
            pltpu.VMEM((2, V_HEAD_DIM + BF16_SUBLANES, tq), F32),
        ],
        compiler_params=_cparams(3),
        name="attn_prompt",
    )(*lams, subln_w.reshape(V_HEAD_DIM, 1), q, k, v, *(w for w, _, _ in side))
    return out[0], out[1:]


def _attn_decode_body(lq1, lk1, lq2, lk2, w_ref, q_ref, kn_ref, vn_ref, kc_ref, vc_ref, o_ref,
                      m_scr, l_scr, acc_scr, *, rows, group, lam0):
    kb = pl.program_id(2)
    n_q = group * rows
    t_kv = kc_ref.shape[0] // group

    def stack_heads(ref):
        x = ref[...]
        return jnp.concatenate(
            [x[:, h * V_HEAD_DIM:(h + 1) * V_HEAD_DIM] for h in range(group)], axis=0)

    q = stack_heads(q_ref)
    first_half = lax.broadcasted_iota(jnp.int32, q.shape, 1) < QK_DIM
    zero = jnp.zeros_like(q)
    q_both = jnp.concatenate([jnp.where(first_half, q, zero), jnp.where(first_half, zero, q)], axis=0)

    def accumulate(k2d, v2d, same_head):
        s = lax.dot_general(q_both, k2d, NT_DIMS, preferred_element_type=F32)
        s = jnp.where(jnp.concatenate([same_head, same_head], axis=0), s, -jnp.inf)
        m_old = m_scr[...]
        m_new = jnp.maximum(m_old, jnp.max(s, axis=-1, keepdims=True))
        alpha = jnp.exp2(m_old - m_new)
        p = jnp.exp2(s - m_new)
        l_scr[...] = alpha * l_scr[...] + jnp.sum(p, axis=-1, keepdims=True)
        acc_scr[...] = alpha * acc_scr[...] + jnp.dot(p.astype(BF16), v2d, preferred_element_type=F32)
        m_scr[...] = m_new

    @pl.when(kb == 0)
    def _new_rows():
        m_scr[...] = jnp.full(m_scr.shape, -jnp.inf, F32)
        l_scr[...] = jnp.zeros(l_scr.shape, F32)
        acc_scr[...] = jnp.zeros(acc_scr.shape, F32)
        q_head = lax.broadcasted_iota(jnp.int32, (n_q, n_q), 0) // rows
        k_head = lax.broadcasted_iota(jnp.int32, (n_q, n_q), 1) // rows
        accumulate(stack_heads(kn_ref), stack_heads(vn_ref), q_head == k_head)

    for h in range(group):
        lo = [pl.ds(s * n_q + h * rows, rows) for s in range(2)]
        k_h = kc_ref[pl.ds(h, t_kv, stride=group), :].astype(BF16)
        v_h = vc_ref[pl.ds(h, t_kv, stride=group), :].astype(BF16)
        q_h = jnp.concatenate([q_both[h * rows:(h + 1) * rows],
                               q_both[n_q + h * rows:n_q + (h + 1) * rows]], axis=0)
        s = lax.dot_general(q_h, k_h, NT_DIMS, preferred_element_type=F32)
        m_old = jnp.concatenate([m_scr[lo[0], :], m_scr[lo[1], :]], axis=0)
        m_new = jnp.maximum(m_old, jnp.max(s, axis=-1, keepdims=True))
        alpha = jnp.exp2(m_old - m_new)
        p = jnp.exp2(s - m_new)
        l_new = alpha * jnp.concatenate([l_scr[lo[0], :], l_scr[lo[1], :]], axis=0) + jnp.sum(
            p, axis=-1, keepdims=True)
        acc_new = alpha * jnp.concatenate([acc_scr[lo[0], :], acc_scr[lo[1], :]], axis=0) + jnp.dot(
            p.astype(BF16), v_h, preferred_element_type=F32)
        for s_idx in range(2):
            part = slice(s_idx * rows, (s_idx + 1) * rows)
            m_scr[lo[s_idx], :] = m_new[part]
            l_scr[lo[s_idx], :] = l_new[part]
            acc_scr[lo[s_idx], :] = acc_new[part]

    @pl.when(kb == pl.num_programs(2) - 1)
    def _finish():
        lam = _lambda_value(lq1, lk1, lq2, lk2, lam0)
        o = acc_scr[...] * (1.0 / l_scr[...])
        o = o[:n_q] - lam * o[n_q:]
        o = _subln(o, w_ref[...], lam0).astype(o_ref.dtype)
        for h in range(group):
            o_ref[:, h * V_HEAD_DIM:(h + 1) * V_HEAD_DIM] = o[h * rows:(h + 1) * rows, :]


def _attn_decode(q, k_new, v_new, cache_k, cache_v, lams, subln_w, *, batch, rows, heads, lam0):
    past = cache_k.shape[1]
    group = heads
    t_kv = min(past, 512)
    assert heads % V7X_SUBLANES == 0 and past % t_kv == 0
    gw = group * V_HEAD_DIM
    cache_k = cache_k.reshape(batch, past * heads, V_HEAD_DIM)
    cache_v = cache_v.reshape(batch, past * heads, V_HEAD_DIM)
    lam_specs = [pl.BlockSpec((1, QK_DIM), lambda b, g, k: (0, 0))] * 4
    new_spec = pl.BlockSpec((rows, gw), lambda b, g, k: (b, g))
    cache_spec = pl.BlockSpec((None, t_kv * group, V_HEAD_DIM), lambda b, g, k: (b, k, 0))
    return pl.pallas_call(
        functools.partial(_attn_decode_body, rows=rows, group=group, lam0=lam0),
        grid=(batch, heads // group, past // t_kv),
        in_specs=lam_specs + [
            pl.BlockSpec((1, V_HEAD_DIM), lambda b, g, k: (0, 0)),
            new_spec, new_spec, new_spec, cache_spec, cache_spec,
        ],
        out_specs=new_spec,
        out_shape=jax.ShapeDtypeStruct(q.shape, BF16),
        scratch_shapes=[
            pltpu.VMEM((2 * group * rows, 1), F32),
            pltpu.VMEM((2 * group * rows, 1), F32),
            pltpu.VMEM((2 * group * rows, V_HEAD_DIM), F32),
        ],
        compiler_params=_cparams(3),
        name="attn_decode",
    )(*lams, subln_w, q, k_new, v_new, cache_k, cache_v)


def _split3(x):
    hi = x.astype(BF16)
    r1 = x - hi.astype(F32)
    mid = r1.astype(BF16)
    lo = (r1 - mid.astype(F32)).astype(BF16)
    return hi, mid, lo


def _ssd_body(z_ref, x_ref, bc_ref, dt_ref, tail_ref, st0_ref, cw_ref, cb_ref, dtb_ref, alog_ref,
              dsk_ref, nw_ref, e_ref, tril_ref, y_ref, stout_ref, xpad_scr, st_scr,
              *, valid, heads_per_group):
    c = pl.program_id(1)
    L = CHUNK
    inner = z_ref.shape[1]
    gw = heads_per_group * SSD_HEAD_DIM
    n_groups = inner // gw
    gn = D_STATE
    tail_rows = V7X_SUBLANES

    @pl.when(c == 0)
    def _load_stream_state():
        xpad_scr[0:tail_rows, :] = tail_ref[...]
        st_scr[...] = st0_ref[...].T

    xpad_scr[tail_rows:tail_rows + L, 0:inner] = x_ref[...]
    xpad_scr[tail_rows:tail_rows + L, inner:] = bc_ref[...]
    conv = cb_ref[...]
    first = tail_rows - (CONV_WIDTH - 1)
    for i in range(CONV_WIDTH):
        conv = conv + cw_ref[i:i + 1, :] * xpad_scr[first + i:first + i + L, :]
    xpad_scr[0:tail_rows, :] = xpad_scr[L:L + tail_rows, :]
    act = _silu(conv)
    xs = act[:, :inner]
    bm = act[:, inner:inner + n_groups * gn]
    cm = act[:, inner + n_groups * gn:]

    dtv = dt_ref[...] + dtb_ref[...]
    dt = jnp.maximum(dtv, 0.0) + jnp.log1p(jnp.exp(-jnp.abs(dtv)))
    if valid < L:
        dt = jnp.where(lax.broadcasted_iota(jnp.int32, dt.shape, 0) < valid, dt, 0.0)
    d_a = dt * (-jnp.exp(alog_ref[...]))

    pieces = jnp.concatenate(_split3(d_a), axis=1)
    cs = jnp.dot(tril_ref[...], pieces, preferred_element_type=F32)
    w = d_a.shape[1]
    a_cs = (cs[:, :w] + cs[:, w:2 * w]) + cs[:, 2 * w:]

    both = jnp.concatenate([a_cs, dt], axis=0)
    pieces = jnp.concatenate(_split3(both), axis=0)
    ex = jnp.dot(pieces, e_ref[...], preferred_element_type=F32)
    ex = (ex[:2 * L] + ex[2 * L:4 * L]) + ex[4 * L:]
    acol = ex[:L]
    dtx = ex[L:]

    row = lax.broadcasted_iota(jnp.int32, (L, inner), 0)
    sub = lax.broadcasted_iota(jnp.int32, (L, inner), 1) & (SSD_HEAD_DIM - 1)
    arow = jnp.sum(jnp.where(row == sub, acol, 0.0), axis=0, keepdims=True)
    decay_in = jnp.exp(jnp.where(sub <= row, acol - arow, -jnp.inf))
    a_last = acol[L - 1:L, :]
    decay_to_end = jnp.exp(a_last - acol)
    decay_from_start = jnp.exp(acol)
    chunk_decay = jnp.exp(a_last)

    xdt = xs * dtx
    xdt_bf = xdt.astype(BF16)
    xd_end = (xdt * decay_to_end).astype(BF16)
    bm_t = jnp.concatenate([bm, jnp.zeros_like(bm)], axis=0).T
    zeros_rows = jnp.zeros((L, gw), BF16)
    lane_head = lax.broadcasted_iota(jnp.int32, (L, gw), 1) // SSD_HEAD_DIM

    for g in range(n_groups):
        hs = slice(g * gw, (g + 1) * gw)
        ns = slice(g * gn, (g + 1) * gn)
        b_bf = bm[:, ns].astype(BF16)
        c_bf = cm[:, ns].astype(BF16)
        cb = lax.dot_general(c_bf, jnp.concatenate([b_bf] * heads_per_group, axis=0), NT_DIMS,
                             preferred_element_type=F32)
        lhs = (cb * decay_in[:, hs]).astype(BF16)
        xg = xdt_bf[:, hs]
        zero = jnp.zeros_like(xg)
        block_diag = jnp.concatenate(
            [jnp.where(lane_head == r, xg, zero) for r in range(heads_per_group)], axis=0)
        y = jnp.dot(lhs, block_diag, preferred_element_type=F32)
        st = st_scr[:, hs]
        y = y + jnp.dot(c_bf, st.astype(BF16), preferred_element_type=F32) * decay_from_start[:, hs]
        bt = bm_t[ns, :].astype(BF16)
        st_scr[:, hs] = chunk_decay[:, hs] * st + jnp.dot(
            bt, jnp.concatenate([xd_end[:, hs], zeros_rows], axis=0), preferred_element_type=F32)
        y = y + dsk_ref[:, hs] * xs[:, hs]
        y = y * _silu(z_ref[:, hs])
        y = y * lax.rsqrt(jnp.mean(y * y, axis=-1, keepdims=True) + EPS)
        y_ref[:, hs] = (y * nw_ref[:, hs]).astype(y_ref.dtype)

    @pl.when(c == pl.num_programs(1) - 1)
    def _store_stream_state():
        stout_ref[...] = st_scr[...].T


def _ssd(zx, dt_raw, tail, st0, p, *, batch, n_chunks, valid):
    rows = zx.shape[0]
    inner = zx.shape[1] // 3
    conv_dim = 2 * inner
    heads = inner // SSD_HEAD_DIM
    L = CHUNK
    assert SSD_HEAD_DIM == CHUNK and conv_dim == inner + 2 * SSD_GROUPS * D_STATE
    expand = np.zeros((V7X_LANES, inner), np.float32)
    for h in range(heads):
        expand[h, h * SSD_HEAD_DIM:(h + 1) * SSD_HEAD_DIM] = 1.0
    tril = np.tril(np.ones((L, L), np.float32))

    def row_blk(col):
        return pl.BlockSpec((L, inner), lambda b, c, col=col: (b * n_chunks + c, col))

    def const(shape):
        return pl.BlockSpec(shape, lambda b, c: (0,) * len(shape))

    return pl.pallas_call(
        functools.partial(_ssd_body, valid=valid, heads_per_group=heads // SSD_GROUPS),
        grid=(batch, n_chunks),
        in_specs=[
            row_blk(0), row_blk(1), row_blk(2),
            pl.BlockSpec((L, V7X_LANES), lambda b, c: (b * n_chunks + c, 0)),
            pl.BlockSpec((None, V7X_SUBLANES, conv_dim), lambda b, c: (b, 0, 0)),
            pl.BlockSpec((None, inner, D_STATE), lambda b, c: (b, 0, 0)),
            const((CONV_WIDTH, conv_dim)), const((1, conv_dim)),
            const((1, V7X_LANES)), const((1, V7X_LANES)),
            const((1, inner)), const((1, inner)),
            const((V7X_LANES, inner)), const((L, L)),
        ],
        out_specs=[
            pl.BlockSpec((L, inner), lambda b, c: (b * n_chunks + c, 0)),
            pl.BlockSpec((None, inner, D_STATE), lambda b, c: (b, 0, 0)),
        ],
        out_shape=[
            jax.ShapeDtypeStruct((rows, inner), BF16),
            jax.ShapeDtypeStruct((batch, inner, D_STATE), F32),
        ],
        scratch_shapes=[
            pltpu.VMEM((L + V7X_SUBLANES, conv_dim), F32),
            pltpu.VMEM((D_STATE, inner), F32),
        ],
        compiler_params=_cparams(2),
        name="ssd",
    )(zx, zx, zx, dt_raw, tail, st0, p["conv_w"], p["conv_b"], p["dt_bias"], p["a_log"],
      p["d_skip"], p["ssd_norm_w"], jnp.asarray(expand, BF16), jnp.asarray(tril, BF16))


def _pad_lanes(v, width):
    return jnp.pad(v.astype(F32), (0, width - v.shape[0])).reshape(1, width)


def _prepare_layer(layer, norm1_w, w_in, lambda_q1, lambda_k1, lambda_q2, lambda_k2, subln_w,
                   conv_w, conv_b, dt_bias, A_log, D_skip, ssd_norm_w, w_out, norm2_w,
                   w_gate, w_up, w_down, ff_tile):
    d_model = w_in.shape[1]
    attn_w = 3 * (w_out.shape[1] // 2)
    inner = w_out.shape[1] // 2
    main_cols = attn_w + inner + 2 * inner
    cast_rows = 128
    w_in_t = jnp.swapaxes(w_in[layer], 0, 1)
    w_in_bf = _cast_rows(w_in_t, main_cols, 2 * cast_rows)
    n_heads = w_in.shape[2] - main_cols
    w_dt = jnp.pad(w_in_t[main_cols:, :].astype(BF16), ((0, V7X_LANES - n_heads), (0, 0)))
    d_ff = w_gate.shape[2]
    ff_width = d_ff + (-d_ff % ff_tile)
    return {
        "lam0": _lambda_init(layer),
        "norm1_w": norm1_w[layer], "norm2_w": norm2_w[layer],
        "w_in": w_in_bf, "w_dt": w_dt, "main_cols": main_cols,
        "lams": [v[layer].reshape(1, -1).astype(F32)
                 for v in (lambda_q1, lambda_k1, lambda_q2, lambda_k2)],
        "subln_w": subln_w[layer].reshape(1, -1).astype(F32),
        "conv_w": conv_w[layer].astype(F32), "conv_b": conv_b[layer].reshape(1, -1).astype(F32),
        "dt_bias": _pad_lanes(dt_bias[layer], V7X_LANES),
        "a_log": _pad_lanes(A_log[layer], V7X_LANES),
        "d_skip": jnp.repeat(D_skip[layer].astype(F32), SSD_HEAD_DIM).reshape(1, -1),
        "ssd_norm_w": ssd_norm_w[layer].reshape(1, -1).astype(F32),
        "late_weights": [("w_out", w_out[layer], 2 * inner, d_model),
                         ("w_gate", w_gate[layer], d_model, ff_width),
                         ("w_up", w_up[layer], d_model, ff_width),
                         ("w_down", w_down[layer], ff_width, d_model)],
        "d_model": d_model, "inner": inner,
    }


def _cast_late_weights(p):
    cast_rows = 128
    for name, w, rows_out, cols_out in p["late_weights"]:
        if rows_out == w.shape[0]:
            p[name] = _cast_cols(w, w.shape[1], cols_out, cast_rows)
        else:
            p[name] = _cast_rows(w, rows_out, 2 * cast_rows)


def _tiles(m):
    return 1024 if m % 1024 == 0 else m


def _in_proj(x2d, p):
    m = x2d.shape[0]
    inner = p["inner"]
    tm = _tiles(m)
    h = _rmsnorm(x2d, p["norm1_w"], BF16, min(256, m))
    proj = functools.partial(_matmul, [h], tm=tm, b_transposed=True)
    q, = proj([p["w_in"]], n=inner, tn=1024, b_col_off=0,
              outs=[_Out(BF16, ATTN_SCALE * LOG2_E)], name="proj_q")
    k32, kbf = proj([p["w_in"]], n=inner, tn=1024, b_col_off=inner,
                    outs=[_Out(F32, by_head=True), _Out(BF16)], name="proj_k")
    v32, vbf = proj([p["w_in"]], n=inner, tn=1024, b_col_off=2 * inner,
                    outs=[_Out(F32, by_head=True), _Out(BF16)], name="proj_v")
    zx, = proj([p["w_in"]], n=3 * inner, tn=1024, b_col_off=3 * inner,
               outs=[_Out(F32)], name="proj_zx")
    dt, = proj([p["w_dt"]], n=V7X_LANES, tn=V7X_LANES, outs=[_Out(F32)], name="proj_dt")
    return q, k32, kbf, v32, vbf, zx, dt


def _out_and_ffn(x2d, attn, ssd_y, p, final_w):
    m = x2d.shape[0]
    d_model = p["d_model"]
    tm = _tiles(m)
    out_proj = functools.partial(
        _matmul, [attn, ssd_y], [p["w_out"], p["w_out"]], b_row_blocks=[0, 1], n=d_model,
        tm=tm, outs=[_Out(F32)], residual=x2d, name="out_proj")
    ffn_up = functools.partial(_ffn_up, tm=tm, tn=512 if tm == 1024 else 1024)
    ffn_down = functools.partial(_matmul_acc, tm=tm, tn=min(1024, d_model), tk=p["w_down"].shape[0] // 4)
    half = m // 2
    if m % (2 * tm) != 0:
        x1, = out_proj(tn=min(1024, d_model))
        h2 = _rmsnorm(x1, p["norm2_w"], BF16, min(256, m))
        x2, = ffn_down(ffn_up(h2, p["w_gate"], p["w_up"])[0], p["w_down"], x1)
        y = None if final_w is None else _rmsnorm(x2, final_w, F32, min(256, m))
        return x2, y

    def norm_of(x, w, dtype, total_rows):
        return _SideNorm(x, w, 0, half, dtype, total_rows, 0)

    tn_out = min(512, d_model)
    x1_a, = out_proj(tn=min(1024, d_model), row_off=0, rows=half)
    x1_b, h2_a = out_proj(tn=tn_out, row_off=half, rows=half,
                          side_norm=norm_of(x1_a, p["norm2_w"], BF16, half))
    act_a, h2_b = ffn_up(h2_a, p["w_gate"], p["w_up"], side_norm=norm_of(x1_b, p["norm2_w"], BF16, half))
    act_b, = ffn_up(h2_b, p["w_gate"], p["w_up"])
    x2_a, = ffn_down(act_a, p["w_down"], x1_a)
    if final_w is None:
        x2_b, = ffn_down(act_b, p["w_down"], x1_b)
        return jnp.concatenate([x2_a, x2_b], axis=0), None
    x2_b, y = ffn_down(act_b, p["w_down"], x1_b, side_norm=norm_of(x2_a, final_w, F32, m))
    return None, _rmsnorm_into(x2_b, final_w, y, half, 256)


def kernel(x_prompt, x_sample, cache_k, cache_v, state_conv, state_ssm, norm1_w, w_in, lambda_q1, lambda_k1, lambda_q2, lambda_k2, subln_w, conv_w, conv_b, dt_bias, A_log, D_skip, ssd_norm_w, w_out, norm2_w, w_gate, w_up, w_down, final_norm_w):
    depth = w_in.shape[0]
    bp, seq, d_model = x_prompt.shape
    bs, rows, _ = x_sample.shape
    heads = cache_k.shape[3]
    inner = w_out.shape[1] // 2
    attn_w = heads * V_HEAD_DIM
    conv_dim = conv_w.shape[2]
    ssd_heads = state_ssm.shape[2]
    assert seq % CHUNK == 0 and rows <= CHUNK and rows >= CONV_WIDTH - 1 and attn_w == inner

    xp = x_prompt.reshape(bp * seq, d_model)
    xs = x_sample.reshape(bs * rows, d_model)
    outs_p = [[], [], [], []]
    outs_s = [[], [], [], []]
    for layer in range(depth):
        p = _prepare_layer(layer, norm1_w, w_in, lambda_q1, lambda_k1, lambda_q2, lambda_k2, subln_w,
                           conv_w, conv_b, dt_bias, A_log, D_skip, ssd_norm_w, w_out, norm2_w,
                           w_gate, w_up, w_down, ff_tile=1024)
        lam0 = p["lam0"]

        q, k32, kbf, v32, vbf, zx, dt = _in_proj(xp, p)
        attn_tk = 512
        n_steps = bp * heads * (seq // (2 * attn_tk))
        late = p["late_weights"]
        ride_along = all(_side_block_rows(w.shape[0], rows_out, n_steps) is not None
                         and w.shape[1] % V7X_LANES == 0 for _, w, rows_out, _ in late)
        attn, casts = _attn_prompt(q, kbf, vbf, p["lams"], p["subln_w"],
                                   [(w, r, c) for _, w, r, c in late] if ride_along else [],
                                   batch=bp, seq=seq, heads=heads, tk=attn_tk, lam0=lam0)
        if ride_along:
            p.update({name: c for (name, _, _, _), c in zip(late, casts)})
        else:
            _cast_late_weights(p)
        ssd_y, st = _ssd(zx, dt, jnp.zeros((bp, V7X_SUBLANES, conv_dim), F32),
                         jnp.zeros((bp, inner, D_STATE), F32), p,
                         batch=bp, n_chunks=seq // CHUNK, valid=CHUNK)
        outs_p[0].append(k32.reshape(bp, seq, heads, V_HEAD_DIM))
        outs_p[1].append(v32.reshape(bp, seq, heads, V_HEAD_DIM))
        outs_p[2].append(zx.reshape(bp, seq, 3 * inner)[:, seq - (CONV_WIDTH - 1):, inner:])
        outs_p[3].append(st.reshape(bp, ssd_heads, SSD_HEAD_DIM, D_STATE))
        xp, y_prompt = _out_and_ffn(xp, attn, ssd_y, p, final_norm_w if layer == depth - 1 else None)

        q, k32, kbf, v32, vbf, zx, dt = _in_proj(xs, p)
        attn = _attn_decode(q, kbf, vbf, cache_k[layer], cache_v[layer],
                            p["lams"], p["subln_w"], batch=bs, rows=rows, heads=heads, lam0=lam0)
        pad_chunk = ((0, 0), (0, CHUNK - rows), (0, 0))
        zx_c = jnp.pad(zx.reshape(bs, rows, 3 * inner), pad_chunk).reshape(bs * CHUNK, 3 * inner)
        dt_c = jnp.pad(dt.reshape(bs, rows, V7X_LANES), pad_chunk).reshape(bs * CHUNK, V7X_LANES)
        tail = jnp.pad(state_conv[layer].astype(F32),
                       ((0, 0), (V7X_SUBLANES - (CONV_WIDTH - 1), 0), (0, 0)))
        ssd_y, st = _ssd(zx_c, dt_c, tail, state_ssm[layer].astype(F32).reshape(bs, inner, D_STATE), p,
                         batch=bs, n_chunks=1, valid=rows)
        ssd_y = ssd_y.reshape(bs, CHUNK, inner)[:, :rows].reshape(bs * rows, inner)
        outs_s[0].append(k32.reshape(bs, rows, heads, V_HEAD_DIM))
        outs_s[1].append(v32.reshape(bs, rows, heads, V_HEAD_DIM))
        outs_s[2].append(zx.reshape(bs, rows, 3 * inner)[:, rows - (CONV_WIDTH - 1):, inner:])
        outs_s[3].append(st.reshape(bs, ssd_heads, SSD_HEAD_DIM, D_STATE))
        xs, y_sample = _out_and_ffn(xs, attn, ssd_y, p, final_norm_w if layer == depth - 1 else None)

    y_prompt = y_prompt.reshape(bp, seq, d_model)
    y_sample = y_sample.reshape(bs, rows, d_model)
    return (y_prompt, y_sample, *(jnp.stack(o) for o in outs_p), *(jnp.stack(o) for o in outs_s))
```

```python
import functools
import math
from typing import Any, NamedTuple

import jax
import jax.numpy as jnp
import numpy as np
from jax import lax
from jax.experimental import pallas as pl
from jax.experimental.pallas import tpu as pltpu

F32 = jnp.float32
BF16 = jnp.bfloat16

CHUNK = 64
V_HEAD_DIM = 128
QK_DIM = V_HEAD_DIM // 2
ATTN_SCALE = QK_DIM ** -0.5
LOG2_E = math.log2(math.e)
SSD_HEAD_DIM = 64
SSD_GROUPS = 8
D_STATE = 128
CONV_WIDTH = 4
EPS = 1e-6

V7X_LANES = 128
V7X_SUBLANES = 8
BF16_SUBLANES = 2 * V7X_SUBLANES
V7X_VMEM_BYTES = 64 * 1024 * 1024
VMEM_LIMIT_BYTES = V7X_VMEM_BYTES - 8 * 1024 * 1024

NT_DIMS = (((1,), (1,)), ((), ()))


def _cparams(n_axes, flags=None):
    return pltpu.CompilerParams(
        dimension_semantics=("arbitrary",) * n_axes, vmem_limit_bytes=VMEM_LIMIT_BYTES, flags=flags)


def _silu(x):
    h = 0.5 * x
    return h + h * jnp.tanh(h)


def _lambda_init(layer):
    return 0.8 - 0.6 * math.exp(-0.3 * layer)


def _rmsnorm_body(x_ref, w_ref, o_ref):
    x = x_ref[...]
    ms = jnp.mean(x * x, axis=-1, keepdims=True)
    o_ref[...] = (x * lax.rsqrt(ms + EPS) * w_ref[...]).astype(o_ref.dtype)


def _rmsnorm(x, w, out_dtype, tr):
    m, d = x.shape
    return pl.pallas_call(
        _rmsnorm_body,
        grid=(m // tr,),
        in_specs=[pl.BlockSpec((tr, d), lambda i: (i, 0)),
                  pl.BlockSpec((1, d), lambda i: (0, 0))],
        out_specs=pl.BlockSpec((tr, d), lambda i: (i, 0)),
        out_shape=jax.ShapeDtypeStruct((m, d), out_dtype),
        compiler_params=_cparams(1),
        name="rmsnorm",
    )(x, w.reshape(1, d).astype(F32))


def _cast_cols_body(x_ref, o_ref):
    n_in = x_ref.shape[1]
    o_ref[:, :n_in] = x_ref[...].astype(o_ref.dtype)
    if o_ref.shape[1] > n_in:
        o_ref[:, n_in:] = jnp.zeros((o_ref.shape[0], o_ref.shape[1] - n_in), o_ref.dtype)


def _cast_cols(w, n_in, n_out, tr):
    rows = w.shape[0]
    assert rows % tr == 0 and n_in % V7X_LANES == 0 and n_out >= n_in
    return pl.pallas_call(
        _cast_cols_body,
        grid=(rows // tr,),
        in_specs=[pl.BlockSpec((tr, n_in), lambda i: (i, 0))],
        out_specs=pl.BlockSpec((tr, n_out), lambda i: (i, 0)),
        out_shape=jax.ShapeDtypeStruct((rows, n_out), BF16),
        compiler_params=_cparams(1),
        name="cast_cols",
    )(w)


def _cast_rows_body(x_ref, o_ref, *, n_blocks_in):
    keep = pl.program_id(0) < n_blocks_in
    o_ref[...] = jnp.where(keep, x_ref[...], 0.0).astype(o_ref.dtype)


def _cast_rows(w, rows_out, tr):
    rows, n = w.shape
    assert rows_out % tr == 0 and (rows_out <= rows or rows % tr == 0)
    n_blocks_in = rows // tr
    return pl.pallas_call(
        functools.partial(_cast_rows_body, n_blocks_in=n_blocks_in),
        grid=(rows_out // tr,),
        in_specs=[pl.BlockSpec((tr, n), lambda i: (jnp.minimum(i, n_blocks_in - 1), 0))],
        out_specs=pl.BlockSpec((tr, n), lambda i: (i, 0)),
        out_shape=jax.ShapeDtypeStruct((rows_out, n), BF16),
        compiler_params=_cparams(1),
        name="cast_rows",
    )(w)


class _Out(NamedTuple):
    dtype: Any
    scale: float = 1.0
    by_head: bool = False


class _SideNorm(NamedTuple):
    x: Any
    w: Any
    row_off: int
    rows: int
    out_dtype: Any
    out_total_rows: int
    out_row_off: int


def _side_block_rows(rows_in, rows_out, n_steps):
    br = BF16_SUBLANES
    while rows_out > br * n_steps:
        br *= 2
    return br if rows_in % br == 0 and rows_out % br == 0 else None


def _side_norm_specs(sn, n_steps, step_of):
    br = _side_block_rows(sn.rows, sn.rows, n_steps)
    assert br is not None and sn.row_off % br == 0 and sn.out_row_off % br == 0
    last = sn.rows // br - 1
    d = sn.x.shape[1]

    def block_at(off_blocks, *grid_idx):
        return (jnp.minimum(step_of(*grid_idx), last) + off_blocks, 0)

    in_specs = [pl.BlockSpec((br, d), functools.partial(block_at, sn.row_off // br)),
                pl.BlockSpec((1, d), lambda *grid_idx: (0, 0))]
    out_spec = pl.BlockSpec((br, d), functools.partial(block_at, sn.out_row_off // br))
    out_shape = jax.ShapeDtypeStruct((sn.out_total_rows, d), sn.out_dtype)
    return in_specs, out_spec, out_shape, [sn.x, sn.w.reshape(1, d).astype(F32)]


def _mm_body(*refs, n_pairs, scales, has_res, has_side, b_transposed):
    a_refs = refs[:n_pairs]
    b_refs = refs[n_pairs:2 * n_pairs]
    pos = 2 * n_pairs
    res_ref = refs[pos] if has_res else None
    pos += int(has_res)
    n_out = len(scales)
    if has_side:
        _rmsnorm_body(refs[pos], refs[pos + 1], refs[pos + 2 + n_out])
        pos += 2
    out_refs = refs[pos:pos + n_out]
    acc = None
    for a_ref, b_ref in zip(a_refs, b_refs):
        if b_transposed:
            d = lax.dot_general(a_ref[...], b_ref[...], NT_DIMS, preferred_element_type=F32)
        else:
            d = jnp.dot(a_ref[...], b_ref[...], preferred_element_type=F32)
        acc = d if acc is None else acc + d
    if has_res:
        acc = acc + res_ref[...]
    for o_ref, s in zip(out_refs, scales):
        val = (acc if s == 1.0 else acc * s).astype(o_ref.dtype)
        o_ref[...] = val.reshape(o_ref.shape)


def _matmul(a_list, b_list, *, n, tm, tn, outs, b_row_blocks=None, b_col_off=0, residual=None,
            b_transposed=False, row_off=0, rows=None, side_norm=None, name="matmul"):
    m = a_list[0].shape[0] - row_off if rows is None else rows
    n_pairs = len(a_list)
    if b_row_blocks is None:
        b_row_blocks = [0] * n_pairs
    assert b_col_off % tn == 0 and n % tn == 0 and m % tm == 0 and row_off % tm == 0
    col_blk = b_col_off // tn
    row_blk = row_off // tm
    n_j = n // tn
    in_specs = []
    for a in a_list:
        in_specs.append(pl.BlockSpec((tm, a.shape[1]), lambda i, j: (row_blk + i, 0)))
    for a, rb in zip(a_list, b_row_blocks):
        if b_transposed:
            in_specs.append(pl.BlockSpec((tn, a.shape[1]), lambda i, j, rb=rb: (col_blk + j, rb)))
        else:
            in_specs.append(pl.BlockSpec((a.shape[1], tn), lambda i, j, rb=rb: (rb, col_blk + j)))
    args = list(a_list) + list(b_list)
    if residual is not None:
        in_specs.append(pl.BlockSpec((tm, tn), lambda i, j: (row_blk + i, j)))
        args.append(residual)
    out_shape, out_specs = [], []
    for dt, _, by_head in outs:
        if by_head:
            out_shape.append(jax.ShapeDtypeStruct((m, n // V_HEAD_DIM, V_HEAD_DIM), dt))
            out_specs.append(pl.BlockSpec((tm, tn // V_HEAD_DIM, V_HEAD_DIM), lambda i, j: (i, j, 0)))
        else:
            out_shape.append(jax.ShapeDtypeStruct((m, n), dt))
            out_specs.append(pl.BlockSpec((tm, tn), lambda i, j: (i, j)))
    if side_norm is not None:
        s_in, s_out, s_shape, s_args = _side_norm_specs(
            side_norm, (m // tm) * n_j, lambda i, j: i * n_j + j)
        in_specs += s_in
        args += s_args
        out_specs.append(s_out)
        out_shape.append(s_shape)
    res = pl.pallas_call(
        functools.partial(_mm_body, n_pairs=n_pairs, scales=tuple(o.scale for o in outs),
                          has_res=residual is not None, has_side=side_norm is not None,
                          b_transposed=b_transposed),
        grid=(m // tm, n_j),
        in_specs=in_specs,
        out_specs=out_specs,
        out_shape=out_shape,
        compiler_params=_cparams(2),
        name=name,
    )(*args)
    return res


def _ffn_up_body(h_ref, wg_ref, wu_ref, *refs):
    if len(refs) > 1:
        side_x, side_w, o_ref, side_o = refs
        _rmsnorm_body(side_x, side_w, side_o)
    else:
        o_ref, = refs
    h = h_ref[...]
    g = jnp.dot(h, wg_ref[...], preferred_element_type=F32)
    u = jnp.dot(h, wu_ref[...], preferred_element_type=F32)
    o_ref[...] = (_silu(g) * u).astype(o_ref.dtype)


def _ffn_up(h, wg, wu, *, tm, tn, side_norm=None):
    m, d = h.shape
    n = wg.shape[1]
    n_j = n // tn
    in_specs = [pl.BlockSpec((tm, d), lambda i, j: (i, 0)),
                pl.BlockSpec((d, tn), lambda i, j: (0, j)),
                pl.BlockSpec((d, tn), lambda i, j: (0, j))]
    args = [h, wg, wu]
    out_specs = [pl.BlockSpec((tm, tn), lambda i, j: (i, j))]
    out_shape = [jax.ShapeDtypeStruct((m, n), BF16)]
    if side_norm is not None:
        s_in, s_out, s_shape, s_args = _side_norm_specs(
            side_norm, (m // tm) * n_j, lambda i, j: i * n_j + j)
        in_specs += s_in
        args += s_args
        out_specs.append(s_out)
        out_shape.append(s_shape)
    return pl.pallas_call(
        _ffn_up_body,
        grid=(m // tm, n_j),
        in_specs=in_specs,
        out_specs=out_specs,
        out_shape=out_shape,
        compiler_params=_cparams(2),
        name="ffn_up",
    )(*args)


def _mm_acc_body(a_ref, b_ref, r_ref, *refs):
    o_ref = refs[-2] if len(refs) > 1 else refs[0]

    def side_job():
        if len(refs) > 1:
            side_x, side_w, _, side_o = refs
            _rmsnorm_body(side_x, side_w, side_o)

    @pl.when(pl.program_id(2) == 0)
    def _first_step():
        side_job()
        o_ref[...] = r_ref[...] + jnp.dot(a_ref[...], b_ref[...], preferred_element_type=F32)

    @pl.when(pl.program_id(2) > 0)
    def _later_steps():
        side_job()
        o_ref[...] += jnp.dot(a_ref[...], b_ref[...], preferred_element_type=F32)


def _matmul_acc(a, b, residual, *, tm, tn, tk, side_norm=None):
    m, kdim = a.shape
    n = b.shape[1]
    assert m % tm == 0 and n % tn == 0 and kdim % tk == 0
    n_j, n_k = n // tn, kdim // tk
    in_specs = [pl.BlockSpec((tm, tk), lambda i, j, k: (i, k)),
                pl.BlockSpec((tk, tn), lambda i, j, k: (k, j)),
                pl.BlockSpec((tm, tn), lambda i, j, k: (i, j))]
    args = [a, b, residual]
    out_specs = [pl.BlockSpec((tm, tn), lambda i, j, k: (i, j))]
    out_shape = [jax.ShapeDtypeStruct((m, n), F32)]
    if side_norm is not None:
        s_in, s_out, s_shape, s_args = _side_norm_specs(
            side_norm, (m // tm) * n_j * n_k, lambda i, j, k: (i * n_j + j) * n_k + k)
        in_specs += s_in
        args += s_args
        out_specs.append(s_out)
        out_shape.append(s_shape)
    return pl.pallas_call(
        _mm_acc_body,
        grid=(m // tm, n_j, n_k),
        in_specs=in_specs,
        out_specs=out_specs,
        out_shape=out_shape,
        compiler_params=_cparams(3),
        name="ffn_down",
    )(*args)


def _rmsnorm_rows_body(x_ref, w_ref, dst_ref, o_ref):
    del dst_ref
    _rmsnorm_body(x_ref, w_ref, o_ref)


def _rmsnorm_into(x, w, dst, row_off, tr):
    m, d = x.shape
    assert m % tr == 0 and row_off % tr == 0
    return pl.pallas_call(
        _rmsnorm_rows_body,
        grid=(m // tr,),
        in_specs=[pl.BlockSpec((tr, d), lambda i: (i, 0)),
                  pl.BlockSpec((1, d), lambda i: (0, 0)),
                  pl.BlockSpec(memory_space=pl.ANY)],
        out_specs=pl.BlockSpec((tr, d), lambda i: (row_off // tr + i, 0)),
        out_shape=jax.ShapeDtypeStruct(dst.shape, dst.dtype),
        input_output_aliases={2: 0},
        compiler_params=_cparams(1),
        name="rmsnorm_rows",
    )(x, w.reshape(1, d).astype(F32), dst)


def _lambda_value(lq1, lk1, lq2, lk2, lam0):
    d1 = jnp.sum(lq1[...] * lk1[...], axis=-1, keepdims=True)
    d2 = jnp.sum(lq2[...] * lk2[...], axis=-1, keepdims=True)
    return jnp.exp(d1) - jnp.exp(d2) + lam0


def _subln(o, w, lam0):
    ms = jnp.mean(o * o, axis=-1, keepdims=True)
    return (o * lax.rsqrt(ms + EPS) * w) * (1.0 - lam0)


def _side_cast(in_ref, out_ref, step, n_in_blocks, n_out_blocks):
    x = in_ref[...]
    if n_out_blocks > n_in_blocks:
        x = jnp.where(jnp.minimum(step, n_out_blocks - 1) < n_in_blocks, x, 0.0)
    n_in = in_ref.shape[1]
    out_ref[:, :n_in] = x.astype(out_ref.dtype)
    if out_ref.shape[1] > n_in:
        out_ref[:, n_in:] = jnp.zeros((out_ref.shape[0], out_ref.shape[1] - n_in), out_ref.dtype)


def _attn_prompt_body(*refs, side_blocks, tq, tk, lam0):
    n_side = len(side_blocks)
    lq1, lk1, lq2, lk2, w_ref, q_ref, k_ref, v_ref = refs[:8]
    side_in = refs[8:8 + n_side]
    o_ref = refs[8 + n_side]
    side_out = refs[9 + n_side:9 + 2 * n_side]
    (k1_scr, k2_scr, vt_scr, sa, sb, pa, pb, cma, cmb, ala, alb, m_scr,
     acc_scr) = refs[9 + 2 * n_side:]
    qi = pl.program_id(2)
    step = (pl.program_id(0) * pl.num_programs(1) + pl.program_id(1)) * pl.num_programs(2) + qi
    n_kv = k1_scr.shape[0]
    k_scrs = (k1_scr, k2_scr)

    @pl.when(qi == 0)
    def _prepare_head():
        first_half = lax.broadcasted_iota(jnp.int32, (tk, V_HEAD_DIM), 1) < QK_DIM
        for c in range(n_kv):
            kb = k_ref[pl.ds(c * tk, tk), :]
            zero = jnp.zeros_like(kb)
            k1_scr[c] = jnp.where(first_half, kb, zero)
            k2_scr[c] = jnp.where(first_half, zero, kb)
            vt_scr[c, :V_HEAD_DIM, :] = v_ref[pl.ds(c * tk, tk), :].T
            ones_row = lax.broadcasted_iota(jnp.int32, (BF16_SUBLANES, tk), 0) == 0
            vt_scr[c, V_HEAD_DIM:, :] = jnp.where(ones_row, 1.0, 0.0).astype(BF16)
        vt_scr[n_kv] = jnp.zeros(vt_scr.shape[1:], BF16)

    m_scr[...] = jnp.full(m_scr.shape, -jnp.inf, F32)
    acc_scr[...] = jnp.zeros(acc_scr.shape, F32)
    @pl.when(step == 0)
    def _define_pb():
        pb[...] = jnp.zeros(pb.shape, BF16)

    alb[...] = jnp.ones(alb.shape, F32)
    for in_ref, out_ref, (n_in_blocks, n_out_blocks) in zip(side_in, side_out, side_blocks):
        _side_cast(in_ref, out_ref, step, n_in_blocks, n_out_blocks)

    def scores(j, s_scr, cm_scr, c0):
        qv = q_ref[c0:, :]
        for idx in range(2):
            s = lax.dot_general(k_scrs[idx][j], qv, NT_DIMS, preferred_element_type=F32)
            s_scr[idx, :, c0:] = s
            cm_scr[idx, :, c0:] = jnp.max(s, axis=0, keepdims=True)

    def softmax(s_scr, cm_scr, p_scr, al_scr, c0, mask, c1=tq):
        for idx in range(2):
            s = s_scr[idx, :, c0:c1]
            if mask is None:
                cmax = cm_scr[idx, :, c0:c1]
            else:
                s = jnp.where(mask, s, -jnp.inf)
                cmax = jnp.max(s, axis=0, keepdims=True)
            m_old = m_scr[idx, :, c0:c1]
            m_new = jnp.maximum(m_old, cmax)
            alpha = jnp.exp2(m_old - m_new)
            p = jnp.exp2(s - m_new)
            p_scr[idx, :, c0:c1] = p.astype(BF16)
            al_scr[idx, :, c0:c1] = alpha
            m_scr[idx, :, c0:c1] = m_new

    def value_product(j, p_scr, al_scr, c0):
        vt = vt_scr[j]
        for idx in range(2):
            acc_scr[idx, :, c0:] = al_scr[idx, :, c0:] * acc_scr[idx, :, c0:] + jnp.dot(
                vt, p_scr[idx, :, c0:], preferred_element_type=F32)

    scores(0, sa, cma, 0)

    def pair(jj, carry):
        j = 2 * jj
        scores(j + 1, sb, cmb, 0)
        value_product(jnp.where(j == 0, n_kv, j - 1), pb, alb, 0)
        softmax(sa, cma, pa, ala, 0, None)
        scores(j + 2, sa, cma, 0)
        value_product(j, pa, ala, 0)
        softmax(sb, cmb, pb, alb, 0, None)
        return carry

    lax.fori_loop(0, qi, pair, 0)

    j0 = 2 * qi
    key_chunk = lax.broadcasted_iota(jnp.int32, (tk, tk), 0) // CHUNK
    qry_chunk = lax.broadcasted_iota(jnp.int32, (tk, tk), 1) // CHUNK
    visible = key_chunk <= qry_chunk
    scores(j0 + 1, sb, cmb, tk)
    value_product(jnp.where(j0 == 0, n_kv, j0 - 1), pb, alb, 0)
    softmax(sa, cma, pa, ala, 0, visible, c1=tk)
    softmax(sa, cma, pa, ala, tk, None)
    softmax(sb, cmb, pb, alb, tk, visible)
    value_product(j0, pa, ala, 0)
    value_product(j0 + 1, pb, alb, tk)

    lam = _lambda_value(lq1, lk1, lq2, lk2, lam0)
    outs = [acc_scr[idx, :V_HEAD_DIM, :] * (1.0 / acc_scr[idx, V_HEAD_DIM:V_HEAD_DIM + 1, :])
            for idx in range(2)]
    o_t = outs[0] - lam * outs[1]
    ms = jnp.mean(o_t * o_t, axis=0, keepdims=True)
    o_t = (o_t * lax.rsqrt(ms + EPS) * w_ref[...]) * (1.0 - lam0)
    o_ref[...] = o_t.astype(o_ref.dtype).T


def _attn_prompt(q, k, v, lams, subln_w, side, *, batch, seq, heads, tk, lam0):
    tq = 2 * tk
    nq = seq // tq
    n_kv = seq // tk
    n_steps = batch * heads * nq
    lam_specs = [pl.BlockSpec((1, QK_DIM), lambda b, h, i: (0, 0))] * 4
    side_in_specs, side_out_specs, side_shapes, side_blocks = [], [], [], []
    for w, rows_out, cols_out in side:
        br = _side_block_rows(w.shape[0], rows_out, n_steps)
        assert br is not None and w.shape[1] % V7X_LANES == 0
        n_in_blocks, n_out_blocks = w.shape[0] // br, rows_out // br

        def block_of(b, h, i, last):
            return (jnp.minimum((b * heads + h) * nq + i, last), 0)

        side_in_specs.append(pl.BlockSpec(
            (br, w.shape[1]), functools.partial(block_of, last=n_in_blocks - 1)))
        side_out_specs.append(pl.BlockSpec(
            (br, cols_out), functools.partial(block_of, last=n_out_blocks - 1)))
        side_shapes.append(jax.ShapeDtypeStruct((rows_out, cols_out), BF16))
        side_blocks.append((n_in_blocks, n_out_blocks))
    out = pl.pallas_call(
        functools.partial(_attn_prompt_body, side_blocks=tuple(side_blocks), tq=tq, tk=tk, lam0=lam0),
        grid=(batch, heads, nq),
        in_specs=lam_specs + [
            pl.BlockSpec((V_HEAD_DIM, 1), lambda b, h, i: (0, 0)),
            pl.BlockSpec((tq, V_HEAD_DIM), lambda b, h, i: (b * nq + i, h)),
            pl.BlockSpec((seq, V_HEAD_DIM), lambda b, h, i: (b, h)),
            pl.BlockSpec((seq, V_HEAD_DIM), lambda b, h, i: (b, h)),
        ] + side_in_specs,
        out_specs=[pl.BlockSpec((tq, V_HEAD_DIM), lambda b, h, i: (b * nq + i, h))] + side_out_specs,
        out_shape=[jax.ShapeDtypeStruct(q.shape, BF16)] + side_shapes,
        scratch_shapes=[
            pltpu.VMEM((n_kv, tk, V_HEAD_DIM), BF16),
            pltpu.VMEM((n_kv, tk, V_HEAD_DIM), BF16),
            pltpu.VMEM((n_kv + 1, V_HEAD_DIM + BF16_SUBLANES, tk), BF16),
            pltpu.VMEM((2, tk, tq), F32), pltpu.VMEM((2, tk, tq), F32),
            pltpu.VMEM((2, tk, tq), BF16), pltpu.VMEM((2, tk, tq), BF16),
            pltpu.VMEM((2, 1, tq), F32), pltpu.VMEM((2, 1, tq), F32),
            pltpu.VMEM((2, 1, tq), F32), pltpu.VMEM((2, 1, tq), F32),
            pltpu.VMEM((2, 1, tq), F32),
            pltpu.VMEM((2, V_HEAD_DIM + BF16_SUBLANES, tq), F32),
        ],
        compiler_params=_cparams(3),
        name="attn_prompt",
    )(*lams, subln_w.reshape(V_HEAD_DIM, 1), q, k, v, *(w for w, _, _ in side))
    return out[0], out[1:]


def _attn_decode_body(lq1, lk1, lq2, lk2, w_ref, q_ref, kn_ref, vn_ref, kc_ref, vc_ref, o_ref,
                      m_scr, l_scr, acc_scr, *, rows, lam0):
    kb = pl.program_id(2)
    group = kc_ref.shape[1]
    n_q = group * rows

    def stack_heads(ref):
        x = ref[...]
        return jnp.concatenate(
            [x[:, h * V_HEAD_DIM:(h + 1) * V_HEAD_DIM] for h in range(group)], axis=0)

    q = stack_heads(q_ref)
    first_half = lax.broadcasted_iota(jnp.int32, q.shape, 1) < QK_DIM
    zero = jnp.zeros_like(q)
    q_both = jnp.concatenate([jnp.where(first_half, q, zero), jnp.where(first_half, zero, q)], axis=0)

    def accumulate(k2d, v2d, same_head):
        s = lax.dot_general(q_both, k2d, NT_DIMS, preferred_element_type=F32)
        s = jnp.where(jnp.concatenate([same_head, same_head], axis=0), s, -jnp.inf)
        m_old = m_scr[...]
        m_new = jnp.maximum(m_old, jnp.max(s, axis=-1, keepdims=True))
        alpha = jnp.exp2(m_old - m_new)
        p = jnp.exp2(s - m_new)
        l_scr[...] = alpha * l_scr[...] + jnp.sum(p, axis=-1, keepdims=True)
        acc_scr[...] = alpha * acc_scr[...] + jnp.dot(p.astype(BF16), v2d, preferred_element_type=F32)
        m_scr[...] = m_new

    @pl.when(kb == 0)
    def _new_rows():
        m_scr[...] = jnp.full(m_scr.shape, -jnp.inf, F32)
        l_scr[...] = jnp.zeros(l_scr.shape, F32)
        acc_scr[...] = jnp.zeros(acc_scr.shape, F32)
        q_head = lax.broadcasted_iota(jnp.int32, (n_q, n_q), 0) // rows
        k_head = lax.broadcasted_iota(jnp.int32, (n_q, n_q), 1) // rows
        accumulate(stack_heads(kn_ref), stack_heads(vn_ref), q_head == k_head)

    n_keys = kc_ref.shape[0] * group
    q_head = lax.broadcasted_iota(jnp.int32, (n_q, n_keys), 0) // rows
    k_head = lax.broadcasted_iota(jnp.int32, (n_q, n_keys), 1) % group
    accumulate(kc_ref[...].reshape(n_keys, V_HEAD_DIM).astype(BF16),
               vc_ref[...].reshape(n_keys, V_HEAD_DIM).astype(BF16), q_head == k_head)

    @pl.when(kb == pl.num_programs(2) - 1)
    def _finish():
        lam = _lambda_value(lq1, lk1, lq2, lk2, lam0)
        o = acc_scr[...] * (1.0 / l_scr[...])
        o = o[:n_q] - lam * o[n_q:]
        o = _subln(o, w_ref[...], lam0).astype(o_ref.dtype)
        for h in range(group):
            o_ref[:, h * V_HEAD_DIM:(h + 1) * V_HEAD_DIM] = o[h * rows:(h + 1) * rows, :]


def _attn_decode(q, k_new, v_new, cache_k, cache_v, lams, subln_w, *, batch, rows, heads, lam0):
    past = cache_k.shape[1]
    group = V7X_SUBLANES
    t_kv = min(past, 512)
    assert heads % group == 0 and past % t_kv == 0
    gw = group * V_HEAD_DIM
    lam_specs = [pl.BlockSpec((1, QK_DIM), lambda b, g, k: (0, 0))] * 4
    new_spec = pl.BlockSpec((rows, gw), lambda b, g, k: (b, g))
    cache_spec = pl.BlockSpec((None, t_kv, group, V_HEAD_DIM), lambda b, g, k: (b, k, g, 0))
    return pl.pallas_call(
        functools.partial(_attn_decode_body, rows=rows, lam0=lam0),
        grid=(batch, heads // group, past // t_kv),
        in_specs=lam_specs + [
            pl.BlockSpec((1, V_HEAD_DIM), lambda b, g, k: (0, 0)),
            new_spec, new_spec, new_spec, cache_spec, cache_spec,
        ],
        out_specs=new_spec,
        out_shape=jax.ShapeDtypeStruct(q.shape, BF16),
        scratch_shapes=[
            pltpu.VMEM((2 * group * rows, 1), F32),
            pltpu.VMEM((2 * group * rows, 1), F32),
            pltpu.VMEM((2 * group * rows, V_HEAD_DIM), F32),
        ],
        compiler_params=_cparams(3),
        name="attn_decode",
    )(*lams, subln_w, q, k_new, v_new, cache_k, cache_v)


def _split3(x):
    hi = x.astype(BF16)
    r1 = x - hi.astype(F32)
    mid = r1.astype(BF16)
    lo = (r1 - mid.astype(F32)).astype(BF16)
    return hi, mid, lo


def _ssd_body(z_ref, x_ref, bc_ref, dt_ref, *refs, n_sub, valid, heads_per_group):
    y_ref = refs[-4]
    for sub in range(n_sub):
        rs = pl.ds(sub * CHUNK, CHUNK)
        _ssd_chunk(z_ref.at[rs], x_ref.at[rs], bc_ref.at[rs], dt_ref.at[rs], *refs[:-4],
                   y_ref.at[rs], *refs[-3:], is_first=sub == 0, is_last=sub == n_sub - 1,
                   valid=valid, heads_per_group=heads_per_group)


def _ssd_chunk(z_ref, x_ref, bc_ref, dt_ref, tail_ref, st0_ref, cw_ref, cb_ref, dtb_ref, alog_ref,
               dsk_ref, nw_ref, e_ref, tril_ref, y_ref, stout_ref, xpad_scr, st_scr,
               *, is_first, is_last, valid, heads_per_group):
    c = pl.program_id(1)
    L = CHUNK
    inner = z_ref.shape[1]
    gw = heads_per_group * SSD_HEAD_DIM
    n_groups = inner // gw
    gn = D_STATE
    tail_rows = V7X_SUBLANES

    if is_first:
        @pl.when(c == 0)
        def _load_stream_state():
            xpad_scr[0:tail_rows, :] = tail_ref[...]
            st_scr[...] = st0_ref[...].T

    xpad_scr[tail_rows:tail_rows + L, 0:inner] = x_ref[...]
    xpad_scr[tail_rows:tail_rows + L, inner:] = bc_ref[...]
    conv = cb_ref[...]
    first = tail_rows - (CONV_WIDTH - 1)
    for i in range(CONV_WIDTH):
        conv = conv + cw_ref[i:i + 1, :] * xpad_scr[first + i:first + i + L, :]
    xpad_scr[0:tail_rows, :] = xpad_scr[L:L + tail_rows, :]
    act = _silu(conv)
    xs = act[:, :inner]
    bm = act[:, inner:inner + n_groups * gn]
    cm = act[:, inner + n_groups * gn:]

    dtv = dt_ref[...] + dtb_ref[...]
    dt = jnp.maximum(dtv, 0.0) + jnp.log1p(jnp.exp(-jnp.abs(dtv)))
    if valid < L:
        dt = jnp.where(lax.broadcasted_iota(jnp.int32, dt.shape, 0) < valid, dt, 0.0)
    d_a = dt * (-jnp.exp(alog_ref[...]))

    pieces = jnp.concatenate(_split3(d_a), axis=1)
    cs = jnp.dot(tril_ref[...], pieces, preferred_element_type=F32)
    w = d_a.shape[1]
    a_cs = (cs[:, :w] + cs[:, w:2 * w]) + cs[:, 2 * w:]

    both = jnp.concatenate([a_cs, dt], axis=0)
    pieces = jnp.concatenate(_split3(both), axis=0)
    ex = jnp.dot(pieces, e_ref[...], preferred_element_type=F32)
    ex = (ex[:2 * L] + ex[2 * L:4 * L]) + ex[4 * L:]
    acol = ex[:L]
    dtx = ex[L:]

    row = lax.broadcasted_iota(jnp.int32, (L, inner), 0)
    sub = lax.broadcasted_iota(jnp.int32, (L, inner), 1) & (SSD_HEAD_DIM - 1)
    arow = jnp.sum(jnp.where(row == sub, acol, 0.0), axis=0, keepdims=True)
    decay_in = jnp.exp(jnp.where(sub <= row, acol - arow, -jnp.inf))
    a_last = acol[L - 1:L, :]
    decay_to_end = jnp.exp(a_last - acol)
    decay_from_start = jnp.exp(acol)
    chunk_decay = jnp.exp(a_last)

    xdt = xs * dtx
    xdt_bf = xdt.astype(BF16)
    xd_end = (xdt * decay_to_end).astype(BF16)
    bm_t = jnp.concatenate([bm, jnp.zeros_like(bm)], axis=0).T
    zeros_rows = jnp.zeros((L, gw), BF16)
    lane_head = lax.broadcasted_iota(jnp.int32, (L, gw), 1) // SSD_HEAD_DIM

    for g in range(n_groups):
        hs = slice(g * gw, (g + 1) * gw)
        ns = slice(g * gn, (g + 1) * gn)
        b_bf = bm[:, ns].astype(BF16)
        c_bf = cm[:, ns].astype(BF16)
        cb = lax.dot_general(c_bf, jnp.concatenate([b_bf] * heads_per_group, axis=0), NT_DIMS,
                             preferred_element_type=F32)
        lhs = (cb * decay_in[:, hs]).astype(BF16)
        xg = xdt_bf[:, hs]
        zero = jnp.zeros_like(xg)
        block_diag = jnp.concatenate(
            [jnp.where(lane_head == r, xg, zero) for r in range(heads_per_group)], axis=0)
        y = jnp.dot(lhs, block_diag, preferred_element_type=F32)
        st = st_scr[:, hs]
        y = y + jnp.dot(c_bf, st.astype(BF16), preferred_element_type=F32) * decay_from_start[:, hs]
        bt = bm_t[ns, :].astype(BF16)
        st_scr[:, hs] = chunk_decay[:, hs] * st + jnp.dot(
            bt, jnp.concatenate([xd_end[:, hs], zeros_rows], axis=0), preferred_element_type=F32)
        y = y + dsk_ref[:, hs] * xs[:, hs]
        y = y * _silu(z_ref[:, hs])
        y = y * lax.rsqrt(jnp.mean(y * y, axis=-1, keepdims=True) + EPS)
        y_ref[:, hs] = (y * nw_ref[:, hs]).astype(y_ref.dtype)

    if is_last:
        @pl.when(c == pl.num_programs(1) - 1)
        def _store_stream_state():
            stout_ref[...] = st_scr[...].T


def _ssd(zx, dt_raw, tail, st0, p, *, batch, n_chunks, valid):
    rows = zx.shape[0]
    inner = zx.shape[1] // 3
    conv_dim = 2 * inner
    heads = inner // SSD_HEAD_DIM
    L = CHUNK
    assert SSD_HEAD_DIM == CHUNK and conv_dim == inner + 2 * SSD_GROUPS * D_STATE
    expand = np.zeros((V7X_LANES, inner), np.float32)
    for h in range(heads):
        expand[h, h * SSD_HEAD_DIM:(h + 1) * SSD_HEAD_DIM] = 1.0
    tril = np.tril(np.ones((L, L), np.float32))

    n_sub = 2 if n_chunks % 2 == 0 else 1
    n_steps = n_chunks // n_sub

    def row_blk(col):
        return pl.BlockSpec((n_sub * L, inner), lambda b, c, col=col: (b * n_steps + c, col))

    def const(shape):
        return pl.BlockSpec(shape, lambda b, c: (0,) * len(shape))

    return pl.pallas_call(
        functools.partial(_ssd_body, n_sub=n_sub, valid=valid, heads_per_group=heads // SSD_GROUPS),
        grid=(batch, n_steps),
        in_specs=[
            row_blk(0), row_blk(1), row_blk(2),
            pl.BlockSpec((n_sub * L, V7X_LANES), lambda b, c: (b * n_steps + c, 0)),
            pl.BlockSpec((None, V7X_SUBLANES, conv_dim), lambda b, c: (b, 0, 0)),
            pl.BlockSpec((None, inner, D_STATE), lambda b, c: (b, 0, 0)),
            const((CONV_WIDTH, conv_dim)), const((1, conv_dim)),
            const((1, V7X_LANES)), const((1, V7X_LANES)),
            const((1, inner)), const((1, inner)),
            const((V7X_LANES, inner)), const((L, L)),
        ],
        out_specs=[
            pl.BlockSpec((n_sub * L, inner), lambda b, c: (b * n_steps + c, 0)),
            pl.BlockSpec((None, inner, D_STATE), lambda b, c: (b, 0, 0)),
        ],
        out_shape=[
            jax.ShapeDtypeStruct((rows, inner), BF16),
            jax.ShapeDtypeStruct((batch, inner, D_STATE), F32),
        ],
        scratch_shapes=[
            pltpu.VMEM((L + V7X_SUBLANES, conv_dim), F32),
            pltpu.VMEM((D_STATE, inner), F32),
        ],
        compiler_params=_cparams(2),
        name="ssd",
    )(zx, zx, zx, dt_raw, tail, st0, p["conv_w"], p["conv_b"], p["dt_bias"], p["a_log"],
      p["d_skip"], p["ssd_norm_w"], jnp.asarray(expand, BF16), jnp.asarray(tril, BF16))


def _pad_lanes(v, width):
    return jnp.pad(v.astype(F32), (0, width - v.shape[0])).reshape(1, width)


def _prepare_layer(layer, norm1_w, w_in, lambda_q1, lambda_k1, lambda_q2, lambda_k2, subln_w,
                   conv_w, conv_b, dt_bias, A_log, D_skip, ssd_norm_w, w_out, norm2_w,
                   w_gate, w_up, w_down, ff_tile):
    d_model = w_in.shape[1]
    attn_w = 3 * (w_out.shape[1] // 2)
    inner = w_out.shape[1] // 2
    main_cols = attn_w + inner + 2 * inner
    cast_rows = 128
    w_in_t = jnp.swapaxes(w_in[layer], 0, 1)
    w_in_bf = _cast_rows(w_in_t, main_cols, 2 * cast_rows)
    n_heads = w_in.shape[2] - main_cols
    w_dt = jnp.pad(w_in_t[main_cols:, :].astype(BF16), ((0, V7X_LANES - n_heads), (0, 0)))
    d_ff = w_gate.shape[2]
    ff_width = d_ff + (-d_ff % ff_tile)
    return {
        "lam0": _lambda_init(layer),
        "norm1_w": norm1_w[layer], "norm2_w": norm2_w[layer],
        "w_in": w_in_bf, "w_dt": w_dt, "main_cols": main_cols,
        "lams": [v[layer].reshape(1, -1).astype(F32)
                 for v in (lambda_q1, lambda_k1, lambda_q2, lambda_k2)],
        "subln_w": subln_w[layer].reshape(1, -1).astype(F32),
        "conv_w": conv_w[layer].astype(F32), "conv_b": conv_b[layer].reshape(1, -1).astype(F32),
        "dt_bias": _pad_lanes(dt_bias[layer], V7X_LANES),
        "a_log": _pad_lanes(A_log[layer], V7X_LANES),
        "d_skip": jnp.repeat(D_skip[layer].astype(F32), SSD_HEAD_DIM).reshape(1, -1),
        "ssd_norm_w": ssd_norm_w[layer].reshape(1, -1).astype(F32),
        "late_weights": [("w_out", w_out[layer], 2 * inner, d_model),
                         ("w_gate", w_gate[layer], d_model, ff_width),
                         ("w_up", w_up[layer], d_model, ff_width),
                         ("w_down", w_down[layer], ff_width, d_model)],
        "d_model": d_model, "inner": inner,
    }


def _cast_late_weights(p):
    cast_rows = 128
    for name, w, rows_out, cols_out in p["late_weights"]:
        if rows_out == w.shape[0]:
            p[name] = _cast_cols(w, w.shape[1], cols_out, cast_rows)
        else:
            p[name] = _cast_rows(w, rows_out, 2 * cast_rows)


def _tiles(m):
    return 1024 if m % 1024 == 0 else m


def _in_proj(x2d, p):
    m = x2d.shape[0]
    inner = p["inner"]
    tm = _tiles(m)
    h = _rmsnorm(x2d, p["norm1_w"], BF16, min(256, m))
    proj = functools.partial(_matmul, [h], tm=tm, b_transposed=True)
    q, = proj([p["w_in"]], n=inner, tn=1024, b_col_off=0,
              outs=[_Out(BF16, ATTN_SCALE * LOG2_E)], name="proj_q")
    k32, kbf = proj([p["w_in"]], n=inner, tn=1024, b_col_off=inner,
                    outs=[_Out(F32, by_head=True), _Out(BF16)], name="proj_k")
    v32, vbf = proj([p["w_in"]], n=inner, tn=1024, b_col_off=2 * inner,
                    outs=[_Out(F32, by_head=True), _Out(BF16)], name="proj_v")
    zx, = proj([p["w_in"]], n=3 * inner, tn=1024, b_col_off=3 * inner,
               outs=[_Out(F32)], name="proj_zx")
    dt, = proj([p["w_dt"]], n=V7X_LANES, tn=V7X_LANES, outs=[_Out(F32)], name="proj_dt")
    return q, k32, kbf, v32, vbf, zx, dt


def _out_and_ffn(x2d, attn, ssd_y, p, final_w):
    m = x2d.shape[0]
    d_model = p["d_model"]
    tm = _tiles(m)
    out_proj = functools.partial(
        _matmul, [attn, ssd_y], [p["w_out"], p["w_out"]], b_row_blocks=[0, 1], n=d_model,
        tm=tm, outs=[_Out(F32)], residual=x2d, name="out_proj")
    ffn_up = functools.partial(_ffn_up, tm=tm, tn=512 if tm == 1024 else 1024)
    ffn_down = functools.partial(_matmul_acc, tm=tm, tn=min(1024, d_model), tk=p["w_down"].shape[0] // 4)
    half = m // 2
    if m % (2 * tm) != 0:
        x1, = out_proj(tn=min(1024, d_model))
        h2 = _rmsnorm(x1, p["norm2_w"], BF16, min(256, m))
        x2, = ffn_down(ffn_up(h2, p["w_gate"], p["w_up"])[0], p["w_down"], x1)
        y = None if final_w is None else _rmsnorm(x2, final_w, F32, min(256, m))
        return x2, y

    def norm_of(x, w, dtype, total_rows):
        return _SideNorm(x, w, 0, half, dtype, total_rows, 0)

    tn_out = min(512, d_model)
    x1_a, = out_proj(tn=min(1024, d_model), row_off=0, rows=half)
    x1_b, h2_a = out_proj(tn=tn_out, row_off=half, rows=half,
                          side_norm=norm_of(x1_a, p["norm2_w"], BF16, half))
    act_a, h2_b = ffn_up(h2_a, p["w_gate"], p["w_up"], side_norm=norm_of(x1_b, p["norm2_w"], BF16, half))
    act_b, = ffn_up(h2_b, p["w_gate"], p["w_up"])
    x2_a, = ffn_down(act_a, p["w_down"], x1_a)
    if final_w is None:
        x2_b, = ffn_down(act_b, p["w_down"], x1_b)
        return jnp.concatenate([x2_a, x2_b], axis=0), None
    x2_b, y = ffn_down(act_b, p["w_down"], x1_b, side_norm=norm_of(x2_a, final_w, F32, m))
    return None, _rmsnorm_into(x2_b, final_w, y, half, 256)


def kernel(x_prompt, x_sample, cache_k, cache_v, state_conv, state_ssm, norm1_w, w_in, lambda_q1, lambda_k1, lambda_q2, lambda_k2, subln_w, conv_w, conv_b, dt_bias, A_log, D_skip, ssd_norm_w, w_out, norm2_w, w_gate, w_up, w_down, final_norm_w):
    depth = w_in.shape[0]
    bp, seq, d_model = x_prompt.shape
    bs, rows, _ = x_sample.shape
    heads = cache_k.shape[3]
    inner = w_out.shape[1] // 2
    attn_w = heads * V_HEAD_DIM
    conv_dim = conv_w.shape[2]
    ssd_heads = state_ssm.shape[2]
    assert seq % CHUNK == 0 and rows <= CHUNK and rows >= CONV_WIDTH - 1 and attn_w == inner

    xp = x_prompt.reshape(bp * seq, d_model)
    xs = x_sample.reshape(bs * rows, d_model)
    outs_p = [[], [], [], []]
    outs_s = [[], [], [], []]
    for layer in range(depth):
        p = _prepare_layer(layer, norm1_w, w_in, lambda_q1, lambda_k1, lambda_q2, lambda_k2, subln_w,
                           conv_w, conv_b, dt_bias, A_log, D_skip, ssd_norm_w, w_out, norm2_w,
                           w_gate, w_up, w_down, ff_tile=1024)
        lam0 = p["lam0"]

        q, k32, kbf, v32, vbf, zx, dt = _in_proj(xp, p)
        attn_tk = 512
        n_steps = bp * heads * (seq // (2 * attn_tk))
        late = p["late_weights"]
        ride_along = all(_side_block_rows(w.shape[0], rows_out, n_steps) is not None
                         and w.shape[1] % V7X_LANES == 0 for _, w, rows_out, _ in late)
        attn, casts = _attn_prompt(q, kbf, vbf, p["lams"], p["subln_w"],
                                   [(w, r, c) for _, w, r, c in late] if ride_along else [],
                                   batch=bp, seq=seq, heads=heads, tk=attn_tk, lam0=lam0)
        if ride_along:
            p.update({name: c for (name, _, _, _), c in zip(late, casts)})
        else:
            _cast_late_weights(p)
        ssd_y, st = _ssd(zx, dt, jnp.zeros((bp, V7X_SUBLANES, conv_dim), F32),
                         jnp.zeros((bp, inner, D_STATE), F32), p,
                         batch=bp, n_chunks=seq // CHUNK, valid=CHUNK)
        outs_p[0].append(k32.reshape(bp, seq, heads, V_HEAD_DIM))
        outs_p[1].append(v32.reshape(bp, seq, heads, V_HEAD_DIM))
        outs_p[2].append(zx.reshape(bp, seq, 3 * inner)[:, seq - (CONV_WIDTH - 1):, inner:])
        outs_p[3].append(st.reshape(bp, ssd_heads, SSD_HEAD_DIM, D_STATE))
        xp, y_prompt = _out_and_ffn(xp, attn, ssd_y, p, final_norm_w if layer == depth - 1 else None)

        q, k32, kbf, v32, vbf, zx, dt = _in_proj(xs, p)
        attn = _attn_decode(q, kbf, vbf, cache_k[layer], cache_v[layer],
                            p["lams"], p["subln_w"], batch=bs, rows=rows, heads=heads, lam0=lam0)
        pad_chunk = ((0, 0), (0, CHUNK - rows), (0, 0))
        zx_c = jnp.pad(zx.reshape(bs, rows, 3 * inner), pad_chunk).reshape(bs * CHUNK, 3 * inner)
        dt_c = jnp.pad(dt.reshape(bs, rows, V7X_LANES), pad_chunk).reshape(bs * CHUNK, V7X_LANES)
        tail = jnp.pad(state_conv[layer].astype(F32),
                       ((0, 0), (V7X_SUBLANES - (CONV_WIDTH - 1), 0), (0, 0)))
        ssd_y, st = _ssd(zx_c, dt_c, tail, state_ssm[layer].astype(F32).reshape(bs, inner, D_STATE), p,
                         batch=bs, n_chunks=1, valid=rows)
        ssd_y = ssd_y.reshape(bs, CHUNK, inner)[:, :rows].reshape(bs * rows, inner)
        outs_s[0].append(k32.reshape(bs, rows, heads, V_HEAD_DIM))
        outs_s[1].append(v32.reshape(bs, rows, heads, V_HEAD_DIM))
        outs_s[2].append(zx.reshape(bs, rows, 3 * inner)[:, rows - (CONV_WIDTH - 1):, inner:])
        outs_s[3].append(st.reshape(bs, ssd_heads, SSD_HEAD_DIM, D_STATE))
        xs, y_sample = _out_and_ffn(xs, attn, ssd_y, p, final_norm_w if layer == depth - 1 else None)

    y_prompt = y_prompt.reshape(bp, seq, d_model)
    y_sample = y_sample.reshape(bs, rows, d_model)
    return (y_prompt, y_sample, *(jnp.stack(o) for o in outs_p), *(jnp.stack(o) for o in outs_s))
```
